```python
import jax, jax.numpy as jnp
from jax import lax
import numpy as np

D_MODEL = 2048
BATCH = 1
SEQ = 8192
DEPTH = 2
DEC_BATCH = 16
DEC_SEQ = 16
PAST_LEN = 4096

CHUNK = 64
D_CONV = 1024
CONV_GROUPS = 8
CONV_W = 3
D_SGU = 1024
SGU_GROUPS = 8
SGU_GROUP_DIM = D_SGU // SGU_GROUPS
SGU_CHUNK = 128
N_EXPERTS = 16
N_EXPERT_GROUPS = 4
EXPERTS_PER_GROUP = N_EXPERTS // N_EXPERT_GROUPS
TOP_K = 2
D_EXPERT = 1024
LN_EPS = 1e-5
ALPHA = (2.0 * DEPTH) ** 0.25
BETA = (8.0 * DEPTH) ** -0.25
D_IN_PROJ = 3 * D_CONV + 2 * D_SGU + 2 * D_MODEL
SPLITS = (D_CONV, 2 * D_CONV, 3 * D_CONV, 3 * D_CONV + 2 * D_SGU, 3 * D_CONV + 2 * D_SGU + D_MODEL)

kernel_name = "streaming_gated_conv_sgu_grouped_moe"


def layer_norm(x, g, b):
    xf = x.astype(jnp.float32)
    mu = jnp.mean(xf, axis=-1, keepdims=True)
    var = jnp.mean(jnp.square(xf - mu), axis=-1, keepdims=True)
    y = (xf - mu) * lax.rsqrt(var + LN_EPS) * g.astype(jnp.float32) + b.astype(jnp.float32)
    return y.astype(x.dtype)


def conv_branch(b_gate, c_gate, h, hist, conv_w, conv_b):
    T = h.shape[1]
    h_in = c_gate * h
    buf = jnp.concatenate([hist.astype(h_in.dtype), h_in], axis=1)
    out = conv_b
    for k in range(CONV_W):
        out = out + conv_w[k] * buf[:, k:k + T]
    return b_gate * out, buf[:, -(CONV_W - 1):]


def sgu_branch(u, v, ln_g, ln_b, w_s, b_s):
    Bt, T, _ = v.shape
    v = layer_norm(v, ln_g, ln_b)
    L = min(T, SGU_CHUNK)
    n = T // L
    pos = jnp.arange(L) // CHUNK
    mask = pos[None, :] <= pos[:, None]
    w = jnp.where(mask[None], w_s[:, :L, :L], 0.0)
    vb = v.reshape(Bt, n, L, SGU_GROUPS, SGU_GROUP_DIM)
    s = jnp.einsum('gts,bnsgc->bntgc', w, vb) + b_s[:, :L].T[None, None, :, :, None]
    return u * s.reshape(Bt, T, D_SGU), v


def moe(x, router_w, router_bias, w_gate, w_up, w_down):
    Bt, T, D = x.shape
    xt = x.reshape(-1, D)
    scores = jax.nn.softmax((xt @ router_w).astype(jnp.float32), axis=-1)
    sel = scores + router_bias.astype(jnp.float32)
    grp = sel.reshape(-1, N_EXPERT_GROUPS, EXPERTS_PER_GROUP)
    grp_score = lax.top_k(grp, TOP_K)[0].sum(-1)
    g_idx = jnp.argmax(grp_score, axis=-1)
    in_group = (jnp.arange(N_EXPERTS) // EXPERTS_PER_GROUP)[None, :] == g_idx[:, None]
    _, e_idx = lax.top_k(jnp.where(in_group, sel, -jnp.inf), TOP_K)
    gw = jnp.take_along_axis(scores, e_idx, axis=-1)
    gw = gw / jnp.sum(gw, axis=-1, keepdims=True)
    comb = jnp.sum(jax.nn.one_hot(e_idx, N_EXPERTS, dtype=jnp.float32) * gw[..., None], axis=1)
    h = jax.nn.silu(jnp.einsum('nd,edf->nef', xt, w_gate)) * jnp.einsum('nd,edf->nef', xt, w_up)
    h = h * comb[..., None].astype(h.dtype)
    y = jnp.einsum('nef,efd->nd', h, w_down)
    return y.reshape(Bt, T, D)


def trunk_layer(x, conv_hist, p, router_w, router_bias):
    (w_in, b_in, conv_w, conv_b, sgu_ln_g, sgu_ln_b, w_s, b_s,
     w_a_out, w_b_out, w_o, ln1_g, ln1_b, w_gate, w_up, w_down, ln2_g, ln2_b) = p
    proj = x @ w_in + b_in
    b_gate, c_gate, h, uv, gate_a, gate_b = jnp.split(proj, SPLITS, axis=-1)
    u, v = jnp.split(jax.nn.gelu(uv), 2, axis=-1)
    ya, conv_state = conv_branch(b_gate, c_gate, h, conv_hist, conv_w, conv_b)
    yb, v_rows = sgu_branch(u, v, sgu_ln_g, sgu_ln_b, w_s, b_s)
    merged = jax.nn.sigmoid(gate_a) * (ya @ w_a_out) + jax.nn.sigmoid(gate_b) * (yb @ w_b_out)
    x = layer_norm(ALPHA * x + merged @ w_o, ln1_g, ln1_b)
    x = layer_norm(ALPHA * x + moe(x, router_w, router_bias, w_gate, w_up, w_down), ln2_g, ln2_b)
    return x, conv_state, v_rows


def setup_inputs(seed: int = 0) -> dict:
    key = jax.random.key(seed)
    ks = jax.random.split(key, 32)
    f32 = jnp.float32
    nrm = lambda k, shape, s: jax.random.normal(k, shape, f32) * s
    return {
        "x_prompt": nrm(ks[0], (BATCH, SEQ, D_MODEL), 1.0),
        "x_sample": nrm(ks[1], (DEC_BATCH, DEC_SEQ, D_MODEL), 1.0),
        "state_conv": nrm(ks[2], (DEPTH, DEC_BATCH, CONV_W - 1, D_CONV), 1.0),
        "w_in": nrm(ks[3], (DEPTH, D_MODEL, D_IN_PROJ), D_MODEL ** -0.5),
        "b_in": nrm(ks[4], (DEPTH, D_IN_PROJ), 0.02),
        "conv_w": nrm(ks[5], (DEPTH, CONV_W, D_CONV), CONV_W ** -0.5),
        "conv_b": nrm(ks[6], (DEPTH, D_CONV), 0.02),
        "sgu_ln_g": 1.0 + nrm(ks[7], (DEPTH, D_SGU), 0.02),
        "sgu_ln_b": nrm(ks[8], (DEPTH, D_SGU), 0.02),
        "w_spatial": nrm(ks[9], (DEPTH, SGU_GROUPS, SGU_CHUNK, SGU_CHUNK), SGU_CHUNK ** -0.5),
        "b_spatial": 1.0 + nrm(ks[10], (DEPTH, SGU_GROUPS, SGU_CHUNK), 0.02),
        "w_a_out": nrm(ks[11], (DEPTH, D_CONV, D_MODEL), BETA * D_CONV ** -0.5),
        "w_b_out": nrm(ks[12], (DEPTH, D_SGU, D_MODEL), BETA * D_SGU ** -0.5),
        "w_o": nrm(ks[13], (DEPTH, D_MODEL, D_MODEL), BETA * D_MODEL ** -0.5),
        "ln1_g": 1.0 + nrm(ks[14], (DEPTH, D_MODEL), 0.02),
        "ln1_b": nrm(ks[15], (DEPTH, D_MODEL), 0.02),
        "router_w": nrm(ks[16], (D_MODEL, N_EXPERTS), D_MODEL ** -0.5),
        "router_bias": nrm(ks[17], (N_EXPERTS,), 0.01),
        "w_gate": nrm(ks[18], (DEPTH, N_EXPERTS, D_MODEL, D_EXPERT), D_MODEL ** -0.5),
        "w_up": nrm(ks[19], (DEPTH, N_EXPERTS, D_MODEL, D_EXPERT), BETA * D_MODEL ** -0.5),
        "w_down": nrm(ks[20], (DEPTH, N_EXPERTS, D_EXPERT, D_MODEL), BETA * D_EXPERT ** -0.5),
        "ln2_g": 1.0 + nrm(ks[21], (DEPTH, D_MODEL), 0.02),
        "ln2_b": nrm(ks[22], (DEPTH, D_MODEL), 0.02),
    }


def reference(x_prompt, x_sample, state_conv, w_in, b_in, conv_w, conv_b, sgu_ln_g, sgu_ln_b,
              w_spatial, b_spatial, w_a_out, w_b_out, w_o, ln1_g, ln1_b, router_w, router_bias,
              w_gate, w_up, w_down, ln2_g, ln2_b):
    xp, xs = x_prompt, x_sample
    zero_hist = jnp.zeros((xp.shape[0], CONV_W - 1, D_CONV), xp.dtype)
    conv_p, conv_s, v_s = [], [], []
    for l in range(DEPTH):
        p = (w_in[l], b_in[l], conv_w[l], conv_b[l], sgu_ln_g[l], sgu_ln_b[l], w_spatial[l], b_spatial[l],
             w_a_out[l], w_b_out[l], w_o[l], ln1_g[l], ln1_b[l], w_gate[l], w_up[l], w_down[l],
             ln2_g[l], ln2_b[l])
        xp, cp, _ = trunk_layer(xp, zero_hist, p, router_w, router_bias)
        xs, cs, vs = trunk_layer(xs, state_conv[l], p, router_w, router_bias)
        conv_p.append(cp)
        conv_s.append(cs)
        v_s.append(vs)
    return (xp, xs, jnp.stack(conv_p), jnp.stack(conv_s), jnp.stack(v_s))
```

```python
import functools
from typing import NamedTuple

import jax
import jax.numpy as jnp
from jax import lax
from jax.experimental import pallas as pl
from jax.experimental.pallas import tpu as pltpu

CHUNK = 64
SGU_CHUNK = 128
SGU_GROUPS = 8
N_EXPERT_GROUPS = 4
TOP_K = 2
LN_EPS = 1e-5
CONV_W = 3

LANES = 128
SUBLANES = 8
MXU_DIM = 256
COL_BLOCK = 512
ROW_SUB = 256
MIX_ROWS = 1024
TOK_ROWS = 256
VMEM_LIMIT = 56 * 1024 * 1024

F32 = jnp.float32
BF16 = jnp.bfloat16


class MixCfg(NamedTuple):
    tm: int
    stream_rows: int
    nb: int
    ns: int
    ng: int
    emit_v: bool
    n_alias: int


def _ln(z, g, b):
    mu = jnp.mean(z, axis=-1, keepdims=True)
    d = z - mu
    var = jnp.mean(d * d, axis=-1, keepdims=True)
    return d * lax.rsqrt(var + LN_EPS) * g + b


def _conv_rows(hin, p2, p1, cw_ref, jj):
    rows = lax.broadcasted_iota(jnp.int32, hin.shape, 0)
    r1 = pltpu.roll(hin, 1, 0)
    r2 = pltpu.roll(hin, 2, 0)
    sh1 = jnp.where(rows == 0, p1, r1)
    sh2 = jnp.where(rows == 0, p2, jnp.where(rows == 1, p1, r2))
    out = cw_ref[jj, 3:4, :] + cw_ref[jj, 0:1, :] * sh2
    out = out + cw_ref[jj, 1:2, :] * sh1
    return out + cw_ref[jj, 2:3, :] * hin


def _mixer_kernel(*refs, cfg: MixCfg):
    (x_ref, w_ref, bias_ref, prev_ref, cw_ref, lng_ref, lnb_ref, wt_ref, bt_ref) = refs[:9]
    outs = refs[9 + cfg.n_alias:]
    ya_ref, yb_ref, ga_ref, gb_ref, cs_ref = outs[:5]
    k = 5
    v_ref = None
    if cfg.emit_v:
        v_ref = outs[k]
        k += 1
    bsc, csc, usc, vsc, carry, wm_sc = outs[k:]

    i = pl.program_id(0)
    j = pl.program_id(1)
    nb, ns, ng = cfg.nb, cfg.ns, cfg.ng
    p_c, p_h, p_u, p_v = nb, 2 * nb, 3 * nb, 3 * nb + ns
    p_ga = p_v + ns
    p_gb = p_ga + ng
    n_sub = cfg.tm // ROW_SUB
    streams_per_sub = max(ROW_SUB // cfg.stream_rows, 1)
    cb = w_ref.shape[1]
    gpb = cb // LANES

    def sub_rows(r):
        if n_sub == 1:
            return pl.ds(0, ROW_SUB)
        return pl.ds(pl.multiple_of(r * ROW_SUB, ROW_SUB), ROW_SUB)

    def proj(r):
        rows = sub_rows(r)
        acc = jnp.dot(x_ref[rows, :], w_ref[...], preferred_element_type=F32)
        return rows, acc + bias_ref[...]

    def for_sub(body):
        if n_sub == 1:
            body(0)
            return

        def step(r, c):
            body(r)
            return c
        lax.fori_loop(0, n_sub, step, 0)

    @pl.when(jnp.logical_and(i == 0, j == 0))
    def _():
        carry[...] = prev_ref[0]

    @pl.when(j < p_c)
    def _():
        def body(r):
            rows, a = proj(r)
            bsc[j, rows, :] = a
        for_sub(body)

    @pl.when(jnp.logical_and(j >= p_c, j < p_h))
    def _():
        def body(r):
            rows, a = proj(r)
            csc[j - p_c, rows, :] = a
        for_sub(body)

    @pl.when(jnp.logical_and(j >= p_h, j < p_u))
    def _():
        jj = j - p_h

        def body(r):
            rows, a = proj(r)
            hin = csc[jj, rows, :] * a
            if streams_per_sub == 1:
                out = _conv_rows(hin, carry[jj, 6:7, :], carry[jj, 7:8, :], cw_ref, jj)
                ya_ref[rows, :] = (bsc[jj, rows, :] * out).astype(ya_ref.dtype)
                carry[jj] = hin[ROW_SUB - SUBLANES:, :]
                cs_ref[0, jj] = hin[ROW_SUB - SUBLANES:, :]
            else:
                sr = cfg.stream_rows
                for s in range(streams_per_sub):
                    hs = hin[s * sr:(s + 1) * sr, :]
                    out = _conv_rows(hs, prev_ref[s, jj, 6:7, :], prev_ref[s, jj, 7:8, :], cw_ref, jj)
                    row0 = r * ROW_SUB + s * sr
                    srow = pl.ds(row0 if n_sub == 1 else pl.multiple_of(row0, sr), sr)
                    ya_ref[srow, :] = (bsc[jj, srow, :] * out).astype(ya_ref.dtype)
                    cs_ref[s, jj] = hs[sr - SUBLANES:, :]
        for_sub(body)

    @pl.when(jnp.logical_and(j >= p_u, j < p_v))
    def _():
        def body(r):
            rows, a = proj(r)
            usc[j - p_u, rows, :] = jax.nn.gelu(a).astype(usc.dtype)
        for_sub(body)

    @pl.when(jnp.logical_and(j >= p_v, j < p_ga))
    def _():
        def body(r):
            rows, a = proj(r)
            vsc[j - p_v, rows, :] = jax.nn.gelu(a)
        for_sub(body)

    @pl.when(j == p_ga - 1)
    def _():
        seq = min(cfg.stream_rows, SGU_CHUNK)
        t = lax.broadcasted_iota(jnp.int32, (ROW_SUB, ROW_SUB), 0)
        s = lax.broadcasted_iota(jnp.int32, (ROW_SUB, ROW_SUB), 1)
        same = (t // seq) == (s // seq)
        causal = ((s % seq) // CHUNK) <= ((t % seq) // CHUNK)
        keep = jnp.logical_and(same, causal)
        for g in range(SGU_GROUPS):
            wm_sc[g] = jnp.where(keep, wt_ref[g], 0.0).astype(wm_sc.dtype)

        ds = ns * cb

        def body(r):
            rows = sub_rows(r)
            parts = [vsc[c, rows, :] for c in range(ns)]
            mu = sum(jnp.sum(p, axis=-1, keepdims=True) for p in parts) / ds
            cen = [p - mu for p in parts]
            var = sum(jnp.sum(c * c, axis=-1, keepdims=True) for c in cen) / ds
            inv = lax.rsqrt(var + LN_EPS)
            for c in range(ns):
                cols = slice(c * cb, (c + 1) * cb)
                vn = cen[c] * inv * lng_ref[:, cols] + lnb_ref[:, cols]
                if v_ref is not None:
                    v_ref[rows, cols] = vn
                vnb = vn.astype(BF16)
                for q in range(gpb):
                    g = c * gpb + q
                    lanes = slice(q * LANES, (q + 1) * LANES)
                    sp = jnp.dot(wm_sc[g], vnb[:, lanes], preferred_element_type=F32) + bt_ref[g]
                    u = usc[c, rows, lanes].astype(F32)
                    yb_ref[rows, g * LANES:(g + 1) * LANES] = (u * sp).astype(yb_ref.dtype)
        for_sub(body)

    @pl.when(jnp.logical_and(j >= p_ga, j < p_gb))
    def _():
        def body(r):
            rows, a = proj(r)
            ga_ref[rows, :] = jax.nn.sigmoid(a).astype(ga_ref.dtype)
        for_sub(body)

    @pl.when(j >= p_gb)
    def _():
        def body(r):
            rows, a = proj(r)
            gb_ref[rows, :] = jax.nn.sigmoid(a).astype(gb_ref.dtype)
        for_sub(body)


def _mixer_call(xb, w_in, b_in, prev, cw, lng, lnb, wt, bt, cfg, n_rows_total, row_block0, aliased):
    rows, d = xb.shape
    n_col = w_in.shape[1]
    cb = COL_BLOCK
    nb, ns, ng = cfg.nb, cfg.ns, cfg.ng
    dc, dsg = nb * cb, ns * cb
    n_j = n_col // cb
    n_i = rows // cfg.tm
    n_streams = prev.shape[0]
    p_h, p_ga, p_gb = 2 * nb, 3 * nb + 2 * ns, 3 * nb + 2 * ns + ng
    rb0 = row_block0

    def const(shape):
        return pl.BlockSpec(shape, lambda i, j: (0,) * len(shape))

    in_specs = [
        pl.BlockSpec((cfg.tm, d), lambda i, j: (i, 0)),
        pl.BlockSpec((d, cb), lambda i, j: (0, j)),
        pl.BlockSpec((1, cb), lambda i, j: (0, j)),
        const(prev.shape), const(cw.shape), const(lng.shape), const(lnb.shape),
        const(wt.shape), const(bt.shape),
    ] + [pl.BlockSpec(memory_space=pl.ANY)] * len(aliased)
    out_shape = [
        jax.ShapeDtypeStruct((n_rows_total, dc), BF16),
        jax.ShapeDtypeStruct((n_rows_total, dsg), BF16),
        jax.ShapeDtypeStruct((n_rows_total, d), F32),
        jax.ShapeDtypeStruct((n_rows_total, d), F32),
        jax.ShapeDtypeStruct((n_streams, nb, SUBLANES, cb), F32),
    ]
    out_specs = [
        pl.BlockSpec((cfg.tm, cb), lambda i, j: (rb0 + i, jnp.clip(j - p_h, 0, nb - 1))),
        pl.BlockSpec((cfg.tm, dsg), lambda i, j: (rb0 + i, 0)),
        pl.BlockSpec((cfg.tm, cb), lambda i, j: (rb0 + i, jnp.clip(j - p_ga, 0, ng - 1))),
        pl.BlockSpec((cfg.tm, cb), lambda i, j: (rb0 + i, jnp.clip(j - p_gb, 0, ng - 1))),
        const((n_streams, nb, SUBLANES, cb)),
    ]
    if cfg.emit_v:
        out_shape.append(jax.ShapeDtypeStruct((rows, dsg), F32))
        out_specs.append(pl.BlockSpec((cfg.tm, dsg), lambda i, j: (i, 0)))
    scratch = [
        pltpu.VMEM((nb, cfg.tm, cb), F32),
        pltpu.VMEM((nb, cfg.tm, cb), F32),
        pltpu.VMEM((ns, cfg.tm, cb), F32),
        pltpu.VMEM((ns, cfg.tm, cb), F32),
        pltpu.VMEM((nb, SUBLANES, cb), F32),
        pltpu.VMEM((SGU_GROUPS, ROW_SUB, ROW_SUB), BF16),
    ]
    n_fixed = 9
    return pl.pallas_call(
        functools.partial(_mixer_kernel, cfg=cfg),
        grid=(n_i, n_j),
        in_specs=in_specs,
        out_specs=out_specs,
        out_shape=out_shape,
        scratch_shapes=scratch,
        input_output_aliases={n_fixed + a: a for a in range(len(aliased))},
        compiler_params=pltpu.CompilerParams(
            dimension_semantics=("arbitrary", "arbitrary"), vmem_limit_bytes=VMEM_LIMIT),
        name="mixer_sample" if aliased else "mixer_prompt",
    )(xb, w_in, b_in, prev, cw, lng, lnb, wt, bt, *aliased)


def _merge_kernel(ya_ref, yb_ref, ga_ref, gb_ref, x_ref, wa_ref, wb_ref, wo_ref, g1_ref, b1_ref,
                  rw_ref, rb_ref, x1_ref, x1b_ref, eidx_ref, gw_ref, rank_ref, cnt_ref, run, *,
                  alpha, n_experts):
    i = pl.program_id(0)
    tm = x_ref.shape[0]
    epg = n_experts // N_EXPERT_GROUPS

    @pl.when(i == 0)
    def _():
        run[...] = jnp.zeros_like(run)

    a = jnp.dot(ya_ref[...], wa_ref[...], preferred_element_type=F32)
    b = jnp.dot(yb_ref[...], wb_ref[...], preferred_element_type=F32)
    merged = ga_ref[...].astype(F32) * a + gb_ref[...].astype(F32) * b
    z = alpha * x_ref[...] + jnp.dot(merged.astype(BF16), wo_ref[...], preferred_element_type=F32)
    x1 = _ln(z, g1_ref[...], b1_ref[...])
    x1_ref[...] = x1
    x1b_ref[...] = x1.astype(x1b_ref.dtype)

    logits = jnp.dot(x1.astype(BF16), rw_ref[...], preferred_element_type=F32)
    lt = logits.T[:n_experts, :]
    ex = jnp.exp(lt - jnp.max(lt, axis=0, keepdims=True))
    sc = ex / jnp.sum(ex, axis=0, keepdims=True)
    sel = sc + rb_ref[...]
    sel_r = [sel[e:e + 1, :] for e in range(n_experts)]
    sc_r = [sc[e:e + 1, :] for e in range(n_experts)]

    def pair_max(v):
        best = None
        for p in range(len(v)):
            for q in range(p + 1, len(v)):
                s = v[p] + v[q]
                best = s if best is None else jnp.maximum(best, s)
        return best

    grp = [pair_max(sel_r[g * epg:(g + 1) * epg]) for g in range(N_EXPERT_GROUPS)]
    gi = jnp.zeros((1, tm), jnp.int32)
    gbest = grp[0]
    for g in range(1, N_EXPERT_GROUPS):
        better = grp[g] > gbest
        gi = jnp.where(better, g, gi)
        gbest = jnp.where(better, grp[g], gbest)

    def pick(rows_, idx):
        out = rows_[-1]
        for c in range(len(rows_) - 2, -1, -1):
            out = jnp.where(idx == c, rows_[c], out)
        return out

    vk = [pick([sel_r[g * epg + q] for g in range(N_EXPERT_GROUPS)], gi) for q in range(epg)]
    pk = [pick([sc_r[g * epg + q] for g in range(N_EXPERT_GROUPS)], gi) for q in range(epg)]

    i1 = jnp.zeros((1, tm), jnp.int32)
    b1 = vk[0]
    for q in range(1, epg):
        better = vk[q] > b1
        i1 = jnp.where(better, q, i1)
        b1 = jnp.where(better, vk[q], b1)
    i2 = jnp.zeros((1, tm), jnp.int32)
    b2 = jnp.full((1, tm), -jnp.inf, F32)
    for q in range(epg):
        cand = jnp.logical_and(i1 != q, vk[q] > b2)
        i2 = jnp.where(cand, q, i2)
        b2 = jnp.where(cand, vk[q], b2)
    p1 = pick(pk, i1)
    p2 = pick(pk, i2)
    den = p1 + p2
    e1 = gi * epg + i1
    e2 = gi * epg + i2

    eio = lax.broadcasted_iota(jnp.int32, (n_experts, tm), 0)
    oh1 = (eio == e1).astype(F32)
    oh2 = (eio == e2).astype(F32)
    ts = lax.broadcasted_iota(jnp.int32, (tm, tm), 0)
    tt = lax.broadcasted_iota(jnp.int32, (tm, tm), 1)
    upper = (ts <= tt).astype(BF16)
    inc1 = jnp.dot(oh1.astype(BF16), upper, preferred_element_type=F32)
    inc2 = jnp.dot(oh2.astype(BF16), upper, preferred_element_type=F32)
    tot1 = jnp.sum(oh1, axis=1, keepdims=True)
    tot2 = jnp.sum(oh2, axis=1, keepdims=True)
    base = run[:, 0:1]
    r1 = jnp.sum(oh1 * (inc1 - 1.0 + base), axis=0, keepdims=True)
    r2 = jnp.sum(oh2 * (inc2 - 1.0 + base + tot1), axis=0, keepdims=True)
    new_run = run[...] + tot1 + tot2
    run[...] = new_run
    cnt_ref[...] = new_run

    eidx_ref[...] = jnp.zeros_like(eidx_ref)
    eidx_ref[0:1, :] = e1
    eidx_ref[1:2, :] = e2
    gw_ref[...] = jnp.zeros_like(gw_ref)
    gw_ref[0:1, :] = p1 / den
    gw_ref[1:2, :] = p2 / den
    rank_ref[...] = jnp.zeros_like(rank_ref)
    rank_ref[0:1, :] = r1.astype(jnp.int32)
    rank_ref[1:2, :] = r2.astype(jnp.int32)


def _merge_call(ya, yb, ga, gb, x, wa, wb, wo, g1, b1, rw, rb, alpha, n_experts):
    n, d = x.shape
    tm = TOK_ROWS
    dc, dsg = ya.shape[1], yb.shape[1]

    def rows(c):
        return pl.BlockSpec((tm, c), lambda i: (i, 0))

    def const(shape):
        return pl.BlockSpec(shape, lambda i: (0,) * len(shape))

    def lanes(r):
        return pl.BlockSpec((r, tm), lambda i: (0, i))

    return pl.pallas_call(
        functools.partial(_merge_kernel, alpha=alpha, n_experts=n_experts),
        grid=(n // tm,),
        in_specs=[rows(dc), rows(dsg), rows(d), rows(d), rows(d),
                  const(wa.shape), const(wb.shape), const(wo.shape), const(g1.shape), const(b1.shape),
                  const(rw.shape), const(rb.shape)],
        out_specs=[rows(d), rows(d), lanes(SUBLANES), lanes(SUBLANES), lanes(SUBLANES),
                   const((n_experts, LANES))],
        out_shape=[jax.ShapeDtypeStruct((n, d), F32), jax.ShapeDtypeStruct((n, d), BF16),
                   jax.ShapeDtypeStruct((SUBLANES, n), jnp.int32),
                   jax.ShapeDtypeStruct((SUBLANES, n), F32),
                   jax.ShapeDtypeStruct((SUBLANES, n), jnp.int32),
                   jax.ShapeDtypeStruct((n_experts, LANES), F32)],
        scratch_shapes=[pltpu.VMEM((n_experts, LANES), F32)],
        compiler_params=pltpu.CompilerParams(
            dimension_semantics=("arbitrary",), vmem_limit_bytes=VMEM_LIMIT),
        name="merge_route",
    )(ya, yb, ga, gb, x, wa, wb, wo, g1, b1, rw, rb)


def _ffn_kernel(te_ref, nu_ref, xs_ref, wg_ref, wu_ref, wd_ref, ys_ref):
    del te_ref

    @pl.when(pl.program_id(0) < nu_ref[0])
    def _():
        x = xs_ref[...]
        g = jnp.dot(x, wg_ref[0], preferred_element_type=F32)
        u = jnp.dot(x, wu_ref[0], preferred_element_type=F32)
        h = (g * jax.nn.sigmoid(g)) * u
        ys_ref[...] = jnp.dot(h.astype(BF16), wd_ref[0], preferred_element_type=F32).astype(ys_ref.dtype)


def _ffn_call(tile_expert, n_used, xs, wg, wu, wd):
    p, d = xs.shape
    f = wg.shape[2]
    tm = TOK_ROWS

    def tile(i, te, nu):
        return jnp.minimum(i, nu[0] - 1)

    return pl.pallas_call(
        _ffn_kernel,
        grid_spec=pltpu.PrefetchScalarGridSpec(
            num_scalar_prefetch=2,
            grid=(p // tm,),
            in_specs=[pl.BlockSpec((tm, d), lambda i, te, nu: (tile(i, te, nu), 0)),
                      pl.BlockSpec((1, d, f), lambda i, te, nu: (te[tile(i, te, nu)], 0, 0)),
                      pl.BlockSpec((1, d, f), lambda i, te, nu: (te[tile(i, te, nu)], 0, 0)),
                      pl.BlockSpec((1, f, d), lambda i, te, nu: (te[tile(i, te, nu)], 0, 0))],
            out_specs=pl.BlockSpec((tm, d), lambda i, te, nu: (tile(i, te, nu), 0)),
        ),
        out_shape=jax.ShapeDtypeStruct((p, d), BF16),
        compiler_params=pltpu.CompilerParams(
            dimension_semantics=("arbitrary",), vmem_limit_bytes=VMEM_LIMIT),
        name="expert_ffn",
    )(tile_expert, n_used, xs, wg, wu, wd)


def _norm2_kernel(x1_ref, y_ref, g_ref, b_ref, x2_ref, x2b_ref, *, alpha):
    x2 = _ln(alpha * x1_ref[...] + y_ref[...], g_ref[...], b_ref[...])
    x2_ref[...] = x2
    x2b_ref[...] = x2.astype(x2b_ref.dtype)


def _norm2_call(x1, y, g, b, alpha):
    n, d = x1.shape
    tm = TOK_ROWS
    rows = pl.BlockSpec((tm, d), lambda i: (i, 0))
    const = pl.BlockSpec((1, d), lambda i: (0, 0))
    return pl.pallas_call(
        functools.partial(_norm2_kernel, alpha=alpha),
        grid=(n // tm,),
        in_specs=[rows, rows, const, const],
        out_specs=[rows, rows],
        out_shape=[jax.ShapeDtypeStruct((n, d), F32), jax.ShapeDtypeStruct((n, d), BF16)],
        compiler_params=pltpu.CompilerParams(
            dimension_semantics=("arbitrary",), vmem_limit_bytes=VMEM_LIMIT),
        name="combine_norm2",
    )(x1, y, g, b)


def _spatial_tiles(w_s, b_s, seq):
    reps = ROW_SUB // seq
    wt = jnp.tile(w_s[:, :seq, :seq], (1, reps, reps))
    bt = jnp.tile(b_s[:, :seq], (1, reps))
    return wt, jnp.broadcast_to(bt[:, :, None], bt.shape + (LANES,))


def _history_rows(hist, nb):
    s, _, dc = hist.shape
    h = jnp.pad(hist, ((0, 0), (SUBLANES - (CONV_W - 1), 0), (0, 0)))
    return h.reshape(s, SUBLANES, nb, dc // nb).transpose(0, 2, 1, 3)


def _state_rows(cs):
    s, nb, _, cb = cs.shape
    return cs[:, :, SUBLANES - (CONV_W - 1):, :].transpose(0, 2, 1, 3).reshape(s, CONV_W - 1, nb * cb)


def kernel(x_prompt, x_sample, state_conv, w_in, b_in, conv_w, conv_b, sgu_ln_g, sgu_ln_b, w_spatial,
           b_spatial, w_a_out, w_b_out, w_o, ln1_g, ln1_b, router_w, router_bias, w_gate, w_up, w_down,
           ln2_g, ln2_b):
    depth, d, n_col = w_in.shape
    batch, seq, _ = x_prompt.shape
    dec_batch, dec_seq, _ = x_sample.shape
    dc = conv_w.shape[2]
    dsg = sgu_ln_g.shape[1]
    n_experts = router_w.shape[1]
    assert batch == 1 and n_col == 3 * dc + 2 * dsg + 2 * d
    assert dc % COL_BLOCK == 0 and dsg % COL_BLOCK == 0 and d % COL_BLOCK == 0
    assert dsg == SGU_GROUPS * LANES and n_experts % N_EXPERT_GROUPS == 0
    n_p, n_s = batch * seq, dec_batch * dec_seq
    n = n_p + n_s
    tm_p = min(MIX_ROWS, n_p)
    assert n_p % tm_p == 0 and n_s == ROW_SUB and ROW_SUB % dec_seq == 0 and n_p % TOK_ROWS == 0
    assert seq % SGU_CHUNK == 0 and SGU_CHUNK % dec_seq == 0
    alpha = (2.0 * depth) ** 0.25
    nb, ns, ng = dc // COL_BLOCK, dsg // COL_BLOCK, d // COL_BLOCK
    cfg_p = MixCfg(tm_p, seq, nb, ns, ng, False, 0)
    cfg_s = MixCfg(n_s, dec_seq, nb, ns, ng, True, 4)

    n_pairs = TOP_K * n
    n_tiles = n_pairs // TOK_ROWS + n_experts
    assert n_pairs % TOK_ROWS == 0

    x = jnp.concatenate([x_prompt.reshape(n_p, d), x_sample.reshape(n_s, d)], axis=0)
    xb = x.astype(BF16)
    rw = jnp.pad(router_w, ((0, 0), (0, LANES - n_experts))).astype(BF16)
    rb = jnp.broadcast_to(router_bias[:, None], (n_experts, TOK_ROWS))
    zero_hist = jnp.zeros((batch, CONV_W - 1, dc), F32)

    conv_p, conv_s, v_s = [], [], []
    for l in range(depth):
        w_in_b = w_in[l].astype(BF16)
        bias = b_in[l][None, :]
        cw = jnp.concatenate([conv_w[l], conv_b[l][None, :],
                              jnp.zeros((SUBLANES - CONV_W - 1, dc), F32)], axis=0)
        cw = cw.reshape(SUBLANES, nb, COL_BLOCK).transpose(1, 0, 2)
        lng, lnb = sgu_ln_g[l][None, :], sgu_ln_b[l][None, :]
        wt_p, bt_p = _spatial_tiles(w_spatial[l], b_spatial[l], min(seq, SGU_CHUNK))
        wt_s, bt_s = _spatial_tiles(w_spatial[l], b_spatial[l], min(dec_seq, SGU_CHUNK))

        outs_p = _mixer_call(xb[:n_p], w_in_b, bias, _history_rows(zero_hist, nb), cw, lng, lnb,
                             wt_p, bt_p, cfg_p, n, 0, ())
        outs_s = _mixer_call(xb[n_p:], w_in_b, bias, _history_rows(state_conv[l], nb), cw, lng, lnb,
                             wt_s, bt_s, cfg_s, n, n_p // n_s, tuple(outs_p[:4]))
        ya, yb, ga, gb = outs_s[:4]
        conv_p.append(_state_rows(outs_p[4]))
        conv_s.append(_state_rows(outs_s[4]))
        v_s.append(outs_s[5].reshape(dec_batch, dec_seq, dsg))

        x1, x1b, eidx, gw, rank, cnt = _merge_call(
            ya, yb, ga, gb, x, w_a_out[l].astype(BF16), w_b_out[l].astype(BF16), w_o[l].astype(BF16),
            ln1_g[l][None, :], ln1_b[l][None, :], rw, rb, alpha, n_experts)

        counts = cnt[:, 0].astype(jnp.int32)
        padded = ((counts + TOK_ROWS - 1) // TOK_ROWS) * TOK_ROWS
        seg_end = jnp.cumsum(padded)
        seg_start = seg_end - padded
        n_used = (seg_end[-1:] // TOK_ROWS).astype(jnp.int32)
        tile_row0 = jnp.arange(n_tiles, dtype=jnp.int32) * TOK_ROWS
        tile_expert = jnp.minimum(
            jnp.sum(tile_row0[:, None] >= seg_end[None, :], axis=1), n_experts - 1).astype(jnp.int32)
        pos = seg_start[eidx[:TOP_K]] + rank[:TOP_K]

        tok = jnp.tile(jnp.arange(n, dtype=jnp.int32), TOP_K)
        row_token = jnp.zeros((n_tiles * TOK_ROWS,), jnp.int32).at[pos.reshape(-1)].set(tok)
        xs = x1b[row_token]
        ys = _ffn_call(tile_expert, n_used, xs, w_gate[l].astype(BF16), w_up[l].astype(BF16),
                       w_down[l].astype(BF16))
        y = sum(gw[k][:, None] * ys[pos[k]].astype(F32) for k in range(TOP_K))

        x, xb = _norm2_call(x1, y, ln2_g[l][None, :], ln2_b[l][None, :], alpha)

    return (x[:n_p].reshape(batch, seq, d), x[n_p:].reshape(dec_batch, dec_seq, d),
            jnp.stack(conv_p), jnp.stack(conv_s), jnp.stack(v_s))
```

```python
import functools
from typing import NamedTuple

import jax
import jax.numpy as jnp
from jax import lax
from jax.experimental import pallas as pl
from jax.experimental.pallas import tpu as pltpu

CHUNK = 64
SGU_CHUNK = 128
SGU_GROUPS = 8
N_EXPERT_GROUPS = 4
TOP_K = 2
LN_EPS = 1e-5
CONV_W = 3

LANES = 128
SUBLANES = 8
COL_BLOCK = 512
ROW_SUB = 256
MIX_ROWS = 1024
TOK_ROWS = 256
GATHER_UNROLL = 8
VMEM_LIMIT = 56 * 1024 * 1024

F32 = jnp.float32
BF16 = jnp.bfloat16


class MixCfg(NamedTuple):
    tm: int
    n_p_tiles: int
    seq: int
    dec_seq: int
    n_s: int
    nb: int
    ns: int
    ng: int


def _ln(z, g, b):
    mu = jnp.mean(z, axis=-1, keepdims=True)
    d = z - mu
    var = jnp.mean(d * d, axis=-1, keepdims=True)
    return d * lax.rsqrt(var + LN_EPS) * g + b


def _conv_rows(hin, p2, p1, cw_ref, jj):
    rows = lax.broadcasted_iota(jnp.int32, hin.shape, 0)
    r1 = pltpu.roll(hin, 1, 0)
    r2 = pltpu.roll(hin, 2, 0)
    sh1 = jnp.where(rows == 0, p1, r1)
    sh2 = jnp.where(rows == 0, p2, jnp.where(rows == 1, p1, r2))
    out = cw_ref[jj, 3:4, :] + cw_ref[jj, 0:1, :] * sh2
    out = out + cw_ref[jj, 1:2, :] * sh1
    return out + cw_ref[jj, 2:3, :] * hin


def _spatial_keep(seq):
    t = lax.broadcasted_iota(jnp.int32, (ROW_SUB, ROW_SUB), 0)
    s = lax.broadcasted_iota(jnp.int32, (ROW_SUB, ROW_SUB), 1)
    same = (t // seq) == (s // seq)
    causal = ((s % seq) // CHUNK) <= ((t % seq) // CHUNK)
    return jnp.logical_and(same, causal)


def _mixer_kernel(x_ref, w_ref, bias_ref, prevp_ref, prevs_ref, cw_ref, lng_ref, lnb_ref,
                  wtp_ref, btp_ref, wts_ref, bts_ref,
                  ya_ref, yb_ref, ga_ref, gb_ref, csp_ref, css_ref, v_ref,
                  bsc, csc, usc, vsc, carry, wm_sc, bt_sc, *, cfg: MixCfg):
    i = pl.program_id(0)
    j = pl.program_id(1)
    nb, ns, ng = cfg.nb, cfg.ns, cfg.ng
    p_c, p_h, p_u, p_v = nb, 2 * nb, 3 * nb, 3 * nb + ns
    p_ga = p_v + ns
    p_gb = p_ga + ng
    cb = w_ref.shape[1]
    gpb = cb // LANES
    is_prompt = i < cfg.n_p_tiles
    is_sample = i == cfg.n_p_tiles
    n_sub = jnp.where(is_prompt, cfg.tm // ROW_SUB, cfg.n_s // ROW_SUB)

    def sub_rows(r):
        return pl.ds(pl.multiple_of(r * ROW_SUB, ROW_SUB), ROW_SUB)

    def proj(r):
        rows = sub_rows(r)
        acc = jnp.dot(x_ref[rows, :], w_ref[...], preferred_element_type=F32)
        return rows, acc + bias_ref[...]

    def for_sub(body):
        def step(r, c):
            body(r)
            return c
        lax.fori_loop(0, n_sub, step, 0)

    @pl.when(jnp.logical_and(i == 0, j == 0))
    def _():
        carry[...] = prevp_ref[0]

    @pl.when(j < p_c)
    def _():
        def body(r):
            rows, a = proj(r)
            bsc[j, rows, :] = a
        for_sub(body)

    @pl.when(jnp.logical_and(j >= p_c, j < p_h))
    def _():
        def body(r):
            rows, a = proj(r)
            csc[j - p_c, rows, :] = a
        for_sub(body)

    @pl.when(jnp.logical_and(j >= p_h, j < p_u))
    def _():
        jj = j - p_h

        def body(r):
            rows, a = proj(r)
            hin = csc[jj, rows, :] * a

            @pl.when(is_prompt)
            def _():
                out = _conv_rows(hin, carry[jj, 6:7, :], carry[jj, 7:8, :], cw_ref, jj)
                ya_ref[rows, :] = (bsc[jj, rows, :] * out).astype(ya_ref.dtype)
                carry[jj] = hin[ROW_SUB - SUBLANES:, :]
                csp_ref[0, jj] = hin[ROW_SUB - SUBLANES:, :]

            @pl.when(is_sample)
            def _():
                sr = cfg.dec_seq
                for s in range(ROW_SUB // sr):
                    hs = hin[s * sr:(s + 1) * sr, :]
                    st = r * (ROW_SUB // sr) + s
                    out = _conv_rows(hs, prevs_ref[st, jj, 6:7, :], prevs_ref[st, jj, 7:8, :], cw_ref, jj)
                    srow = pl.ds(pl.multiple_of(r * ROW_SUB + s * sr, sr), sr)
                    ya_ref[srow, :] = (bsc[jj, srow, :] * out).astype(ya_ref.dtype)
                    css_ref[st, jj] = hs[sr - SUBLANES:, :]
        for_sub(body)

    @pl.when(jnp.logical_and(j >= p_u, j < p_v))
    def _():
        def body(r):
            rows, a = proj(r)
            usc[j - p_u, rows, :] = jax.nn.gelu(a)
        for_sub(body)

    @pl.when(jnp.logical_and(j >= p_v, j < p_ga))
    def _():
        def body(r):
            rows, a = proj(r)
            vsc[j - p_v, rows, :] = jax.nn.gelu(a)
        for_sub(body)

    @pl.when(j == p_ga - 1)
    def _():
        @pl.when(is_prompt)
        def _():
            keep = _spatial_keep(min(cfg.seq, SGU_CHUNK))
            for g in range(SGU_GROUPS):
                wm_sc[g] = jnp.where(keep, wtp_ref[g], 0.0).astype(wm_sc.dtype)
            bt_sc[...] = btp_ref[...]

        @pl.when(is_sample)
        def _():
            keep = _spatial_keep(min(cfg.dec_seq, SGU_CHUNK))
            for g in range(SGU_GROUPS):
                wm_sc[g] = jnp.where(keep, wts_ref[g], 0.0).astype(wm_sc.dtype)
            bt_sc[...] = bts_ref[...]

        ds = ns * cb

        def body(r):
            rows = sub_rows(r)
            parts = [vsc[c, rows, :] for c in range(ns)]
            mu = sum(jnp.sum(p, axis=-1, keepdims=True) for p in parts) / ds
            cen = [p - mu for p in parts]
            var = sum(jnp.sum(c * c, axis=-1, keepdims=True) for c in cen) / ds
            inv = lax.rsqrt(var + LN_EPS)
            for c in range(ns):
                cols = slice(c * cb, (c + 1) * cb)
                vn = cen[c] * inv * lng_ref[:, cols] + lnb_ref[:, cols]

                @pl.when(is_sample)
                def _():
                    v_ref[rows, cols] = vn
                vnb = vn.astype(BF16)
                for q in range(gpb):
                    g = c * gpb + q
                    lanes = slice(q * LANES, (q + 1) * LANES)
                    sp = jnp.dot(wm_sc[g], vnb[:, lanes], preferred_element_type=F32) + bt_sc[g]
                    yb_ref[rows, g * LANES:(g + 1) * LANES] = (usc[c, rows, lanes] * sp).astype(yb_ref.dtype)
        for_sub(body)

    @pl.when(jnp.logical_and(j >= p_ga, j < p_gb))
    def _():
        def body(r):
            rows, a = proj(r)
            ga_ref[rows, :] = jax.nn.sigmoid(a)
        for_sub(body)

    @pl.when(j >= p_gb)
    def _():
        def body(r):
            rows, a = proj(r)
            gb_ref[rows, :] = jax.nn.sigmoid(a)
        for_sub(body)


def _mixer_call(xb, w_in, b_in, prev_p, prev_s, cw, lng, lnb, wt_p, bt_p, wt_s, bt_s, cfg):
    n, d = xb.shape
    cb = COL_BLOCK
    nb, ns, ng = cfg.nb, cfg.ns, cfg.ng
    dc, dsg = nb * cb, ns * cb
    n_j = w_in.shape[1] // cb
    n_i = cfg.n_p_tiles + 1
    p_h, p_ga, p_gb = 2 * nb, 3 * nb + 2 * ns, 3 * nb + 2 * ns + ng

    def const(shape):
        return pl.BlockSpec(shape, lambda i, j: (0,) * len(shape))

    in_specs = [
        pl.BlockSpec((cfg.tm, d), lambda i, j: (i, 0)),
        pl.BlockSpec((d, cb), lambda i, j: (0, j)),
        pl.BlockSpec((1, cb), lambda i, j: (0, j)),
        const(prev_p.shape), const(prev_s.shape), const(cw.shape), const(lng.shape), const(lnb.shape),
        const(wt_p.shape), const(bt_p.shape), const(wt_s.shape), const(bt_s.shape),
    ]
    out_shape = [
        jax.ShapeDtypeStruct((n, dc), BF16),
        jax.ShapeDtypeStruct((n, dsg), BF16),
        jax.ShapeDtypeStruct((n, d), F32),
        jax.ShapeDtypeStruct((n, d), F32),
        jax.ShapeDtypeStruct(prev_p.shape, F32),
        jax.ShapeDtypeStruct(prev_s.shape, F32),
        jax.ShapeDtypeStruct((cfg.n_s, dsg), F32),
    ]
    out_specs = [
        pl.BlockSpec((cfg.tm, cb), lambda i, j: (i, jnp.clip(j - p_h, 0, nb - 1))),
        pl.BlockSpec((cfg.tm, dsg), lambda i, j: (i, 0)),
        pl.BlockSpec((cfg.tm, cb), lambda i, j: (i, jnp.clip(j - p_ga, 0, ng - 1))),
        pl.BlockSpec((cfg.tm, cb), lambda i, j: (i, jnp.clip(j - p_gb, 0, ng - 1))),
        const(prev_p.shape), const(prev_s.shape), const((cfg.n_s, dsg)),
    ]
    scratch = [
        pltpu.VMEM((nb, cfg.tm, cb), F32),
        pltpu.VMEM((nb, cfg.tm, cb), F32),
        pltpu.VMEM((ns, cfg.tm, cb), F32),
        pltpu.VMEM((ns, cfg.tm, cb), F32),
        pltpu.VMEM((nb, SUBLANES, cb), F32),
        pltpu.VMEM((SGU_GROUPS, ROW_SUB, ROW_SUB), BF16),
        pltpu.VMEM((SGU_GROUPS, ROW_SUB, LANES), F32),
    ]
    return pl.pallas_call(
        functools.partial(_mixer_kernel, cfg=cfg),
        grid=(n_i, n_j),
        in_specs=in_specs,
        out_specs=out_specs,
        out_shape=out_shape,
        scratch_shapes=scratch,
        compiler_params=pltpu.CompilerParams(
            dimension_semantics=("arbitrary", "arbitrary"), vmem_limit_bytes=VMEM_LIMIT),
        name="mixer",
    )(xb, w_in, b_in, prev_p, prev_s, cw, lng, lnb, wt_p, bt_p, wt_s, bt_s)


def _merge_kernel(ya_ref, yb_ref, ga_ref, gb_ref, x_ref, wa_ref, wb_ref, wo_ref, g1_ref, b1_ref,
                  rw_ref, rb_ref, x1_ref, eidx_ref, gw_ref, rank_ref, cnt_ref, run, *,
                  alpha, n_experts):
    i = pl.program_id(0)
    tm = x_ref.shape[0]
    epg = n_experts // N_EXPERT_GROUPS

    @pl.when(i == 0)
    def _():
        run[...] = jnp.zeros_like(run)

    a = jnp.dot(ya_ref[...], wa_ref[...], preferred_element_type=F32)
    b = jnp.dot(yb_ref[...], wb_ref[...], preferred_element_type=F32)
    merged = ga_ref[...] * a + gb_ref[...] * b
    z = alpha * x_ref[...] + jnp.dot(merged.astype(BF16), wo_ref[...], preferred_element_type=F32)
    x1 = _ln(z, g1_ref[...], b1_ref[...])
    x1_ref[...] = x1

    logits = jnp.dot(x1.astype(BF16), rw_ref[...], preferred_element_type=F32)
    lt = logits.T[:n_experts, :]
    ex = jnp.exp(lt - jnp.max(lt, axis=0, keepdims=True))
    sc = ex / jnp.sum(ex, axis=0, keepdims=True)
    sel = sc + rb_ref[...]
    sel_r = [sel[e:e + 1, :] for e in range(n_experts)]
    sc_r = [sc[e:e + 1, :] for e in range(n_experts)]

    def pair_max(v):
        best = None
        for p in range(len(v)):
            for q in range(p + 1, len(v)):
                s = v[p] + v[q]
                best = s if best is None else jnp.maximum(best, s)
        return best

    grp = [pair_max(sel_r[g * epg:(g + 1) * epg]) for g in range(N_EXPERT_GROUPS)]
    gi = jnp.zeros((1, tm), jnp.int32)
    gbest = grp[0]
    for g in range(1, N_EXPERT_GROUPS):
        better = grp[g] > gbest
        gi = jnp.where(better, g, gi)
        gbest = jnp.where(better, grp[g], gbest)

    def pick(rows_, idx):
        out = rows_[-1]
        for c in range(len(rows_) - 2, -1, -1):
            out = jnp.where(idx == c, rows_[c], out)
        return out

    vk = [pick([sel_r[g * epg + q] for g in range(N_EXPERT_GROUPS)], gi) for q in range(epg)]
    pk = [pick([sc_r[g * epg + q] for g in range(N_EXPERT_GROUPS)], gi) for q in range(epg)]

    i1 = jnp.zeros((1, tm), jnp.int32)
    b1 = vk[0]
    for q in range(1, epg):
        better = vk[q] > b1
        i1 = jnp.where(better, q, i1)
        b1 = jnp.where(better, vk[q], b1)
    i2 = jnp.zeros((1, tm), jnp.int32)
    b2 = jnp.full((1, tm), -jnp.inf, F32)
    for q in range(epg):
        cand = jnp.logical_and(i1 != q, vk[q] > b2)
        i2 = jnp.where(cand, q, i2)
        b2 = jnp.where(cand, vk[q], b2)
    p1 = pick(pk, i1)
    p2 = pick(pk, i2)
    den = p1 + p2
    e1 = gi * epg + i1
    e2 = gi * epg + i2

    eio = lax.broadcasted_iota(jnp.int32, (n_experts, tm), 0)
    oh1 = (eio == e1).astype(F32)
    oh2 = (eio == e2).astype(F32)
    ts = lax.broadcasted_iota(jnp.int32, (tm, tm), 0)
    tt = lax.broadcasted_iota(jnp.int32, (tm, tm), 1)
    upper = (ts <= tt).astype(BF16)
    inc1 = jnp.dot(oh1.astype(BF16), upper, preferred_element_type=F32)
    inc2 = jnp.dot(oh2.astype(BF16), upper, preferred_element_type=F32)
    tot1 = jnp.sum(oh1, axis=1, keepdims=True)
    tot2 = jnp.sum(oh2, axis=1, keepdims=True)
    base = run[:, 0:1]
    r1 = jnp.sum(oh1 * (inc1 - 1.0 + base), axis=0, keepdims=True)
    r2 = jnp.sum(oh2 * (inc2 - 1.0 + base + tot1), axis=0, keepdims=True)
    new_run = run[...] + tot1 + tot2
    run[...] = new_run
    cnt_ref[...] = new_run

    eidx_ref[...] = jnp.zeros_like(eidx_ref)
    eidx_ref[0:1, :] = e1
    eidx_ref[1:2, :] = e2
    gw_ref[...] = jnp.zeros_like(gw_ref)
    gw_ref[0:1, :] = p1 / den
    gw_ref[1:2, :] = p2 / den
    rank_ref[...] = jnp.zeros_like(rank_ref)
    rank_ref[0:1, :] = r1.astype(jnp.int32)
    rank_ref[1:2, :] = r2.astype(jnp.int32)


def _merge_call(ya, yb, ga, gb, x, wa, wb, wo, g1, b1, rw, rb, alpha, n_experts):
    n, d = x.shape
    tm = TOK_ROWS
    dc, dsg = ya.shape[1], yb.shape[1]

    def rows(c):
        return pl.BlockSpec((tm, c), lambda i: (i, 0))

    def const(shape):
        return pl.BlockSpec(shape, lambda i: (0,) * len(shape))

    def lanes(r):
        return pl.BlockSpec((r, tm), lambda i: (0, i))

    return pl.pallas_call(
        functools.partial(_merge_kernel, alpha=alpha, n_experts=n_experts),
        grid=(n // tm,),
        in_specs=[rows(dc), rows(dsg), rows(d), rows(d), rows(d),
                  const(wa.shape), const(wb.shape), const(wo.shape), const(g1.shape), const(b1.shape),
                  const(rw.shape), const(rb.shape)],
        out_specs=[rows(d), lanes(SUBLANES), lanes(SUBLANES), lanes(SUBLANES),
                   const((n_experts, LANES))],
        out_shape=[jax.ShapeDtypeStruct((n, d), F32),
                   jax.ShapeDtypeStruct((SUBLANES, n), jnp.int32),
                   jax.ShapeDtypeStruct((SUBLANES, n), F32),
                   jax.ShapeDtypeStruct((SUBLANES, n), jnp.int32),
                   jax.ShapeDtypeStruct((n_experts, LANES), F32)],
        scratch_shapes=[pltpu.VMEM((n_experts, LANES), F32)],
        compiler_params=pltpu.CompilerParams(
            dimension_semantics=("arbitrary",), vmem_limit_bytes=VMEM_LIMIT),
        name="merge_route",
    )(ya, yb, ga, gb, x, wa, wb, wo, g1, b1, rw, rb)


def _row_gather(src_hbm, dst, sem, index_of, n_rows):
    def step(q, c):
        for u in range(GATHER_UNROLL):
            r = q * GATHER_UNROLL + u
            pltpu.make_async_copy(src_hbm.at[pl.ds(index_of(r), 1)], dst.at[pl.ds(r, 1)], sem).start()
        return c
    lax.fori_loop(0, n_rows // GATHER_UNROLL, step, 0)


def _row_gather_wait(src_hbm, dst, sem):
    pltpu.make_async_copy(src_hbm.at[pl.ds(0, dst.shape[0])], dst, sem).wait()


def _ffn_kernel(te_ref, nu_ref, tok_ref, x1_hbm, wg_ref, wu_ref, wd_ref, ys_ref, xbuf, sem):
    del te_ref
    i = pl.program_id(0)
    tm = xbuf.shape[1]
    slot = i % 2

    def start(tile, s):
        _row_gather(x1_hbm, xbuf.at[s], sem.at[s], lambda r: tok_ref[tile * tm + r], tm)

    @pl.when(i == 0)
    def _():
        start(0, 0)

    @pl.when(i + 1 < nu_ref[0])
    def _():
        start(i + 1, 1 - slot)

    @pl.when(i < nu_ref[0])
    def _():
        _row_gather_wait(x1_hbm, xbuf.at[slot], sem.at[slot])
        x = xbuf[slot].astype(BF16)
        g = jnp.dot(x, wg_ref[0], preferred_element_type=F32)
        u = jnp.dot(x, wu_ref[0], preferred_element_type=F32)
        h = (g * jax.nn.sigmoid(g)) * u
        ys_ref[...] = jnp.dot(h.astype(BF16), wd_ref[0], preferred_element_type=F32)

    @pl.when(i >= nu_ref[0])
    def _():
        ys_ref[...] = jnp.zeros_like(ys_ref)


def _ffn_call(tile_expert, n_used, row_token, x1, wg, wu, wd):
    d = x1.shape[1]
    p = row_token.shape[0]
    f = wg.shape[2]
    tm = TOK_ROWS

    def expert(i, te, nu, tok):
        return (te[jnp.minimum(i, nu[0] - 1)], 0, 0)

    return pl.pallas_call(
        _ffn_kernel,
        grid_spec=pltpu.PrefetchScalarGridSpec(
            num_scalar_prefetch=3,
            grid=(p // tm,),
            in_specs=[pl.BlockSpec(memory_space=pl.ANY),
                      pl.BlockSpec((1, d, f), expert),
                      pl.BlockSpec((1, d, f), expert),
                      pl.BlockSpec((1, f, d), expert)],
            out_specs=pl.BlockSpec((tm, d), lambda i, te, nu, tok: (i, 0)),
            scratch_shapes=[pltpu.VMEM((2, tm, d), F32), pltpu.SemaphoreType.DMA((2,))],
        ),
        out_shape=jax.ShapeDtypeStruct((p, d), F32),
        compiler_params=pltpu.CompilerParams(
            dimension_semantics=("arbitrary",), vmem_limit_bytes=VMEM_LIMIT),
        name="expert_ffn",
    )(tile_expert, n_used, row_token, x1, wg, wu, wd)


def _norm2_kernel(pos_ref, x1_ref, gw_ref, g_ref, b_ref, ys_hbm, x2_ref, x2b_ref, ybuf, sem, *, alpha):
    i = pl.program_id(0)
    n_steps = pl.num_programs(0)
    tm = x1_ref.shape[0]
    n = tm * n_steps
    slot = i % 2

    def start(tile, s):
        for k in range(TOP_K):
            _row_gather(ys_hbm, ybuf.at[s, k], sem.at[s],
                        lambda r, k=k: pos_ref[k * n + tile * tm + r], tm)

    @pl.when(i == 0)
    def _():
        start(0, 0)

    @pl.when(i + 1 < n_steps)
    def _():
        start(i + 1, 1 - slot)

    for k in range(TOP_K):
        _row_gather_wait(ys_hbm, ybuf.at[slot, k], sem.at[slot])
    y = None
    for k in range(TOP_K):
        term = gw_ref[:, k:k + 1] * ybuf[slot, k]
        y = term if y is None else y + term
    x2 = _ln(alpha * x1_ref[...] + y, g_ref[...], b_ref[...])
    x2_ref[...] = x2
    x2b_ref[...] = x2.astype(x2b_ref.dtype)


def _norm2_call(pos_flat, x1, gwc, g, b, ys, alpha):
    n, d = x1.shape
    tm = TOK_ROWS
    rows = pl.BlockSpec((tm, d), lambda i, pos: (i, 0))
    const = pl.BlockSpec((1, d), lambda i, pos: (0, 0))
    return pl.pallas_call(
        functools.partial(_norm2_kernel, alpha=alpha),
        grid_spec=pltpu.PrefetchScalarGridSpec(
            num_scalar_prefetch=1,
            grid=(n // tm,),
            in_specs=[rows, pl.BlockSpec((tm, LANES), lambda i, pos: (i, 0)), const, const,
                      pl.BlockSpec(memory_space=pl.ANY)],
            out_specs=[rows, rows],
            scratch_shapes=[pltpu.VMEM((2, TOP_K, tm, d), F32), pltpu.SemaphoreType.DMA((2,))],
        ),
        out_shape=[jax.ShapeDtypeStruct((n, d), F32), jax.ShapeDtypeStruct((n, d), BF16)],
        compiler_params=pltpu.CompilerParams(
            dimension_semantics=("arbitrary",), vmem_limit_bytes=VMEM_LIMIT),
        name="combine_norm2",
    )(pos_flat, x1, gwc, g, b, ys)


def _spatial_tiles(w_s, b_s, seq):
    reps = ROW_SUB // seq
    wt = jnp.tile(w_s[:, :seq, :seq], (1, reps, reps))
    bt = jnp.tile(b_s[:, :seq], (1, reps))
    return wt, jnp.broadcast_to(bt[:, :, None], bt.shape + (LANES,))


def _history_rows(hist, nb):
    s, _, dc = hist.shape
    h = jnp.pad(hist, ((0, 0), (SUBLANES - (CONV_W - 1), 0), (0, 0)))
    return h.reshape(s, SUBLANES, nb, dc // nb).transpose(0, 2, 1, 3)


def _state_rows(cs):
    s, nb, _, cb = cs.shape
    return cs[:, :, SUBLANES - (CONV_W - 1):, :].transpose(0, 2, 1, 3).reshape(s, CONV_W - 1, nb * cb)


def kernel(x_prompt, x_sample, state_conv, w_in, b_in, conv_w, conv_b, sgu_ln_g, sgu_ln_b, w_spatial,
           b_spatial, w_a_out, w_b_out, w_o, ln1_g, ln1_b, router_w, router_bias, w_gate, w_up, w_down,
           ln2_g, ln2_b):
    depth, d, n_col = w_in.shape
    batch, seq, _ = x_prompt.shape
    dec_batch, dec_seq, _ = x_sample.shape
    dc = conv_w.shape[2]
    dsg = sgu_ln_g.shape[1]
    n_experts = router_w.shape[1]
    assert batch == 1 and n_col == 3 * dc + 2 * dsg + 2 * d
    assert dc % COL_BLOCK == 0 and dsg % COL_BLOCK == 0 and d % COL_BLOCK == 0
    assert dsg == SGU_GROUPS * LANES and n_experts % N_EXPERT_GROUPS == 0
    n_p, n_s = batch * seq, dec_batch * dec_seq
    n = n_p + n_s
    tm_p = min(MIX_ROWS, n_p)
    assert n_p % tm_p == 0 and n_s == ROW_SUB and ROW_SUB % dec_seq == 0 and n_p % TOK_ROWS == 0
    assert seq % SGU_CHUNK == 0 and SGU_CHUNK % dec_seq == 0
    alpha = (2.0 * depth) ** 0.25
    nb, ns, ng = dc // COL_BLOCK, dsg // COL_BLOCK, d // COL_BLOCK
    cfg = MixCfg(tm_p, n_p // tm_p, seq, dec_seq, n_s, nb, ns, ng)

    n_pairs = TOP_K * n
    n_tiles = n_pairs // TOK_ROWS + n_experts
    assert n_pairs % TOK_ROWS == 0

    x = jnp.concatenate([x_prompt.reshape(n_p, d), x_sample.reshape(n_s, d)], axis=0)
    xb = x.astype(BF16)
    rw = jnp.pad(router_w, ((0, 0), (0, LANES - n_experts))).astype(BF16)
    rb = jnp.broadcast_to(router_bias[:, None], (n_experts, TOK_ROWS))
    zero_hist = jnp.zeros((batch, CONV_W - 1, dc), F32)

    conv_p, conv_s, v_s = [], [], []
    for l in range(depth):
        w_in_b = w_in[l].astype(BF16)
        bias = b_in[l][None, :]
        cw = jnp.concatenate([conv_w[l], conv_b[l][None, :],
                              jnp.zeros((SUBLANES - CONV_W - 1, dc), F32)], axis=0)
        cw = cw.reshape(SUBLANES, nb, COL_BLOCK).transpose(1, 0, 2)
        lng, lnb = sgu_ln_g[l][None, :], sgu_ln_b[l][None, :]
        wt_p, bt_p = _spatial_tiles(w_spatial[l], b_spatial[l], min(seq, SGU_CHUNK))
        wt_s, bt_s = _spatial_tiles(w_spatial[l], b_spatial[l], min(dec_seq, SGU_CHUNK))

        ya, yb, ga, gb, cs_p, cs_s, v_rows = _mixer_call(
            xb, w_in_b, bias, _history_rows(zero_hist, nb), _history_rows(state_conv[l], nb), cw, lng, lnb,
            wt_p, bt_p, wt_s, bt_s, cfg)
        conv_p.append(_state_rows(cs_p))
        conv_s.append(_state_rows(cs_s))
        v_s.append(v_rows.reshape(dec_batch, dec_seq, dsg))

        x1, eidx, gw, rank, cnt = _merge_call(
            ya, yb, ga, gb, x, w_a_out[l].astype(BF16), w_b_out[l].astype(BF16), w_o[l].astype(BF16),
            ln1_g[l][None, :], ln1_b[l][None, :], rw, rb, alpha, n_experts)

        counts = cnt[:, 0].astype(jnp.int32)
        padded = ((counts + TOK_ROWS - 1) // TOK_ROWS) * TOK_ROWS
        seg_end = jnp.cumsum(padded)
        seg_start = seg_end - padded
        n_used = (seg_end[-1:] // TOK_ROWS).astype(jnp.int32)
        tile_row0 = jnp.arange(n_tiles, dtype=jnp.int32) * TOK_ROWS
        tile_expert = jnp.minimum(
            jnp.sum(tile_row0[:, None] >= seg_end[None, :], axis=1), n_experts - 1).astype(jnp.int32)
        pos = seg_start[eidx[:TOP_K]] + rank[:TOP_K]

        tok = jnp.tile(jnp.arange(n, dtype=jnp.int32), TOP_K)
        pos_flat = pos.reshape(-1)
        row_token = jnp.zeros((n_tiles * TOK_ROWS,), jnp.int32).at[pos_flat].set(tok)
        ys = _ffn_call(tile_expert, n_used, row_token, x1, w_gate[l].astype(BF16), w_up[l].astype(BF16),
                       w_down[l].astype(BF16))
        gwc = jnp.pad(gw[:TOP_K].T, ((0, 0), (0, LANES - TOP_K)))
        x, xb = _norm2_call(pos_flat, x1, gwc, ln2_g[l][None, :], ln2_b[l][None, :], ys, alpha)

    return (x[:n_p].reshape(batch, seq, d), x[n_p:].reshape(dec_batch, dec_seq, d),
            jnp.stack(conv_p), jnp.stack(conv_s), jnp.stack(v_s))
```

```python
import functools
from typing import NamedTuple

import jax
import jax.numpy as jnp
from jax import lax
from jax.experimental import pallas as pl
from jax.experimental.pallas import tpu as pltpu

CHUNK = 64
SGU_CHUNK = 128
SGU_GROUPS = 8
N_EXPERT_GROUPS = 4
TOP_K = 2
LN_EPS = 1e-5
CONV_W = 3

LANES = 128
SUBLANES = 8
COL_BLOCK = 512
ROW_SUB = 256
MIX_ROWS = 1024
TOK_ROWS = 256
W_PARTS = 8
GATHER_SPLIT = 4
VMEM_LIMIT = 56 * 1024 * 1024

F32 = jnp.float32
BF16 = jnp.bfloat16


class MixCfg(NamedTuple):
    tm: int
    n_p_tiles: int
    seq: int
    dec_seq: int
    n_s: int
    nb: int
    ns: int
    ng: int


def _ln(z, g, b):
    mu = jnp.mean(z, axis=-1, keepdims=True)
    d = z - mu
    var = jnp.mean(d * d, axis=-1, keepdims=True)
    return d * lax.rsqrt(var + LN_EPS) * g + b


def _conv_rows(hin, p2, p1, cw_ref, jj):
    rows = lax.broadcasted_iota(jnp.int32, hin.shape, 0)
    r1 = pltpu.roll(hin, 1, 0)
    r2 = pltpu.roll(hin, 2, 0)
    sh1 = jnp.where(rows == 0, p1, r1)
    sh2 = jnp.where(rows == 0, p2, jnp.where(rows == 1, p1, r2))
    out = cw_ref[jj, 3:4, :] + cw_ref[jj, 0:1, :] * sh2
    out = out + cw_ref[jj, 1:2, :] * sh1
    return out + cw_ref[jj, 2:3, :] * hin


def _spatial_keep(seq):
    t = lax.broadcasted_iota(jnp.int32, (ROW_SUB, ROW_SUB), 0)
    s = lax.broadcasted_iota(jnp.int32, (ROW_SUB, ROW_SUB), 1)
    same = (t // seq) == (s // seq)
    causal = ((s % seq) // CHUNK) <= ((t % seq) // CHUNK)
    return jnp.logical_and(same, causal)


def _mixer_kernel(x_ref, w_ref, bias_ref, prevp_ref, prevs_ref, cw_ref, lng_ref, lnb_ref,
                  wtp_ref, btp_ref, wts_ref, bts_ref,
                  ya_ref, yb_ref, ga_ref, gb_ref, csp_ref, css_ref, v_ref,
                  bsc, csc, usc, vsc, carry, wm_sc, bt_sc, *, cfg: MixCfg):
    i = pl.program_id(0)
    j = pl.program_id(1)
    nb, ns, ng = cfg.nb, cfg.ns, cfg.ng
    p_c, p_h, p_u, p_v = nb, 2 * nb, 3 * nb, 3 * nb + ns
    p_ga = p_v + ns
    p_gb = p_ga + ng
    cb = w_ref.shape[1]
    gpb = cb // LANES
    is_prompt = i < cfg.n_p_tiles
    is_sample = i == cfg.n_p_tiles

    def sub_rows(r):
        return pl.ds(r * ROW_SUB, ROW_SUB)

    def proj(r):
        rows = sub_rows(r)
        acc = jnp.dot(x_ref[rows, :], w_ref[...], preferred_element_type=F32)
        return rows, acc + bias_ref[...]

    def for_sub(body):
        @pl.when(is_prompt)
        def _():
            for r in range(cfg.tm // ROW_SUB):
                body(r, False)

        @pl.when(is_sample)
        def _():
            for r in range(cfg.n_s // ROW_SUB):
                body(r, True)

    @pl.when(jnp.logical_and(i == 0, j == 0))
    def _():
        carry[...] = prevp_ref[0]

    @pl.when(j < p_c)
    def _():
        def body(r, sample):
            rows, a = proj(r)
            bsc[j, rows, :] = a
        for_sub(body)

    @pl.when(jnp.logical_and(j >= p_c, j < p_h))
    def _():
        def body(r, sample):
            rows, a = proj(r)
            csc[j - p_c, rows, :] = a
        for_sub(body)

    @pl.when(jnp.logical_and(j >= p_h, j < p_u))
    def _():
        jj = j - p_h

        def body(r, sample):
            rows, a = proj(r)
            hin = csc[jj, rows, :] * a
            if not sample:
                out = _conv_rows(hin, carry[jj, 6:7, :], carry[jj, 7:8, :], cw_ref, jj)
                ya_ref[rows, :] = (bsc[jj, rows, :] * out).astype(ya_ref.dtype)
                carry[jj] = hin[ROW_SUB - SUBLANES:, :]
                csp_ref[0, jj] = hin[ROW_SUB - SUBLANES:, :]
            else:
                sr = cfg.dec_seq
                for s in range(ROW_SUB // sr):
                    hs = hin[s * sr:(s + 1) * sr, :]
                    st = r * (ROW_SUB // sr) + s
                    out = _conv_rows(hs, prevs_ref[st, jj, 6:7, :], prevs_ref[st, jj, 7:8, :], cw_ref, jj)
                    srow = pl.ds(r * ROW_SUB + s * sr, sr)
                    ya_ref[srow, :] = (bsc[jj, srow, :] * out).astype(ya_ref.dtype)
                    css_ref[st, jj] = hs[sr - SUBLANES:, :]
        for_sub(body)

    @pl.when(jnp.logical_and(j >= p_u, j < p_v))
    def _():
        def body(r, sample):
            rows, a = proj(r)
            usc[j - p_u, rows, :] = jax.nn.gelu(a)
        for_sub(body)

    @pl.when(jnp.logical_and(j >= p_v, j < p_ga))
    def _():
        def body(r, sample):
            rows, a = proj(r)
            vsc[j - p_v, rows, :] = jax.nn.gelu(a)
        for_sub(body)

    @pl.when(j == p_ga - 1)
    def _():
        @pl.when(is_prompt)
        def _():
            keep = _spatial_keep(min(cfg.seq, SGU_CHUNK))
            for g in range(SGU_GROUPS):
                wm_sc[g] = jnp.where(keep, wtp_ref[g], 0.0).astype(wm_sc.dtype)
            bt_sc[...] = btp_ref[...]

        @pl.when(is_sample)
        def _():
            keep = _spatial_keep(min(cfg.dec_seq, SGU_CHUNK))
            for g in range(SGU_GROUPS):
                wm_sc[g] = jnp.where(keep, wts_ref[g], 0.0).astype(wm_sc.dtype)
            bt_sc[...] = bts_ref[...]

        ds = ns * cb

        def body(r, sample):
            rows = sub_rows(r)
            parts = [vsc[c, rows, :] for c in range(ns)]
            mu = sum(jnp.sum(p, axis=-1, keepdims=True) for p in parts) / ds
            cen = [p - mu for p in parts]
            var = sum(jnp.sum(c * c, axis=-1, keepdims=True) for c in cen) / ds
            inv = lax.rsqrt(var + LN_EPS)
            for c in range(ns):
                cols = slice(c * cb, (c + 1) * cb)
                vn = cen[c] * inv * lng_ref[:, cols] + lnb_ref[:, cols]
                if sample:
                    v_ref[rows, cols] = vn
                vnb = vn.astype(BF16)
                for q in range(gpb):
                    g = c * gpb + q
                    lanes = slice(q * LANES, (q + 1) * LANES)
                    sp = jnp.dot(wm_sc[g], vnb[:, lanes], preferred_element_type=F32) + bt_sc[g]
                    yb_ref[rows, g * LANES:(g + 1) * LANES] = (usc[c, rows, lanes] * sp).astype(yb_ref.dtype)
        for_sub(body)

    @pl.when(jnp.logical_and(j >= p_ga, j < p_gb))
    def _():
        def body(r, sample):
            rows, a = proj(r)
            ga_ref[rows, :] = jax.nn.sigmoid(a)
        for_sub(body)

    @pl.when(j >= p_gb)
    def _():
        def body(r, sample):
            rows, a = proj(r)
            gb_ref[rows, :] = jax.nn.sigmoid(a)
        for_sub(body)


def _mixer_call(xb, w_in, b_in, prev_p, prev_s, cw, lng, lnb, wt_p, bt_p, wt_s, bt_s, cfg):
    n, d = xb.shape
    cb = COL_BLOCK
    nb, ns, ng = cfg.nb, cfg.ns, cfg.ng
    dc, dsg = nb * cb, ns * cb
    n_j = w_in.shape[1] // cb
    n_i = cfg.n_p_tiles + 1
    p_h, p_ga, p_gb = 2 * nb, 3 * nb + 2 * ns, 3 * nb + 2 * ns + ng

    def const(shape):
        return pl.BlockSpec(shape, lambda i, j: (0,) * len(shape))

    in_specs = [
        pl.BlockSpec((cfg.tm, d), lambda i, j: (i, 0)),
        pl.BlockSpec((d, cb), lambda i, j: (0, j)),
        pl.BlockSpec((1, cb), lambda i, j: (0, j)),
        const(prev_p.shape), const(prev_s.shape), const(cw.shape), const(lng.shape), const(lnb.shape),
        const(wt_p.shape), const(bt_p.shape), const(wt_s.shape), const(bt_s.shape),
    ]
    out_shape = [
        jax.ShapeDtypeStruct((n, dc), BF16),
        jax.ShapeDtypeStruct((n, dsg), BF16),
        jax.ShapeDtypeStruct((n, d), F32),
        jax.ShapeDtypeStruct((n, d), F32),
        jax.ShapeDtypeStruct(prev_p.shape, F32),
        jax.ShapeDtypeStruct(prev_s.shape, F32),
        jax.ShapeDtypeStruct((cfg.n_s, dsg), F32),
    ]
    out_specs = [
        pl.BlockSpec((cfg.tm, cb), lambda i, j: (i, jnp.clip(j - p_h, 0, nb - 1))),
        pl.BlockSpec((cfg.tm, dsg), lambda i, j: (i, 0)),
        pl.BlockSpec((cfg.tm, cb), lambda i, j: (i, jnp.clip(j - p_ga, 0, ng - 1))),
        pl.BlockSpec((cfg.tm, cb), lambda i, j: (i, jnp.clip(j - p_gb, 0, ng - 1))),
        const(prev_p.shape), const(prev_s.shape), const((cfg.n_s, dsg)),
    ]
    scratch = [
        pltpu.VMEM((nb, cfg.tm, cb), F32),
        pltpu.VMEM((nb, cfg.tm, cb), F32),
        pltpu.VMEM((ns, cfg.tm, cb), F32),
        pltpu.VMEM((ns, cfg.tm, cb), F32),
        pltpu.VMEM((nb, SUBLANES, cb), F32),
        pltpu.VMEM((SGU_GROUPS, ROW_SUB, ROW_SUB), BF16),
        pltpu.VMEM((SGU_GROUPS, ROW_SUB, LANES), F32),
    ]
    return pl.pallas_call(
        functools.partial(_mixer_kernel, cfg=cfg),
        grid=(n_i, n_j),
        in_specs=in_specs,
        out_specs=out_specs,
        out_shape=out_shape,
        scratch_shapes=scratch,
        compiler_params=pltpu.CompilerParams(
            dimension_semantics=("arbitrary", "arbitrary"), vmem_limit_bytes=VMEM_LIMIT),
        name="mixer",
    )(xb, w_in, b_in, prev_p, prev_s, cw, lng, lnb, wt_p, bt_p, wt_s, bt_s)


def _merge_kernel(ya_ref, yb_ref, ga_ref, gb_ref, x_ref, wa_ref, wb_ref, wo_ref, g1_ref, b1_ref,
                  rw_ref, rb_ref, x1_ref, eidx_ref, gw_ref, rank_ref, cnt_ref, run, *,
                  alpha, n_experts):
    i = pl.program_id(0)
    tm = x_ref.shape[0]
    epg = n_experts // N_EXPERT_GROUPS

    @pl.when(i == 0)
    def _():
        run[...] = jnp.zeros_like(run)

    a = jnp.dot(ya_ref[...], wa_ref[...], preferred_element_type=F32)
    b = jnp.dot(yb_ref[...], wb_ref[...], preferred_element_type=F32)
    merged = ga_ref[...] * a + gb_ref[...] * b
    z = alpha * x_ref[...] + jnp.dot(merged.astype(BF16), wo_ref[...], preferred_element_type=F32)
    x1 = _ln(z, g1_ref[...], b1_ref[...])
    x1_ref[...] = x1

    logits = jnp.dot(x1.astype(BF16), rw_ref[...], preferred_element_type=F32)
    lt = logits.T[:n_experts, :]
    ex = jnp.exp(lt - jnp.max(lt, axis=0, keepdims=True))
    sc = ex / jnp.sum(ex, axis=0, keepdims=True)
    sel = sc + rb_ref[...]
    sel_r = [sel[e:e + 1, :] for e in range(n_experts)]
    sc_r = [sc[e:e + 1, :] for e in range(n_experts)]

    def pair_max(v):
        best = None
        for p in range(len(v)):
            for q in range(p + 1, len(v)):
                s = v[p] + v[q]
                best = s if best is None else jnp.maximum(best, s)
        return best

    grp = [pair_max(sel_r[g * epg:(g + 1) * epg]) for g in range(N_EXPERT_GROUPS)]
    gi = jnp.zeros((1, tm), jnp.int32)
    gbest = grp[0]
    for g in range(1, N_EXPERT_GROUPS):
        better = grp[g] > gbest
        gi = jnp.where(better, g, gi)
        gbest = jnp.where(better, grp[g], gbest)

    def pick(rows_, idx):
        out = rows_[-1]
        for c in range(len(rows_) - 2, -1, -1):
            out = jnp.where(idx == c, rows_[c], out)
        return out

    vk = [pick([sel_r[g * epg + q] for g in range(N_EXPERT_GROUPS)], gi) for q in range(epg)]
    pk = [pick([sc_r[g * epg + q] for g in range(N_EXPERT_GROUPS)], gi) for q in range(epg)]

    i1 = jnp.zeros((1, tm), jnp.int32)
    b1 = vk[0]
    for q in range(1, epg):
        better = vk[q] > b1
        i1 = jnp.where(better, q, i1)
        b1 = jnp.where(better, vk[q], b1)
    i2 = jnp.zeros((1, tm), jnp.int32)
    b2 = jnp.full((1, tm), -jnp.inf, F32)
    for q in range(epg):
        cand = jnp.logical_and(i1 != q, vk[q] > b2)
        i2 = jnp.where(cand, q, i2)
        b2 = jnp.where(cand, vk[q], b2)
    p1 = pick(pk, i1)
    p2 = pick(pk, i2)
    den = p1 + p2
    e1 = gi * epg + i1
    e2 = gi * epg + i2

    eio = lax.broadcasted_iota(jnp.int32, (n_experts, tm), 0)
    oh1 = (eio == e1).astype(F32)
    oh2 = (eio == e2).astype(F32)
    ts = lax.broadcasted_iota(jnp.int32, (tm, tm), 0)
    tt = lax.broadcasted_iota(jnp.int32, (tm, tm), 1)
    upper = (ts <= tt).astype(BF16)
    inc1 = jnp.dot(oh1.astype(BF16), upper, preferred_element_type=F32)
    inc2 = jnp.dot(oh2.astype(BF16), upper, preferred_element_type=F32)
    tot1 = jnp.sum(oh1, axis=1, keepdims=True)
    tot2 = jnp.sum(oh2, axis=1, keepdims=True)
    base = run[:, 0:1]
    r1 = jnp.sum(oh1 * (inc1 - 1.0 + base), axis=0, keepdims=True)
    r2 = jnp.sum(oh2 * (inc2 - 1.0 + base + tot1), axis=0, keepdims=True)
    new_run = run[...] + tot1 + tot2
    run[...] = new_run
    cnt_ref[...] = new_run

    eidx_ref[...] = jnp.zeros_like(eidx_ref)
    eidx_ref[0:1, :] = e1
    eidx_ref[1:2, :] = e2
    gw_ref[...] = jnp.zeros_like(gw_ref)
    gw_ref[0:1, :] = p1 / den
    gw_ref[1:2, :] = p2 / den
    rank_ref[...] = jnp.zeros_like(rank_ref)
    rank_ref[0:1, :] = r1.astype(jnp.int32)
    rank_ref[1:2, :] = r2.astype(jnp.int32)


def _merge_call(ya, yb, ga, gb, x, wa, wb, wo, g1, b1, rw, rb, alpha, n_experts):
    n, d = x.shape
    tm = TOK_ROWS
    dc, dsg = ya.shape[1], yb.shape[1]

    def rows(c):
        return pl.BlockSpec((tm, c), lambda i: (i, 0))

    def const(shape):
        return pl.BlockSpec(shape, lambda i: (0,) * len(shape))

    def lanes(r):
        return pl.BlockSpec((r, tm), lambda i: (0, i))

    return pl.pallas_call(
        functools.partial(_merge_kernel, alpha=alpha, n_experts=n_experts),
        grid=(n // tm,),
        in_specs=[rows(dc), rows(dsg), rows(d), rows(d), rows(d),
                  const(wa.shape), const(wb.shape), const(wo.shape), const(g1.shape), const(b1.shape),
                  const(rw.shape), const(rb.shape)],
        out_specs=[rows(d), lanes(SUBLANES), lanes(SUBLANES), lanes(SUBLANES),
                   const((n_experts, LANES))],
        out_shape=[jax.ShapeDtypeStruct((n, d), F32),
                   jax.ShapeDtypeStruct((SUBLANES, n), jnp.int32),
                   jax.ShapeDtypeStruct((SUBLANES, n), F32),
                   jax.ShapeDtypeStruct((SUBLANES, n), jnp.int32),
                   jax.ShapeDtypeStruct((n_experts, LANES), F32)],
        scratch_shapes=[pltpu.VMEM((n_experts, LANES), F32)],
        compiler_params=pltpu.CompilerParams(
            dimension_semantics=("arbitrary",), vmem_limit_bytes=VMEM_LIMIT),
        name="merge_route",
    )(ya, yb, ga, gb, x, wa, wb, wo, g1, b1, rw, rb)


def _row_gather(src_hbm, dst, sem, index_of, n_rows):
    for r in range(n_rows):
        pltpu.make_async_copy(src_hbm.at[pl.ds(index_of(r), 1)], dst.at[pl.ds(r, 1)], sem).start()


def _row_gather_wait(src_hbm, dst, sem):
    pltpu.make_async_copy(src_hbm.at[pl.ds(0, dst.shape[0])], dst, sem).wait()


def _ffn_kernel(te_ref, nu_ref, ne_ref, t0_ref, t1_ref, ws_ref, tok_ref,
                x1_hbm, wg_hbm, wu_hbm, wd_hbm, ys_ref,
                xbuf, xsem, wgb, wub, wdb, sg, su, sd, wsem, *, layer):
    i = pl.program_id(0)
    n_used = nu_ref[0]
    tm = xbuf.shape[1]
    slot = i % 2

    def part_copies(e, t, s):
        return (pltpu.make_async_copy(wg_hbm.at[layer, e, t], sg.at[s], wsem.at[s]),
                pltpu.make_async_copy(wu_hbm.at[layer, e, t], su.at[s], wsem.at[s]),
                pltpu.make_async_copy(wd_hbm.at[layer, e, t], sd.at[s], wsem.at[s]))

    def start_part(e, t, s):
        for c in part_copies(e, t, s):
            c.start()

    def finish_part(e, t, s, w):
        for c in part_copies(e, t, s):
            c.wait()
        rin, rout = sg.shape[1], sd.shape[1]
        wgb[w, pl.ds(pl.multiple_of(t * rin, rin), rin), :] = sg[s].astype(BF16)
        wub[w, pl.ds(pl.multiple_of(t * rin, rin), rin), :] = su[s].astype(BF16)
        wdb[w, pl.ds(pl.multiple_of(t * rout, rout), rout), :] = sd[s].astype(BF16)

    def load_parts(e, t_lo, t_hi, w):
        def step(t, c):
            s = (t - t_lo) % 2
            finish_part(e, t, s, w)

            @pl.when(t + 2 < t_hi)
            def _():
                start_part(e, t + 2, s)
            return c
        lax.fori_loop(t_lo, t_hi, step, 0)

    def start_first_parts(e, t_lo, t_hi):
        @pl.when(t_lo < t_hi)
        def _():
            start_part(e, t_lo, 0)

        @pl.when(t_lo + 1 < t_hi)
        def _():
            start_part(e, t_lo + 1, 1)

    @pl.when(i == 0)
    def _():
        def first(r, c):
            pltpu.make_async_copy(x1_hbm.at[pl.ds(tok_ref[r], 1)], xbuf.at[0, pl.ds(r, 1)], xsem.at[0]).start()
            return c
        lax.fori_loop(0, tm, first, 0)
        start_first_parts(te_ref[0], 0, W_PARTS)
        load_parts(te_ref[0], 0, W_PARTS, 0)

    @pl.when(i < n_used)
    def _():
        w = ws_ref[i]
        e_next, t_lo, t_hi = ne_ref[i], t0_ref[i], t1_ref[i]
        start_first_parts(e_next, t_lo, t_hi)
        nxt = jnp.minimum(i + 1, n_used - 1)
        _row_gather_wait(x1_hbm, xbuf.at[slot], xsem.at[slot])
        x = xbuf[slot].astype(BF16)
        f = wgb.shape[2]
        hs = []
        for c in range(GATHER_SPLIT):
            rows = tm // GATHER_SPLIT
            _row_gather(x1_hbm, xbuf.at[1 - slot, pl.ds(c * rows, rows)], xsem.at[1 - slot],
                        lambda r, c=c: tok_ref[nxt * tm + c * rows + r], rows)
            cols = slice(c * (f // GATHER_SPLIT), (c + 1) * (f // GATHER_SPLIT))
            g = jnp.dot(x, wgb[w, :, cols], preferred_element_type=F32)
            u = jnp.dot(x, wub[w, :, cols], preferred_element_type=F32)
            hs.append(((g * jax.nn.sigmoid(g)) * u).astype(BF16))
        h = hs[0] if len(hs) == 1 else jnp.concatenate(hs, axis=1)
        ys_ref[...] = jnp.dot(h, wdb[w], preferred_element_type=F32)
        load_parts(e_next, t_lo, t_hi, 1 - w)

    @pl.when(i == n_used - 1)
    def _():
        _row_gather_wait(x1_hbm, xbuf.at[1 - slot], xsem.at[1 - slot])

    @pl.when(i >= n_used)
    def _():
        ys_ref[...] = jnp.zeros_like(ys_ref)


def _ffn_call(sched, row_token, x1, w_gate, w_up, w_down, layer):
    d = x1.shape[1]
    p = row_token.shape[0]
    depth, n_e, _, f = w_gate.shape
    tm = TOK_ROWS
    wg = w_gate.reshape(depth, n_e, W_PARTS, d // W_PARTS, f)
    wu = w_up.reshape(depth, n_e, W_PARTS, d // W_PARTS, f)
    wd = w_down.reshape(depth, n_e, W_PARTS, f // W_PARTS, d)
    any_spec = pl.BlockSpec(memory_space=pl.ANY)
    return pl.pallas_call(
        functools.partial(_ffn_kernel, layer=layer),
        grid_spec=pltpu.PrefetchScalarGridSpec(
            num_scalar_prefetch=7,
            grid=(p // tm,),
            in_specs=[any_spec, any_spec, any_spec, any_spec],
            out_specs=pl.BlockSpec((tm, d), lambda i, *_: (i, 0)),
            scratch_shapes=[pltpu.VMEM((2, tm, d), F32), pltpu.SemaphoreType.DMA((2,)),
                            pltpu.VMEM((2, d, f), BF16), pltpu.VMEM((2, d, f), BF16),
                            pltpu.VMEM((2, f, d), BF16),
                            pltpu.VMEM((2, d // W_PARTS, f), F32), pltpu.VMEM((2, d // W_PARTS, f), F32),
                            pltpu.VMEM((2, f // W_PARTS, d), F32), pltpu.SemaphoreType.DMA((2,))],
        ),
        out_shape=jax.ShapeDtypeStruct((p, d), F32),
        compiler_params=pltpu.CompilerParams(
            dimension_semantics=("arbitrary",), vmem_limit_bytes=VMEM_LIMIT),
        name="expert_ffn",
    )(*sched, row_token, x1, wg, wu, wd)


def _norm2_kernel(pos_ref, x1_ref, gw_ref, g_ref, b_ref, ys_hbm, x2_ref, x2b_ref, ybuf, sem, *, alpha):
    i = pl.program_id(0)
    n_steps = pl.num_programs(0)
    tm = x1_ref.shape[0]
    n = tm * n_steps
    slot = i % 2

    @pl.when(i == 0)
    def _():
        def first(r, c):
            for k in range(TOP_K):
                pltpu.make_async_copy(ys_hbm.at[pl.ds(pos_ref[k * n + r], 1)], ybuf.at[0, k, pl.ds(r, 1)],
                                      sem.at[0]).start()
            return c
        lax.fori_loop(0, tm, first, 0)

    nxt = jnp.minimum(i + 1, n_steps - 1)
    for k in range(TOP_K):
        _row_gather(ys_hbm, ybuf.at[1 - slot, k], sem.at[1 - slot],
                    lambda r, k=k: pos_ref[k * n + nxt * tm + r], tm)
    for k in range(TOP_K):
        _row_gather_wait(ys_hbm, ybuf.at[slot, k], sem.at[slot])
    y = None
    for k in range(TOP_K):
        term = gw_ref[:, k:k + 1] * ybuf[slot, k]
        y = term if y is None else y + term
    x2 = _ln(alpha * x1_ref[...] + y, g_ref[...], b_ref[...])
    x2_ref[...] = x2
    x2b_ref[...] = x2.astype(x2b_ref.dtype)

    @pl.when(i == n_steps - 1)
    def _():
        for k in range(TOP_K):
            _row_gather_wait(ys_hbm, ybuf.at[1 - slot, k], sem.at[1 - slot])


def _norm2_call(pos_flat, x1, gwc, g, b, ys, alpha):
    n, d = x1.shape
    tm = TOK_ROWS
    rows = pl.BlockSpec((tm, d), lambda i, pos: (i, 0))
    const = pl.BlockSpec((1, d), lambda i, pos: (0, 0))
    return pl.pallas_call(
        functools.partial(_norm2_kernel, alpha=alpha),
        grid_spec=pltpu.PrefetchScalarGridSpec(
            num_scalar_prefetch=1,
            grid=(n // tm,),
            in_specs=[rows, pl.BlockSpec((tm, LANES), lambda i, pos: (i, 0)), const, const,
                      pl.BlockSpec(memory_space=pl.ANY)],
            out_specs=[rows, rows],
            scratch_shapes=[pltpu.VMEM((2, TOP_K, tm, d), F32), pltpu.SemaphoreType.DMA((2,))],
        ),
        out_shape=[jax.ShapeDtypeStruct((n, d), F32), jax.ShapeDtypeStruct((n, d), BF16)],
        compiler_params=pltpu.CompilerParams(
            dimension_semantics=("arbitrary",), vmem_limit_bytes=VMEM_LIMIT),
        name="combine_norm2",
    )(pos_flat, x1, gwc, g, b, ys)


def _spatial_tiles(w_s, b_s, seq):
    reps = ROW_SUB // seq
    wt = jnp.tile(w_s[:, :seq, :seq], (1, reps, reps))
    bt = jnp.tile(b_s[:, :seq], (1, reps))
    return wt, jnp.broadcast_to(bt[:, :, None], bt.shape + (LANES,))


def _history_rows(hist, nb):
    s, _, dc = hist.shape
    h = jnp.pad(hist, ((0, 0), (SUBLANES - (CONV_W - 1), 0), (0, 0)))
    return h.reshape(s, SUBLANES, nb, dc // nb).transpose(0, 2, 1, 3)


def _state_rows(cs):
    s, nb, _, cb = cs.shape
    return cs[:, :, SUBLANES - (CONV_W - 1):, :].transpose(0, 2, 1, 3).reshape(s, CONV_W - 1, nb * cb)


def kernel(x_prompt, x_sample, state_conv, w_in, b_in, conv_w, conv_b, sgu_ln_g, sgu_ln_b, w_spatial,
           b_spatial, w_a_out, w_b_out, w_o, ln1_g, ln1_b, router_w, router_bias, w_gate, w_up, w_down,
           ln2_g, ln2_b):
    depth, d, n_col = w_in.shape
    batch, seq, _ = x_prompt.shape
    dec_batch, dec_seq, _ = x_sample.shape
    dc = conv_w.shape[2]
    dsg = sgu_ln_g.shape[1]
    n_experts = router_w.shape[1]
    assert batch == 1 and n_col == 3 * dc + 2 * dsg + 2 * d
    assert dc % COL_BLOCK == 0 and dsg % COL_BLOCK == 0 and d % COL_BLOCK == 0
    assert dsg == SGU_GROUPS * LANES and n_experts % N_EXPERT_GROUPS == 0
    n_p, n_s = batch * seq, dec_batch * dec_seq
    n = n_p + n_s
    tm_p = min(MIX_ROWS, n_p)
    assert n_p % tm_p == 0 and n_s == ROW_SUB and ROW_SUB % dec_seq == 0 and n_p % TOK_ROWS == 0
    assert seq % SGU_CHUNK == 0 and SGU_CHUNK % dec_seq == 0
    alpha = (2.0 * depth) ** 0.25
    nb, ns, ng = dc // COL_BLOCK, dsg // COL_BLOCK, d // COL_BLOCK
    cfg = MixCfg(tm_p, n_p // tm_p, seq, dec_seq, n_s, nb, ns, ng)

    n_pairs = TOP_K * n
    n_tiles = n_pairs // TOK_ROWS + n_experts
    assert n_pairs % TOK_ROWS == 0

    x = jnp.concatenate([x_prompt.reshape(n_p, d), x_sample.reshape(n_s, d)], axis=0)
    xb = x.astype(BF16)
    rw = jnp.pad(router_w, ((0, 0), (0, LANES - n_experts))).astype(BF16)
    rb = jnp.broadcast_to(router_bias[:, None], (n_experts, TOK_ROWS))
    zero_hist = jnp.zeros((batch, CONV_W - 1, dc), F32)

    conv_p, conv_s, v_s = [], [], []
    for l in range(depth):
        w_in_b = w_in[l].astype(BF16)
        bias = b_in[l][None, :]
        cw = jnp.concatenate([conv_w[l], conv_b[l][None, :],
                              jnp.zeros((SUBLANES - CONV_W - 1, dc), F32)], axis=0)
        cw = cw.reshape(SUBLANES, nb, COL_BLOCK).transpose(1, 0, 2)
        lng, lnb = sgu_ln_g[l][None, :], sgu_ln_b[l][None, :]
        wt_p, bt_p = _spatial_tiles(w_spatial[l], b_spatial[l], min(seq, SGU_CHUNK))
        wt_s, bt_s = _spatial_tiles(w_spatial[l], b_spatial[l], min(dec_seq, SGU_CHUNK))

        ya, yb, ga, gb, cs_p, cs_s, v_rows = _mixer_call(
            xb, w_in_b, bias, _history_rows(zero_hist, nb), _history_rows(state_conv[l], nb), cw, lng, lnb,
            wt_p, bt_p, wt_s, bt_s, cfg)
        conv_p.append(_state_rows(cs_p))
        conv_s.append(_state_rows(cs_s))
        v_s.append(v_rows.reshape(dec_batch, dec_seq, dsg))

        x1, eidx, gw, rank, cnt = _merge_call(
            ya, yb, ga, gb, x, w_a_out[l].astype(BF16), w_b_out[l].astype(BF16), w_o[l].astype(BF16),
            ln1_g[l][None, :], ln1_b[l][None, :], rw, rb, alpha, n_experts)

        counts = cnt[:, 0].astype(jnp.int32)
        padded = ((counts + TOK_ROWS - 1) // TOK_ROWS) * TOK_ROWS
        seg_end = jnp.cumsum(padded)
        seg_start = seg_end - padded
        n_used = (seg_end[-1:] // TOK_ROWS).astype(jnp.int32)
        tile_row0 = jnp.arange(n_tiles, dtype=jnp.int32) * TOK_ROWS
        tile_expert = jnp.minimum(
            jnp.sum(tile_row0[:, None] >= seg_end[None, :], axis=1), n_experts - 1).astype(jnp.int32)
        e_ids = jnp.arange(n_experts, dtype=jnp.int32)
        pos = rank[:TOP_K] + jnp.sum(
            jnp.where(eidx[:TOP_K, :, None] == e_ids, seg_start.astype(jnp.int32), 0), axis=-1)

        n_t = (padded // TOK_ROWS).astype(jnp.int32)
        later = jnp.where(n_t > 0, e_ids, n_experts)
        nxt_e = jnp.concatenate([jnp.flip(lax.cummin(jnp.flip(later)))[1:],
                                 jnp.full((1,), n_experts, jnp.int32)])
        group = (jnp.cumsum(n_t > 0) - 1).astype(jnp.int32)
        tile_i = jnp.arange(n_tiles, dtype=jnp.int32)
        q = tile_i - (seg_start // TOK_ROWS).astype(jnp.int32)[tile_expert]
        g_t = jnp.maximum(n_t[tile_expert], 1)
        loads = jnp.logical_and(tile_i < n_used[0], nxt_e[tile_expert] < n_experts)
        t_lo = jnp.where(loads, (W_PARTS * q) // g_t, 0).astype(jnp.int32)
        t_hi = jnp.where(loads, (W_PARTS * (q + 1)) // g_t, 0).astype(jnp.int32)
        sched = (tile_expert, n_used, jnp.minimum(nxt_e[tile_expert], n_experts - 1).astype(jnp.int32),
                 t_lo, t_hi, (group[tile_expert] % 2).astype(jnp.int32))

        tok = jnp.tile(jnp.arange(n, dtype=jnp.int32), TOP_K)
        pos_flat = pos.reshape(-1)
        row_token = jnp.zeros((n_tiles * TOK_ROWS,), jnp.int32).at[pos_flat].set(tok)
        ys = _ffn_call(sched, row_token, x1, w_gate, w_up, w_down, l)
        gwc = jnp.pad(gw[:TOP_K].T, ((0, 0), (0, LANES - TOP_K)))
        x, xb = _norm2_call(pos_flat, x1, gwc, ln2_g[l][None, :], ln2_b[l][None, :], ys, alpha)

    return (x[:n_p].reshape(batch, seq, d), x[n_p:].reshape(dec_batch, dec_seq, d),
            jnp.stack(conv_p), jnp.stack(conv_s), jnp.stack(v_s))
```

```python
import functools
from typing import NamedTuple

import jax
import jax.numpy as jnp
from jax import lax
from jax.experimental import pallas as pl
from jax.experimental.pallas import tpu as pltpu

CHUNK = 64
SGU_CHUNK = 128
SGU_GROUPS = 8
N_EXPERT_GROUPS = 4
TOP_K = 2
LN_EPS = 1e-5
CONV_W = 3

LANES = 128
SUBLANES = 8
COL_BLOCK = 512
ROW_SUB = 256
MIX_ROWS = 1024
TOK_ROWS = 256
W_PARTS = 8
VMEM_LIMIT = 56 * 1024 * 1024

F32 = jnp.float32
BF16 = jnp.bfloat16


class MixCfg(NamedTuple):
    tm: int
    n_p_tiles: int
    seq: int
    dec_seq: int
    n_s: int
    nb: int
    ns: int
    ng: int


def _ln(z, g, b):
    mu = jnp.mean(z, axis=-1, keepdims=True)
    d = z - mu
    var = jnp.mean(d * d, axis=-1, keepdims=True)
    return d * lax.rsqrt(var + LN_EPS) * g + b


def _conv_rows(hin, p2, p1, cw_ref, jj):
    rows = lax.broadcasted_iota(jnp.int32, hin.shape, 0)
    r1 = pltpu.roll(hin, 1, 0)
    r2 = pltpu.roll(hin, 2, 0)
    sh1 = jnp.where(rows == 0, p1, r1)
    sh2 = jnp.where(rows == 0, p2, jnp.where(rows == 1, p1, r2))
    out = cw_ref[jj, 3:4, :] + cw_ref[jj, 0:1, :] * sh2
    out = out + cw_ref[jj, 1:2, :] * sh1
    return out + cw_ref[jj, 2:3, :] * hin


def _spatial_keep(seq):
    t = lax.broadcasted_iota(jnp.int32, (ROW_SUB, ROW_SUB), 0)
    s = lax.broadcasted_iota(jnp.int32, (ROW_SUB, ROW_SUB), 1)
    same = (t // seq) == (s // seq)
    causal = ((s % seq) // CHUNK) <= ((t % seq) // CHUNK)
    return jnp.logical_and(same, causal)


def _mixer_kernel(x_ref, w_ref, bias_ref, prevp_ref, prevs_ref, cw_ref, lng_ref, lnb_ref,
                  wtp_ref, btp_ref, wts_ref, bts_ref,
                  ya_ref, yb_ref, ga_ref, gb_ref, csp_ref, css_ref, v_ref,
                  bsc, csc, usc, vsc, carry, wm_sc, bt_sc, *, cfg: MixCfg):
    i = pl.program_id(0)
    j = pl.program_id(1)
    nb, ns, ng = cfg.nb, cfg.ns, cfg.ng
    p_c, p_h, p_u, p_v = nb, 2 * nb, 3 * nb, 3 * nb + ns
    p_ga = p_v + ns
    p_gb = p_ga + ng
    cb = w_ref.shape[1]
    gpb = cb // LANES
    is_prompt = i < cfg.n_p_tiles
    is_sample = i == cfg.n_p_tiles

    def sub_rows(r):
        return pl.ds(r * ROW_SUB, ROW_SUB)

    def proj(r):
        rows = sub_rows(r)
        acc = jnp.dot(x_ref[rows, :], w_ref[...], preferred_element_type=F32)
        return rows, acc + bias_ref[...]

    def for_sub(body):
        @pl.when(is_prompt)
        def _():
            for r in range(cfg.tm // ROW_SUB):
                body(r, False)

        @pl.when(is_sample)
        def _():
            for r in range(cfg.n_s // ROW_SUB):
                body(r, True)

    @pl.when(jnp.logical_and(i == 0, j == 0))
    def _():
        carry[...] = prevp_ref[0]

    @pl.when(j < p_c)
    def _():
        def body(r, sample):
            rows, a = proj(r)
            bsc[j, rows, :] = a
        for_sub(body)

    @pl.when(jnp.logical_and(j >= p_c, j < p_h))
    def _():
        def body(r, sample):
            rows, a = proj(r)
            csc[j - p_c, rows, :] = a
        for_sub(body)

    @pl.when(jnp.logical_and(j >= p_h, j < p_u))
    def _():
        jj = j - p_h

        def body(r, sample):
            rows, a = proj(r)
            hin = csc[jj, rows, :] * a
            if not sample:
                out = _conv_rows(hin, carry[jj, 6:7, :], carry[jj, 7:8, :], cw_ref, jj)
                ya_ref[rows, :] = (bsc[jj, rows, :] * out).astype(ya_ref.dtype)
                carry[jj] = hin[ROW_SUB - SUBLANES:, :]
                csp_ref[0, jj] = hin[ROW_SUB - SUBLANES:, :]
            else:
                sr = cfg.dec_seq
                for s in range(ROW_SUB // sr):
                    hs = hin[s * sr:(s + 1) * sr, :]
                    st = r * (ROW_SUB // sr) + s
                    out = _conv_rows(hs, prevs_ref[st, jj, 6:7, :], prevs_ref[st, jj, 7:8, :], cw_ref, jj)
                    srow = pl.ds(r * ROW_SUB + s * sr, sr)
                    ya_ref[srow, :] = (bsc[jj, srow, :] * out).astype(ya_ref.dtype)
                    css_ref[st, jj] = hs[sr - SUBLANES:, :]
        for_sub(body)

    @pl.when(jnp.logical_and(j >= p_u, j < p_v))
    def _():
        def body(r, sample):
            rows, a = proj(r)
            usc[j - p_u, rows, :] = jax.nn.gelu(a)
        for_sub(body)

    @pl.when(jnp.logical_and(j >= p_v, j < p_ga))
    def _():
        def body(r, sample):
            rows, a = proj(r)
            vsc[j - p_v, rows, :] = jax.nn.gelu(a)
        for_sub(body)

    @pl.when(j == p_ga - 1)
    def _():
        @pl.when(is_prompt)
        def _():
            keep = _spatial_keep(min(cfg.seq, SGU_CHUNK))
            for g in range(SGU_GROUPS):
                wm_sc[g] = jnp.where(keep, wtp_ref[g], 0.0).astype(wm_sc.dtype)
            bt_sc[...] = btp_ref[...]

        @pl.when(is_sample)
        def _():
            keep = _spatial_keep(min(cfg.dec_seq, SGU_CHUNK))
            for g in range(SGU_GROUPS):
                wm_sc[g] = jnp.where(keep, wts_ref[g], 0.0).astype(wm_sc.dtype)
            bt_sc[...] = bts_ref[...]

        ds = ns * cb

        def body(r, sample):
            rows = sub_rows(r)
            parts = [vsc[c, rows, :] for c in range(ns)]
            mu = sum(jnp.sum(p, axis=-1, keepdims=True) for p in parts) / ds
            cen = [p - mu for p in parts]
            var = sum(jnp.sum(c * c, axis=-1, keepdims=True) for c in cen) / ds
            inv = lax.rsqrt(var + LN_EPS)
            for c in range(ns):
                cols = slice(c * cb, (c + 1) * cb)
                vn = cen[c] * inv * lng_ref[:, cols] + lnb_ref[:, cols]
                if sample:
                    v_ref[rows, cols] = vn
                vnb = vn.astype(BF16)
                for q in range(gpb):
                    g = c * gpb + q
                    lanes = slice(q * LANES, (q + 1) * LANES)
                    sp = jnp.dot(wm_sc[g], vnb[:, lanes], preferred_element_type=F32) + bt_sc[g]
                    yb_ref[rows, g * LANES:(g + 1) * LANES] = (usc[c, rows, lanes] * sp).astype(yb_ref.dtype)
        for_sub(body)

    @pl.when(jnp.logical_and(j >= p_ga, j < p_gb))
    def _():
        def body(r, sample):
            rows, a = proj(r)
            ga_ref[rows, :] = jax.nn.sigmoid(a)
        for_sub(body)

    @pl.when(j >= p_gb)
    def _():
        def body(r, sample):
            rows, a = proj(r)
            gb_ref[rows, :] = jax.nn.sigmoid(a)
        for_sub(body)


def _mixer_call(xb, w_in, b_in, prev_p, prev_s, cw, lng, lnb, wt_p, bt_p, wt_s, bt_s, cfg):
    n, d = xb.shape
    cb = COL_BLOCK
    nb, ns, ng = cfg.nb, cfg.ns, cfg.ng
    dc, dsg = nb * cb, ns * cb
    n_j = w_in.shape[1] // cb
    n_i = cfg.n_p_tiles + 1
    p_h, p_ga, p_gb = 2 * nb, 3 * nb + 2 * ns, 3 * nb + 2 * ns + ng

    def const(shape):
        return pl.BlockSpec(shape, lambda i, j: (0,) * len(shape))

    in_specs = [
        pl.BlockSpec((cfg.tm, d), lambda i, j: (i, 0)),
        pl.BlockSpec((d, cb), lambda i, j: (0, j)),
        pl.BlockSpec((1, cb), lambda i, j: (0, j)),
        const(prev_p.shape), const(prev_s.shape), const(cw.shape), const(lng.shape), const(lnb.shape),
        const(wt_p.shape), const(bt_p.shape), const(wt_s.shape), const(bt_s.shape),
    ]
    out_shape = [
        jax.ShapeDtypeStruct((n, dc), BF16),
        jax.ShapeDtypeStruct((n, dsg), BF16),
        jax.ShapeDtypeStruct((n, d), F32),
        jax.ShapeDtypeStruct((n, d), F32),
        jax.ShapeDtypeStruct(prev_p.shape, F32),
        jax.ShapeDtypeStruct(prev_s.shape, F32),
        jax.ShapeDtypeStruct((cfg.n_s, dsg), F32),
    ]
    out_specs = [
        pl.BlockSpec((cfg.tm, cb), lambda i, j: (i, jnp.clip(j - p_h, 0, nb - 1))),
        pl.BlockSpec((cfg.tm, dsg), lambda i, j: (i, 0)),
        pl.BlockSpec((cfg.tm, cb), lambda i, j: (i, jnp.clip(j - p_ga, 0, ng - 1))),
        pl.BlockSpec((cfg.tm, cb), lambda i, j: (i, jnp.clip(j - p_gb, 0, ng - 1))),
        const(prev_p.shape), const(prev_s.shape), const((cfg.n_s, dsg)),
    ]
    scratch = [
        pltpu.VMEM((nb, cfg.tm, cb), F32),
        pltpu.VMEM((nb, cfg.tm, cb), F32),
        pltpu.VMEM((ns, cfg.tm, cb), F32),
        pltpu.VMEM((ns, cfg.tm, cb), F32),
        pltpu.VMEM((nb, SUBLANES, cb), F32),
        pltpu.VMEM((SGU_GROUPS, ROW_SUB, ROW_SUB), BF16),
        pltpu.VMEM((SGU_GROUPS, ROW_SUB, LANES), F32),
    ]
    return pl.pallas_call(
        functools.partial(_mixer_kernel, cfg=cfg),
        grid=(n_i, n_j),
        in_specs=in_specs,
        out_specs=out_specs,
        out_shape=out_shape,
        scratch_shapes=scratch,
        compiler_params=pltpu.CompilerParams(
            dimension_semantics=("arbitrary", "arbitrary"), vmem_limit_bytes=VMEM_LIMIT),
        name="mixer",
    )(xb, w_in, b_in, prev_p, prev_s, cw, lng, lnb, wt_p, bt_p, wt_s, bt_s)


def _merge_kernel(ya_ref, yb_ref, ga_ref, gb_ref, x_ref, wa_ref, wb_ref, wo_ref, g1_ref, b1_ref,
                  rw_ref, rb_ref, x1_ref, eidx_ref, gw_ref, rank_ref, cnt_ref, run, *,
                  alpha, n_experts):
    i = pl.program_id(0)
    tm = x_ref.shape[0]
    epg = n_experts // N_EXPERT_GROUPS

    @pl.when(i == 0)
    def _():
        run[...] = jnp.zeros_like(run)

    a = jnp.dot(ya_ref[...], wa_ref[...], preferred_element_type=F32)
    b = jnp.dot(yb_ref[...], wb_ref[...], preferred_element_type=F32)
    merged = ga_ref[...] * a + gb_ref[...] * b
    z = alpha * x_ref[...] + jnp.dot(merged.astype(BF16), wo_ref[...], preferred_element_type=F32)
    x1 = _ln(z, g1_ref[...], b1_ref[...])
    x1_ref[...] = x1

    logits = jnp.dot(x1.astype(BF16), rw_ref[...], preferred_element_type=F32)
    lt = logits.T[:n_experts, :]
    ex = jnp.exp(lt - jnp.max(lt, axis=0, keepdims=True))
    sc = ex / jnp.sum(ex, axis=0, keepdims=True)
    sel = sc + rb_ref[...]
    sel_r = [sel[e:e + 1, :] for e in range(n_experts)]
    sc_r = [sc[e:e + 1, :] for e in range(n_experts)]

    def pair_max(v):
        best = None
        for p in range(len(v)):
            for q in range(p + 1, len(v)):
                s = v[p] + v[q]
                best = s if best is None else jnp.maximum(best, s)
        return best

    grp = [pair_max(sel_r[g * epg:(g + 1) * epg]) for g in range(N_EXPERT_GROUPS)]
    gi = jnp.zeros((1, tm), jnp.int32)
    gbest = grp[0]
    for g in range(1, N_EXPERT_GROUPS):
        better = grp[g] > gbest
        gi = jnp.where(better, g, gi)
        gbest = jnp.where(better, grp[g], gbest)

    def pick(rows_, idx):
        out = rows_[-1]
        for c in range(len(rows_) - 2, -1, -1):
            out = jnp.where(idx == c, rows_[c], out)
        return out

    vk = [pick([sel_r[g * epg + q] for g in range(N_EXPERT_GROUPS)], gi) for q in range(epg)]
    pk = [pick([sc_r[g * epg + q] for g in range(N_EXPERT_GROUPS)], gi) for q in range(epg)]

    i1 = jnp.zeros((1, tm), jnp.int32)
    b1 = vk[0]
    for q in range(1, epg):
        better = vk[q] > b1
        i1 = jnp.where(better, q, i1)
        b1 = jnp.where(better, vk[q], b1)
    i2 = jnp.zeros((1, tm), jnp.int32)
    b2 = jnp.full((1, tm), -jnp.inf, F32)
    for q in range(epg):
        cand = jnp.logical_and(i1 != q, vk[q] > b2)
        i2 = jnp.where(cand, q, i2)
        b2 = jnp.where(cand, vk[q], b2)
    p1 = pick(pk, i1)
    p2 = pick(pk, i2)
    den = p1 + p2
    e1 = gi * epg + i1
    e2 = gi * epg + i2

    eio = lax.broadcasted_iota(jnp.int32, (n_experts, tm), 0)
    oh1 = (eio == e1).astype(F32)
    oh2 = (eio == e2).astype(F32)
    ts = lax.broadcasted_iota(jnp.int32, (tm, tm), 0)
    tt = lax.broadcasted_iota(jnp.int32, (tm, tm), 1)
    upper = (ts <= tt).astype(BF16)
    inc1 = jnp.dot(oh1.astype(BF16), upper, preferred_element_type=F32)
    inc2 = jnp.dot(oh2.astype(BF16), upper, preferred_element_type=F32)
    tot1 = jnp.sum(oh1, axis=1, keepdims=True)
    tot2 = jnp.sum(oh2, axis=1, keepdims=True)
    base = run[:, 0:1]
    r1 = jnp.sum(oh1 * (inc1 - 1.0 + base), axis=0, keepdims=True)
    r2 = jnp.sum(oh2 * (inc2 - 1.0 + base + tot1), axis=0, keepdims=True)
    new_run = run[...] + tot1 + tot2
    run[...] = new_run
    cnt_ref[...] = new_run

    eidx_ref[...] = jnp.zeros_like(eidx_ref)
    eidx_ref[0:1, :] = e1
    eidx_ref[1:2, :] = e2
    gw_ref[...] = jnp.zeros_like(gw_ref)
    gw_ref[0:1, :] = p1 / den
    gw_ref[1:2, :] = p2 / den
    rank_ref[...] = jnp.zeros_like(rank_ref)
    rank_ref[0:1, :] = r1.astype(jnp.int32)
    rank_ref[1:2, :] = r2.astype(jnp.int32)


def _merge_call(ya, yb, ga, gb, x, wa, wb, wo, g1, b1, rw, rb, alpha, n_experts):
    n, d = x.shape
    tm = TOK_ROWS
    dc, dsg = ya.shape[1], yb.shape[1]

    def rows(c):
        return pl.BlockSpec((tm, c), lambda i: (i, 0))

    def const(shape):
        return pl.BlockSpec(shape, lambda i: (0,) * len(shape))

    def lanes(r):
        return pl.BlockSpec((r, tm), lambda i: (0, i))

    return pl.pallas_call(
        functools.partial(_merge_kernel, alpha=alpha, n_experts=n_experts),
        grid=(n // tm,),
        in_specs=[rows(dc), rows(dsg), rows(d), rows(d), rows(d),
                  const(wa.shape), const(wb.shape), const(wo.shape), const(g1.shape), const(b1.shape),
                  const(rw.shape), const(rb.shape)],
        out_specs=[rows(d), lanes(SUBLANES), lanes(SUBLANES), lanes(SUBLANES),
                   const((n_experts, LANES))],
        out_shape=[jax.ShapeDtypeStruct((n, d), F32),
                   jax.ShapeDtypeStruct((SUBLANES, n), jnp.int32),
                   jax.ShapeDtypeStruct((SUBLANES, n), F32),
                   jax.ShapeDtypeStruct((SUBLANES, n), jnp.int32),
                   jax.ShapeDtypeStruct((n_experts, LANES), F32)],
        scratch_shapes=[pltpu.VMEM((n_experts, LANES), F32)],
        compiler_params=pltpu.CompilerParams(
            dimension_semantics=("arbitrary",), vmem_limit_bytes=VMEM_LIMIT),
        name="merge_route",
    )(ya, yb, ga, gb, x, wa, wb, wo, g1, b1, rw, rb)


def _row_gather(src_hbm, dst, sem, index_of, n_rows):
    for r in range(n_rows):
        pltpu.make_async_copy(src_hbm.at[pl.ds(index_of(r), 1)], dst.at[pl.ds(r, 1)], sem).start()


def _row_gather_wait(src_hbm, dst, sem):
    pltpu.make_async_copy(src_hbm.at[pl.ds(0, dst.shape[0])], dst, sem).wait()


def _ffn_kernel(te_ref, nu_ref, ne_ref, t0_ref, t1_ref, ws_ref, tok_ref,
                x1_hbm, wg_hbm, wu_hbm, wd_hbm, ys_ref,
                xbuf, xsem, wgb, wub, wdb, sg, su, sd, wsem, *, layer):
    i = pl.program_id(0)
    n_used = nu_ref[0]
    tm = xbuf.shape[1]
    slot = i % 2

    def part_copies(e, t, s):
        return (pltpu.make_async_copy(wg_hbm.at[layer, e, t], sg.at[s], wsem.at[s]),
                pltpu.make_async_copy(wu_hbm.at[layer, e, t], su.at[s], wsem.at[s]),
                pltpu.make_async_copy(wd_hbm.at[layer, e, t], sd.at[s], wsem.at[s]))

    def start_part(e, t, s):
        for c in part_copies(e, t, s):
            c.start(priority=1)

    def finish_part(e, t, s, w):
        for c in part_copies(e, t, s):
            c.wait()
        rin, rout = sg.shape[1], sd.shape[1]
        wgb[w, pl.ds(pl.multiple_of(t * rin, rin), rin), :] = sg[s].astype(BF16)
        wub[w, pl.ds(pl.multiple_of(t * rin, rin), rin), :] = su[s].astype(BF16)
        wdb[w, pl.ds(pl.multiple_of(t * rout, rout), rout), :] = sd[s].astype(BF16)

    def load_parts(e, t_lo, t_hi, w):
        def step(t, c):
            s = (t - t_lo) % 2
            finish_part(e, t, s, w)

            @pl.when(t + 2 < t_hi)
            def _():
                start_part(e, t + 2, s)
            return c
        lax.fori_loop(t_lo, t_hi, step, 0)

    def start_first_parts(e, t_lo, t_hi):
        @pl.when(t_lo < t_hi)
        def _():
            start_part(e, t_lo, 0)

        @pl.when(t_lo + 1 < t_hi)
        def _():
            start_part(e, t_lo + 1, 1)

    @pl.when(i == 0)
    def _():
        def first(r, c):
            pltpu.make_async_copy(x1_hbm.at[pl.ds(tok_ref[r], 1)], xbuf.at[0, pl.ds(r, 1)], xsem.at[0]).start()
            return c
        lax.fori_loop(0, tm, first, 0)
        start_first_parts(te_ref[0], 0, W_PARTS)
        load_parts(te_ref[0], 0, W_PARTS, 0)

    @pl.when(i < n_used)
    def _():
        w = ws_ref[i]
        e_next, t_lo, t_hi = ne_ref[i], t0_ref[i], t1_ref[i]
        nxt = jnp.minimum(i + 1, n_used - 1)
        _row_gather(x1_hbm, xbuf.at[1 - slot], xsem.at[1 - slot], lambda r: tok_ref[nxt * tm + r], tm)
        start_first_parts(e_next, t_lo, t_hi)
        _row_gather_wait(x1_hbm, xbuf.at[slot], xsem.at[slot])
        x = xbuf[slot].astype(BF16)
        g = jnp.dot(x, wgb[w], preferred_element_type=F32)
        u = jnp.dot(x, wub[w], preferred_element_type=F32)
        h = ((g * jax.nn.sigmoid(g)) * u).astype(BF16)
        ys_ref[...] = jnp.dot(h, wdb[w], preferred_element_type=F32)
        load_parts(e_next, t_lo, t_hi, 1 - w)

    @pl.when(i == n_used - 1)
    def _():
        _row_gather_wait(x1_hbm, xbuf.at[1 - slot], xsem.at[1 - slot])

    @pl.when(i >= n_used)
    def _():
        ys_ref[...] = jnp.zeros_like(ys_ref)


def _ffn_call(sched, row_token, x1, w_gate, w_up, w_down, layer):
    d = x1.shape[1]
    p = row_token.shape[0]
    depth, n_e, _, f = w_gate.shape
    tm = TOK_ROWS
    wg = w_gate.reshape(depth, n_e, W_PARTS, d // W_PARTS, f)
    wu = w_up.reshape(depth, n_e, W_PARTS, d // W_PARTS, f)
    wd = w_down.reshape(depth, n_e, W_PARTS, f // W_PARTS, d)
    any_spec = pl.BlockSpec(memory_space=pl.ANY)
    return pl.pallas_call(
        functools.partial(_ffn_kernel, layer=layer),
        grid_spec=pltpu.PrefetchScalarGridSpec(
            num_scalar_prefetch=7,
            grid=(p // tm,),
            in_specs=[any_spec, any_spec, any_spec, any_spec],
            out_specs=pl.BlockSpec((tm, d), lambda i, *_: (i, 0)),
            scratch_shapes=[pltpu.VMEM((2, tm, d), F32), pltpu.SemaphoreType.DMA((2,)),
                            pltpu.VMEM((2, d, f), BF16), pltpu.VMEM((2, d, f), BF16),
                            pltpu.VMEM((2, f, d), BF16),
                            pltpu.VMEM((2, d // W_PARTS, f), F32), pltpu.VMEM((2, d // W_PARTS, f), F32),
                            pltpu.VMEM((2, f // W_PARTS, d), F32), pltpu.SemaphoreType.DMA((2,))],
        ),
        out_shape=jax.ShapeDtypeStruct((p, d), F32),
        compiler_params=pltpu.CompilerParams(
            dimension_semantics=("arbitrary",), vmem_limit_bytes=VMEM_LIMIT),
        name="expert_ffn",
    )(*sched, row_token, x1, wg, wu, wd)


def _norm2_kernel(pos_ref, x1_ref, gw_ref, g_ref, b_ref, ys_hbm, x2_ref, x2b_ref, ybuf, sem, *, alpha):
    i = pl.program_id(0)
    n_steps = pl.num_programs(0)
    tm = x1_ref.shape[0]
    n = tm * n_steps
    slot = i % 2

    @pl.when(i == 0)
    def _():
        def first(r, c):
            for k in range(TOP_K):
                pltpu.make_async_copy(ys_hbm.at[pl.ds(pos_ref[k * n + r], 1)], ybuf.at[0, k, pl.ds(r, 1)],
                                      sem.at[0]).start()
            return c
        lax.fori_loop(0, tm, first, 0)

    nxt = jnp.minimum(i + 1, n_steps - 1)
    for k in range(TOP_K):
        _row_gather(ys_hbm, ybuf.at[1 - slot, k], sem.at[1 - slot],
                    lambda r, k=k: pos_ref[k * n + nxt * tm + r], tm)
    for k in range(TOP_K):
        _row_gather_wait(ys_hbm, ybuf.at[slot, k], sem.at[slot])
    y = None
    for k in range(TOP_K):
        term = gw_ref[:, k:k + 1] * ybuf[slot, k]
        y = term if y is None else y + term
    x2 = _ln(alpha * x1_ref[...] + y, g_ref[...], b_ref[...])
    x2_ref[...] = x2
    x2b_ref[...] = x2.astype(x2b_ref.dtype)

    @pl.when(i == n_steps - 1)
    def _():
        for k in range(TOP_K):
            _row_gather_wait(ys_hbm, ybuf.at[1 - slot, k], sem.at[1 - slot])


def _norm2_call(pos_flat, x1, gwc, g, b, ys, alpha):
    n, d = x1.shape
    tm = TOK_ROWS
    rows = pl.BlockSpec((tm, d), lambda i, pos: (i, 0))
    const = pl.BlockSpec((1, d), lambda i, pos: (0, 0))
    return pl.pallas_call(
        functools.partial(_norm2_kernel, alpha=alpha),
        grid_spec=pltpu.PrefetchScalarGridSpec(
            num_scalar_prefetch=1,
            grid=(n // tm,),
            in_specs=[rows, pl.BlockSpec((tm, LANES), lambda i, pos: (i, 0)), const, const,
                      pl.BlockSpec(memory_space=pl.ANY)],
            out_specs=[rows, rows],
            scratch_shapes=[pltpu.VMEM((2, TOP_K, tm, d), F32), pltpu.SemaphoreType.DMA((2,))],
        ),
        out_shape=[jax.ShapeDtypeStruct((n, d), F32), jax.ShapeDtypeStruct((n, d), BF16)],
        compiler_params=pltpu.CompilerParams(
            dimension_semantics=("arbitrary",), vmem_limit_bytes=VMEM_LIMIT),
        name="combine_norm2",
    )(pos_flat, x1, gwc, g, b, ys)


def _spatial_tiles(w_s, b_s, seq):
    reps = ROW_SUB // seq
    wt = jnp.tile(w_s[:, :seq, :seq], (1, reps, reps))
    bt = jnp.tile(b_s[:, :seq], (1, reps))
    return wt, jnp.broadcast_to(bt[:, :, None], bt.shape + (LANES,))


def _history_rows(hist, nb):
    s, _, dc = hist.shape
    h = jnp.pad(hist, ((0, 0), (SUBLANES - (CONV_W - 1), 0), (0, 0)))
    return h.reshape(s, SUBLANES, nb, dc // nb).transpose(0, 2, 1, 3)


def _state_rows(cs):
    s, nb, _, cb = cs.shape
    return cs[:, :, SUBLANES - (CONV_W - 1):, :].transpose(0, 2, 1, 3).reshape(s, CONV_W - 1, nb * cb)


def kernel(x_prompt, x_sample, state_conv, w_in, b_in, conv_w, conv_b, sgu_ln_g, sgu_ln_b, w_spatial,
           b_spatial, w_a_out, w_b_out, w_o, ln1_g, ln1_b, router_w, router_bias, w_gate, w_up, w_down,
           ln2_g, ln2_b):
    depth, d, n_col = w_in.shape
    batch, seq, _ = x_prompt.shape
    dec_batch, dec_seq, _ = x_sample.shape
    dc = conv_w.shape[2]
    dsg = sgu_ln_g.shape[1]
    n_experts = router_w.shape[1]
    assert batch == 1 and n_col == 3 * dc + 2 * dsg + 2 * d
    assert dc % COL_BLOCK == 0 and dsg % COL_BLOCK == 0 and d % COL_BLOCK == 0
    assert dsg == SGU_GROUPS * LANES and n_experts % N_EXPERT_GROUPS == 0
    n_p, n_s = batch * seq, dec_batch * dec_seq
    n = n_p + n_s
    tm_p = min(MIX_ROWS, n_p)
    assert n_p % tm_p == 0 and n_s == ROW_SUB and ROW_SUB % dec_seq == 0 and n_p % TOK_ROWS == 0
    assert seq % SGU_CHUNK == 0 and SGU_CHUNK % dec_seq == 0
    alpha = (2.0 * depth) ** 0.25
    nb, ns, ng = dc // COL_BLOCK, dsg // COL_BLOCK, d // COL_BLOCK
    cfg = MixCfg(tm_p, n_p // tm_p, seq, dec_seq, n_s, nb, ns, ng)

    n_pairs = TOP_K * n
    n_tiles = n_pairs // TOK_ROWS + n_experts
    assert n_pairs % TOK_ROWS == 0

    x = jnp.concatenate([x_prompt.reshape(n_p, d), x_sample.reshape(n_s, d)], axis=0)
    xb = x.astype(BF16)
    rw = jnp.pad(router_w, ((0, 0), (0, LANES - n_experts))).astype(BF16)
    rb = jnp.broadcast_to(router_bias[:, None], (n_experts, TOK_ROWS))
    zero_hist = jnp.zeros((batch, CONV_W - 1, dc), F32)

    conv_p, conv_s, v_s = [], [], []
    for l in range(depth):
        w_in_b = w_in[l].astype(BF16)
        bias = b_in[l][None, :]
        cw = jnp.concatenate([conv_w[l], conv_b[l][None, :],
                              jnp.zeros((SUBLANES - CONV_W - 1, dc), F32)], axis=0)
        cw = cw.reshape(SUBLANES, nb, COL_BLOCK).transpose(1, 0, 2)
        lng, lnb = sgu_ln_g[l][None, :], sgu_ln_b[l][None, :]
        wt_p, bt_p = _spatial_tiles(w_spatial[l], b_spatial[l], min(seq, SGU_CHUNK))
        wt_s, bt_s = _spatial_tiles(w_spatial[l], b_spatial[l], min(dec_seq, SGU_CHUNK))

        ya, yb, ga, gb, cs_p, cs_s, v_rows = _mixer_call(
            xb, w_in_b, bias, _history_rows(zero_hist, nb), _history_rows(state_conv[l], nb), cw, lng, lnb,
            wt_p, bt_p, wt_s, bt_s, cfg)
        conv_p.append(_state_rows(cs_p))
        conv_s.append(_state_rows(cs_s))
        v_s.append(v_rows.reshape(dec_batch, dec_seq, dsg))

        x1, eidx, gw, rank, cnt = _merge_call(
            ya, yb, ga, gb, x, w_a_out[l].astype(BF16), w_b_out[l].astype(BF16), w_o[l].astype(BF16),
            ln1_g[l][None, :], ln1_b[l][None, :], rw, rb, alpha, n_experts)

        counts = cnt[:, 0].astype(jnp.int32)
        padded = ((counts + TOK_ROWS - 1) // TOK_ROWS) * TOK_ROWS
        seg_end = jnp.cumsum(padded)
        seg_start = seg_end - padded
        n_used = (seg_end[-1:] // TOK_ROWS).astype(jnp.int32)
        tile_row0 = jnp.arange(n_tiles, dtype=jnp.int32) * TOK_ROWS
        tile_expert = jnp.minimum(
            jnp.sum(tile_row0[:, None] >= seg_end[None, :], axis=1), n_experts - 1).astype(jnp.int32)
        e_ids = jnp.arange(n_experts, dtype=jnp.int32)
        pos = rank[:TOP_K] + jnp.sum(
            jnp.where(eidx[:TOP_K, :, None] == e_ids, seg_start.astype(jnp.int32), 0), axis=-1)

        n_t = (padded // TOK_ROWS).astype(jnp.int32)
        later = jnp.where(n_t > 0, e_ids, n_experts)
        nxt_e = jnp.concatenate([jnp.flip(lax.cummin(jnp.flip(later)))[1:],
                                 jnp.full((1,), n_experts, jnp.int32)])
        group = (jnp.cumsum(n_t > 0) - 1).astype(jnp.int32)
        tile_i = jnp.arange(n_tiles, dtype=jnp.int32)
        q = tile_i - (seg_start // TOK_ROWS).astype(jnp.int32)[tile_expert]
        g_t = jnp.maximum(n_t[tile_expert], 1)
        loads = jnp.logical_and(tile_i < n_used[0], nxt_e[tile_expert] < n_experts)
        t_lo = jnp.where(loads, (W_PARTS * q) // g_t, 0).astype(jnp.int32)
        t_hi = jnp.where(loads, (W_PARTS * (q + 1)) // g_t, 0).astype(jnp.int32)
        sched = (tile_expert, n_used, jnp.minimum(nxt_e[tile_expert], n_experts - 1).astype(jnp.int32),
                 t_lo, t_hi, (group[tile_expert] % 2).astype(jnp.int32))

        tok = jnp.tile(jnp.arange(n, dtype=jnp.int32), TOP_K)
        pos_flat = pos.reshape(-1)
        row_token = jnp.zeros((n_tiles * TOK_ROWS,), jnp.int32).at[pos_flat].set(tok)
        ys = _ffn_call(sched, row_token, x1, w_gate, w_up, w_down, l)
        gwc = jnp.pad(gw[:TOP_K].T, ((0, 0), (0, LANES - TOP_K)))
        x, xb = _norm2_call(pos_flat, x1, gwc, ln2_g[l][None, :], ln2_b[l][None, :], ys, alpha)

    return (x[:n_p].reshape(batch, seq, d), x[n_p:].reshape(dec_batch, dec_seq, d),
            jnp.stack(conv_p), jnp.stack(conv_s), jnp.stack(v_s))
```

```python
import functools
from typing import NamedTuple

import jax
import jax.numpy as jnp
from jax import lax
from jax.experimental import pallas as pl
from jax.experimental.pallas import tpu as pltpu

CHUNK = 64
SGU_CHUNK = 128
SGU_GROUPS = 8
N_EXPERT_GROUPS = 4
TOP_K = 2
LN_EPS = 1e-5
CONV_W = 3

LANES = 128
SUBLANES = 8
COL_BLOCK = 512
ROW_SUB = 256
MIX_ROWS = 1024
TOK_ROWS = 256
W_PARTS = 8
PREFETCH = 2
VMEM_LIMIT = 56 * 1024 * 1024

F32 = jnp.float32
BF16 = jnp.bfloat16


class MixCfg(NamedTuple):
    tm: int
    n_p_tiles: int
    seq: int
    dec_seq: int
    n_s: int
    nb: int
    ns: int
    ng: int


def _ln(z, g, b):
    mu = jnp.mean(z, axis=-1, keepdims=True)
    d = z - mu
    var = jnp.mean(d * d, axis=-1, keepdims=True)
    return d * lax.rsqrt(var + LN_EPS) * g + b


def _conv_rows(hin, p2, p1, cw_ref, jj):
    rows = lax.broadcasted_iota(jnp.int32, hin.shape, 0)
    r1 = pltpu.roll(hin, 1, 0)
    r2 = pltpu.roll(hin, 2, 0)
    sh1 = jnp.where(rows == 0, p1, r1)
    sh2 = jnp.where(rows == 0, p2, jnp.where(rows == 1, p1, r2))
    out = cw_ref[jj, 3:4, :] + cw_ref[jj, 0:1, :] * sh2
    out = out + cw_ref[jj, 1:2, :] * sh1
    return out + cw_ref[jj, 2:3, :] * hin


def _spatial_keep(seq):
    t = lax.broadcasted_iota(jnp.int32, (ROW_SUB, ROW_SUB), 0)
    s = lax.broadcasted_iota(jnp.int32, (ROW_SUB, ROW_SUB), 1)
    same = (t // seq) == (s // seq)
    causal = ((s % seq) // CHUNK) <= ((t % seq) // CHUNK)
    return jnp.logical_and(same, causal)


def _mixer_kernel(x_ref, w_ref, bias_ref, prevp_ref, prevs_ref, cw_ref, lng_ref, lnb_ref,
                  wtp_ref, btp_ref, wts_ref, bts_ref,
                  ya_ref, yb_ref, ga_ref, gb_ref, csp_ref, css_ref, v_ref,
                  bsc, csc, usc, vsc, carry, wm_sc, bt_sc, *, cfg: MixCfg):
    i = pl.program_id(0)
    j = pl.program_id(1)
    nb, ns, ng = cfg.nb, cfg.ns, cfg.ng
    p_c, p_h, p_u, p_v = nb, 2 * nb, 3 * nb, 3 * nb + ns
    p_ga = p_v + ns
    p_gb = p_ga + ng
    cb = w_ref.shape[1]
    gpb = cb // LANES
    is_prompt = i < cfg.n_p_tiles
    is_sample = i == cfg.n_p_tiles

    def sub_rows(r):
        return pl.ds(r * ROW_SUB, ROW_SUB)

    def proj(r):
        rows = sub_rows(r)
        acc = jnp.dot(x_ref[rows, :], w_ref[...], preferred_element_type=F32)
        return rows, acc + bias_ref[...]

    def for_sub(body):
        @pl.when(is_prompt)
        def _():
            for r in range(cfg.tm // ROW_SUB):
                body(r, False)

        @pl.when(is_sample)
        def _():
            for r in range(cfg.n_s // ROW_SUB):
                body(r, True)

    @pl.when(jnp.logical_and(i == 0, j == 0))
    def _():
        carry[...] = prevp_ref[0]

    @pl.when(j < p_c)
    def _():
        def body(r, sample):
            rows, a = proj(r)
            bsc[j, rows, :] = a
        for_sub(body)

    @pl.when(jnp.logical_and(j >= p_c, j < p_h))
    def _():
        def body(r, sample):
            rows, a = proj(r)
            csc[j - p_c, rows, :] = a
        for_sub(body)

    @pl.when(jnp.logical_and(j >= p_h, j < p_u))
    def _():
        jj = j - p_h

        def body(r, sample):
            rows, a = proj(r)
            hin = csc[jj, rows, :] * a
            if not sample:
                out = _conv_rows(hin, carry[jj, 6:7, :], carry[jj, 7:8, :], cw_ref, jj)
                ya_ref[rows, :] = (bsc[jj, rows, :] * out).astype(ya_ref.dtype)
                carry[jj] = hin[ROW_SUB - SUBLANES:, :]
                csp_ref[0, jj] = hin[ROW_SUB - SUBLANES:, :]
            else:
                sr = cfg.dec_seq
                for s in range(ROW_SUB // sr):
                    hs = hin[s * sr:(s + 1) * sr, :]
                    st = r * (ROW_SUB // sr) + s
                    out = _conv_rows(hs, prevs_ref[st, jj, 6:7, :], prevs_ref[st, jj, 7:8, :], cw_ref, jj)
                    srow = pl.ds(r * ROW_SUB + s * sr, sr)
                    ya_ref[srow, :] = (bsc[jj, srow, :] * out).astype(ya_ref.dtype)
                    css_ref[st, jj] = hs[sr - SUBLANES:, :]
        for_sub(body)

    @pl.when(jnp.logical_and(j >= p_u, j < p_v))
    def _():
        def body(r, sample):
            rows, a = proj(r)
            usc[j - p_u, rows, :] = jax.nn.gelu(a)
        for_sub(body)

    @pl.when(jnp.logical_and(j >= p_v, j < p_ga))
    def _():
        def body(r, sample):
            rows, a = proj(r)
            vsc[j - p_v, rows, :] = jax.nn.gelu(a)
        for_sub(body)

    @pl.when(j == p_ga - 1)
    def _():
        @pl.when(is_prompt)
        def _():
            keep = _spatial_keep(min(cfg.seq, SGU_CHUNK))
            for g in range(SGU_GROUPS):
                wm_sc[g] = jnp.where(keep, wtp_ref[g], 0.0).astype(wm_sc.dtype)
            bt_sc[...] = btp_ref[...]

        @pl.when(is_sample)
        def _():
            keep = _spatial_keep(min(cfg.dec_seq, SGU_CHUNK))
            for g in range(SGU_GROUPS):
                wm_sc[g] = jnp.where(keep, wts_ref[g], 0.0).astype(wm_sc.dtype)
            bt_sc[...] = bts_ref[...]

        ds = ns * cb

        def body(r, sample):
            rows = sub_rows(r)
            parts = [vsc[c, rows, :] for c in range(ns)]
            mu = sum(jnp.sum(p, axis=-1, keepdims=True) for p in parts) / ds
            cen = [p - mu for p in parts]
            var = sum(jnp.sum(c * c, axis=-1, keepdims=True) for c in cen) / ds
            inv = lax.rsqrt(var + LN_EPS)
            for c in range(ns):
                cols = slice(c * cb, (c + 1) * cb)
                vn = cen[c] * inv * lng_ref[:, cols] + lnb_ref[:, cols]
                if sample:
                    v_ref[rows, cols] = vn
                vnb = vn.astype(BF16)
                for q in range(gpb):
                    g = c * gpb + q
                    lanes = slice(q * LANES, (q + 1) * LANES)
                    sp = jnp.dot(wm_sc[g], vnb[:, lanes], preferred_element_type=F32) + bt_sc[g]
                    yb_ref[rows, g * LANES:(g + 1) * LANES] = (usc[c, rows, lanes] * sp).astype(yb_ref.dtype)
        for_sub(body)

    @pl.when(jnp.logical_and(j >= p_ga, j < p_gb))
    def _():
        def body(r, sample):
            rows, a = proj(r)
            ga_ref[rows, :] = jax.nn.sigmoid(a)
        for_sub(body)

    @pl.when(j >= p_gb)
    def _():
        def body(r, sample):
            rows, a = proj(r)
            gb_ref[rows, :] = jax.nn.sigmoid(a)
        for_sub(body)


def _mixer_call(xb, w_in, b_in, prev_p, prev_s, cw, lng, lnb, wt_p, bt_p, wt_s, bt_s, cfg):
    n, d = xb.shape
    cb = COL_BLOCK
    nb, ns, ng = cfg.nb, cfg.ns, cfg.ng
    dc, dsg = nb * cb, ns * cb
    n_j = w_in.shape[1] // cb
    n_i = cfg.n_p_tiles + 1
    p_h, p_ga, p_gb = 2 * nb, 3 * nb + 2 * ns, 3 * nb + 2 * ns + ng

    def const(shape):
        return pl.BlockSpec(shape, lambda i, j: (0,) * len(shape))

    in_specs = [
        pl.BlockSpec((cfg.tm, d), lambda i, j: (i, 0)),
        pl.BlockSpec((d, cb), lambda i, j: (0, j)),
        pl.BlockSpec((1, cb), lambda i, j: (0, j)),
        const(prev_p.shape), const(prev_s.shape), const(cw.shape), const(lng.shape), const(lnb.shape),
        const(wt_p.shape), const(bt_p.shape), const(wt_s.shape), const(bt_s.shape),
    ]
    out_shape = [
        jax.ShapeDtypeStruct((n, dc), BF16),
        jax.ShapeDtypeStruct((n, dsg), BF16),
        jax.ShapeDtypeStruct((n, d), F32),
        jax.ShapeDtypeStruct((n, d), F32),
        jax.ShapeDtypeStruct(prev_p.shape, F32),
        jax.ShapeDtypeStruct(prev_s.shape, F32),
        jax.ShapeDtypeStruct((cfg.n_s, dsg), F32),
    ]
    out_specs = [
        pl.BlockSpec((cfg.tm, cb), lambda i, j: (i, jnp.clip(j - p_h, 0, nb - 1))),
        pl.BlockSpec((cfg.tm, dsg), lambda i, j: (i, 0)),
        pl.BlockSpec((cfg.tm, cb), lambda i, j: (i, jnp.clip(j - p_ga, 0, ng - 1))),
        pl.BlockSpec((cfg.tm, cb), lambda i, j: (i, jnp.clip(j - p_gb, 0, ng - 1))),
        const(prev_p.shape), const(prev_s.shape), const((cfg.n_s, dsg)),
    ]
    scratch = [
        pltpu.VMEM((nb, cfg.tm, cb), F32),
        pltpu.VMEM((nb, cfg.tm, cb), F32),
        pltpu.VMEM((ns, cfg.tm, cb), F32),
        pltpu.VMEM((ns, cfg.tm, cb), F32),
        pltpu.VMEM((nb, SUBLANES, cb), F32),
        pltpu.VMEM((SGU_GROUPS, ROW_SUB, ROW_SUB), BF16),
        pltpu.VMEM((SGU_GROUPS, ROW_SUB, LANES), F32),
    ]
    return pl.pallas_call(
        functools.partial(_mixer_kernel, cfg=cfg),
        grid=(n_i, n_j),
        in_specs=in_specs,
        out_specs=out_specs,
        out_shape=out_shape,
        scratch_shapes=scratch,
        compiler_params=pltpu.CompilerParams(
            dimension_semantics=("arbitrary", "arbitrary"), vmem_limit_bytes=VMEM_LIMIT),
        name="mixer",
    )(xb, w_in, b_in, prev_p, prev_s, cw, lng, lnb, wt_p, bt_p, wt_s, bt_s)


def _merge_kernel(ya_ref, yb_ref, ga_ref, gb_ref, x_ref, wa_ref, wb_ref, wo_ref, g1_ref, b1_ref,
                  rw_ref, rb_ref, x1_ref, eidx_ref, gw_ref, rank_ref, cnt_ref, run, *,
                  alpha, n_experts):
    i = pl.program_id(0)
    tm = x_ref.shape[0]
    epg = n_experts // N_EXPERT_GROUPS

    @pl.when(i == 0)
    def _():
        run[...] = jnp.zeros_like(run)

    a = jnp.dot(ya_ref[...], wa_ref[...], preferred_element_type=F32)
    b = jnp.dot(yb_ref[...], wb_ref[...], preferred_element_type=F32)
    merged = ga_ref[...] * a + gb_ref[...] * b
    z = alpha * x_ref[...] + jnp.dot(merged.astype(BF16), wo_ref[...], preferred_element_type=F32)
    x1 = _ln(z, g1_ref[...], b1_ref[...])
    x1_ref[...] = x1

    logits = jnp.dot(x1.astype(BF16), rw_ref[...], preferred_element_type=F32)
    lt = logits.T[:n_experts, :]
    ex = jnp.exp(lt - jnp.max(lt, axis=0, keepdims=True))
    sc = ex / jnp.sum(ex, axis=0, keepdims=True)
    sel = sc + rb_ref[...]
    sel_r = [sel[e:e + 1, :] for e in range(n_experts)]
    sc_r = [sc[e:e + 1, :] for e in range(n_experts)]

    def pair_max(v):
        best = None
        for p in range(len(v)):
            for q in range(p + 1, len(v)):
                s = v[p] + v[q]
                best = s if best is None else jnp.maximum(best, s)
        return best

    grp = [pair_max(sel_r[g * epg:(g + 1) * epg]) for g in range(N_EXPERT_GROUPS)]
    gi = jnp.zeros((1, tm), jnp.int32)
    gbest = grp[0]
    for g in range(1, N_EXPERT_GROUPS):
        better = grp[g] > gbest
        gi = jnp.where(better, g, gi)
        gbest = jnp.where(better, grp[g], gbest)

    def pick(rows_, idx):
        out = rows_[-1]
        for c in range(len(rows_) - 2, -1, -1):
            out = jnp.where(idx == c, rows_[c], out)
        return out

    vk = [pick([sel_r[g * epg + q] for g in range(N_EXPERT_GROUPS)], gi) for q in range(epg)]
    pk = [pick([sc_r[g * epg + q] for g in range(N_EXPERT_GROUPS)], gi) for q in range(epg)]

    i1 = jnp.zeros((1, tm), jnp.int32)
    b1 = vk[0]
    for q in range(1, epg):
        better = vk[q] > b1
        i1 = jnp.where(better, q, i1)
        b1 = jnp.where(better, vk[q], b1)
    i2 = jnp.zeros((1, tm), jnp.int32)
    b2 = jnp.full((1, tm), -jnp.inf, F32)
    for q in range(epg):
        cand = jnp.logical_and(i1 != q, vk[q] > b2)
        i2 = jnp.where(cand, q, i2)
        b2 = jnp.where(cand, vk[q], b2)
    p1 = pick(pk, i1)
    p2 = pick(pk, i2)
    den = p1 + p2
    e1 = gi * epg + i1
    e2 = gi * epg + i2

    eio = lax.broadcasted_iota(jnp.int32, (n_experts, tm), 0)
    oh1 = (eio == e1).astype(F32)
    oh2 = (eio == e2).astype(F32)
    ts = lax.broadcasted_iota(jnp.int32, (tm, tm), 0)
    tt = lax.broadcasted_iota(jnp.int32, (tm, tm), 1)
    upper = (ts <= tt).astype(BF16)
    inc1 = jnp.dot(oh1.astype(BF16), upper, preferred_element_type=F32)
    inc2 = jnp.dot(oh2.astype(BF16), upper, preferred_element_type=F32)
    tot1 = jnp.sum(oh1, axis=1, keepdims=True)
    tot2 = jnp.sum(oh2, axis=1, keepdims=True)
    base = run[:, 0:1]
    r1 = jnp.sum(oh1 * (inc1 - 1.0 + base), axis=0, keepdims=True)
    r2 = jnp.sum(oh2 * (inc2 - 1.0 + base + tot1), axis=0, keepdims=True)
    new_run = run[...] + tot1 + tot2
    run[...] = new_run
    cnt_ref[...] = new_run

    eidx_ref[...] = jnp.zeros_like(eidx_ref)
    eidx_ref[0:1, :] = e1
    eidx_ref[1:2, :] = e2
    gw_ref[...] = jnp.zeros_like(gw_ref)
    gw_ref[0:1, :] = p1 / den
    gw_ref[1:2, :] = p2 / den
    rank_ref[...] = jnp.zeros_like(rank_ref)
    rank_ref[0:1, :] = r1.astype(jnp.int32)
    rank_ref[1:2, :] = r2.astype(jnp.int32)


def _merge_call(ya, yb, ga, gb, x, wa, wb, wo, g1, b1, rw, rb, alpha, n_experts):
    n, d = x.shape
    tm = TOK_ROWS
    dc, dsg = ya.shape[1], yb.shape[1]

    def rows(c):
        return pl.BlockSpec((tm, c), lambda i: (i, 0))

    def const(shape):
        return pl.BlockSpec(shape, lambda i: (0,) * len(shape))

    def lanes(r):
        return pl.BlockSpec((r, tm), lambda i: (0, i))

    return pl.pallas_call(
        functools.partial(_merge_kernel, alpha=alpha, n_experts=n_experts),
        grid=(n // tm,),
        in_specs=[rows(dc), rows(dsg), rows(d), rows(d), rows(d),
                  const(wa.shape), const(wb.shape), const(wo.shape), const(g1.shape), const(b1.shape),
                  const(rw.shape), const(rb.shape)],
        out_specs=[rows(d), lanes(SUBLANES), lanes(SUBLANES), lanes(SUBLANES),
                   const((n_experts, LANES))],
        out_shape=[jax.ShapeDtypeStruct((n, d), F32),
                   jax.ShapeDtypeStruct((SUBLANES, n), jnp.int32),
                   jax.ShapeDtypeStruct((SUBLANES, n), F32),
                   jax.ShapeDtypeStruct((SUBLANES, n), jnp.int32),
                   jax.ShapeDtypeStruct((n_experts, LANES), F32)],
        scratch_shapes=[pltpu.VMEM((n_experts, LANES), F32)],
        compiler_params=pltpu.CompilerParams(
            dimension_semantics=("arbitrary",), vmem_limit_bytes=VMEM_LIMIT),
        name="merge_route",
    )(ya, yb, ga, gb, x, wa, wb, wo, g1, b1, rw, rb)


def _row_gather(src_hbm, dst, sem, index_of, n_rows):
    for r in range(n_rows):
        pltpu.make_async_copy(src_hbm.at[pl.ds(index_of(r), 1)], dst.at[pl.ds(r, 1)], sem).start()


def _row_gather_wait(src_hbm, dst, sem):
    pltpu.make_async_copy(src_hbm.at[pl.ds(0, dst.shape[0])], dst, sem).wait()


def _ffn_kernel(te_ref, nu_ref, ne_ref, t0_ref, t1_ref, ws_ref, tok_ref,
                x1_hbm, wg_hbm, wu_hbm, wd_hbm, ys_ref,
                xbuf, xsem, wgb, wub, wdb, sg, su, sd, wsem, *, layer):
    i = pl.program_id(0)
    n_used = nu_ref[0]
    tm = xbuf.shape[1]
    slot = i % (PREFETCH + 1)

    def part_copies(e, t, s):
        return (pltpu.make_async_copy(wg_hbm.at[layer, e, t], sg.at[s], wsem.at[s]),
                pltpu.make_async_copy(wu_hbm.at[layer, e, t], su.at[s], wsem.at[s]),
                pltpu.make_async_copy(wd_hbm.at[layer, e, t], sd.at[s], wsem.at[s]))

    def start_part(e, t, s):
        for c in part_copies(e, t, s):
            c.start(priority=1)

    def finish_part(e, t, s, w):
        for c in part_copies(e, t, s):
            c.wait()
        rin, rout = sg.shape[1], sd.shape[1]
        wgb[w, pl.ds(pl.multiple_of(t * rin, rin), rin), :] = sg[s].astype(BF16)
        wub[w, pl.ds(pl.multiple_of(t * rin, rin), rin), :] = su[s].astype(BF16)
        wdb[w, pl.ds(pl.multiple_of(t * rout, rout), rout), :] = sd[s].astype(BF16)

    def load_parts(e, t_lo, t_hi, w):
        def step(t, c):
            s = (t - t_lo) % 2
            finish_part(e, t, s, w)

            @pl.when(t + 2 < t_hi)
            def _():
                start_part(e, t + 2, s)
            return c
        lax.fori_loop(t_lo, t_hi, step, 0)

    def start_first_parts(e, t_lo, t_hi):
        @pl.when(t_lo < t_hi)
        def _():
            start_part(e, t_lo, 0)

        @pl.when(t_lo + 1 < t_hi)
        def _():
            start_part(e, t_lo + 1, 1)

    @pl.when(i == 0)
    def _():
        for a in range(PREFETCH):
            tile = jnp.minimum(a, n_used - 1)

            def first(r, c, a=a, tile=tile):
                pltpu.make_async_copy(x1_hbm.at[pl.ds(tok_ref[tile * tm + r], 1)], xbuf.at[a, pl.ds(r, 1)],
                                      xsem.at[a]).start()
                return c
            lax.fori_loop(0, tm, first, 0)
        start_first_parts(te_ref[0], 0, W_PARTS)
        load_parts(te_ref[0], 0, W_PARTS, 0)

    @pl.when(i < n_used)
    def _():
        w = ws_ref[i]
        e_next, t_lo, t_hi = ne_ref[i], t0_ref[i], t1_ref[i]
        start_first_parts(e_next, t_lo, t_hi)
        _row_gather_wait(x1_hbm, xbuf.at[slot], xsem.at[slot])
        x = xbuf[slot].astype(BF16)
        ahead = jnp.minimum(i + PREFETCH, n_used - 1)
        into = (i + PREFETCH) % (PREFETCH + 1)
        _row_gather(x1_hbm, xbuf.at[into], xsem.at[into], lambda r: tok_ref[ahead * tm + r], tm)
        g = jnp.dot(x, wgb[w], preferred_element_type=F32)
        u = jnp.dot(x, wub[w], preferred_element_type=F32)
        h = ((g * jax.nn.sigmoid(g)) * u).astype(BF16)
        ys_ref[...] = jnp.dot(h, wdb[w], preferred_element_type=F32)
        load_parts(e_next, t_lo, t_hi, 1 - w)

    @pl.when(i == n_used - 1)
    def _():
        for a in range(1, PREFETCH + 1):
            s = (i + a) % (PREFETCH + 1)
            _row_gather_wait(x1_hbm, xbuf.at[s], xsem.at[s])

    @pl.when(i >= n_used)
    def _():
        ys_ref[...] = jnp.zeros_like(ys_ref)


def _ffn_call(sched, row_token, x1, w_gate, w_up, w_down, layer):
    d = x1.shape[1]
    p = row_token.shape[0]
    depth, n_e, _, f = w_gate.shape
    tm = TOK_ROWS
    wg = w_gate.reshape(depth, n_e, W_PARTS, d // W_PARTS, f)
    wu = w_up.reshape(depth, n_e, W_PARTS, d // W_PARTS, f)
    wd = w_down.reshape(depth, n_e, W_PARTS, f // W_PARTS, d)
    any_spec = pl.BlockSpec(memory_space=pl.ANY)
    return pl.pallas_call(
        functools.partial(_ffn_kernel, layer=layer),
        grid_spec=pltpu.PrefetchScalarGridSpec(
            num_scalar_prefetch=7,
            grid=(p // tm,),
            in_specs=[any_spec, any_spec, any_spec, any_spec],
            out_specs=pl.BlockSpec((tm, d), lambda i, *_: (i, 0)),
            scratch_shapes=[pltpu.VMEM((PREFETCH + 1, tm, d), F32), pltpu.SemaphoreType.DMA((PREFETCH + 1,)),
                            pltpu.VMEM((2, d, f), BF16), pltpu.VMEM((2, d, f), BF16),
                            pltpu.VMEM((2, f, d), BF16),
                            pltpu.VMEM((2, d // W_PARTS, f), F32), pltpu.VMEM((2, d // W_PARTS, f), F32),
                            pltpu.VMEM((2, f // W_PARTS, d), F32), pltpu.SemaphoreType.DMA((2,))],
        ),
        out_shape=jax.ShapeDtypeStruct((p, d), F32),
        compiler_params=pltpu.CompilerParams(
            dimension_semantics=("arbitrary",), vmem_limit_bytes=VMEM_LIMIT),
        name="expert_ffn",
    )(*sched, row_token, x1, wg, wu, wd)


def _norm2_kernel(pos_ref, x1_ref, gw_ref, g_ref, b_ref, ys_hbm, x2_ref, x2b_ref, ybuf, sem, *, alpha):
    i = pl.program_id(0)
    n_steps = pl.num_programs(0)
    tm = x1_ref.shape[0]
    n = tm * n_steps
    slot = i % (PREFETCH + 1)

    @pl.when(i == 0)
    def _():
        for a in range(PREFETCH):
            tile = jnp.minimum(a, n_steps - 1)

            def first(r, c, a=a, tile=tile):
                for k in range(TOP_K):
                    pltpu.make_async_copy(ys_hbm.at[pl.ds(pos_ref[k * n + tile * tm + r], 1)],
                                          ybuf.at[a, k, pl.ds(r, 1)], sem.at[a]).start()
                return c
            lax.fori_loop(0, tm, first, 0)

    for k in range(TOP_K):
        _row_gather_wait(ys_hbm, ybuf.at[slot, k], sem.at[slot])
    ahead = jnp.minimum(i + PREFETCH, n_steps - 1)
    into = (i + PREFETCH) % (PREFETCH + 1)
    for k in range(TOP_K):
        _row_gather(ys_hbm, ybuf.at[into, k], sem.at[into],
                    lambda r, k=k: pos_ref[k * n + ahead * tm + r], tm)
    y = None
    for k in range(TOP_K):
        term = gw_ref[:, k:k + 1] * ybuf[slot, k]
        y = term if y is None else y + term
    x2 = _ln(alpha * x1_ref[...] + y, g_ref[...], b_ref[...])
    x2_ref[...] = x2
    x2b_ref[...] = x2.astype(x2b_ref.dtype)

    @pl.when(i == n_steps - 1)
    def _():
        for a in range(1, PREFETCH + 1):
            s = (i + a) % (PREFETCH + 1)
            for k in range(TOP_K):
                _row_gather_wait(ys_hbm, ybuf.at[s, k], sem.at[s])


def _norm2_call(pos_flat, x1, gwc, g, b, ys, alpha):
    n, d = x1.shape
    tm = TOK_ROWS
    rows = pl.BlockSpec((tm, d), lambda i, pos: (i, 0))
    const = pl.BlockSpec((1, d), lambda i, pos: (0, 0))
    return pl.pallas_call(
        functools.partial(_norm2_kernel, alpha=alpha),
        grid_spec=pltpu.PrefetchScalarGridSpec(
            num_scalar_prefetch=1,
            grid=(n // tm,),
            in_specs=[rows, pl.BlockSpec((tm, LANES), lambda i, pos: (i, 0)), const, const,
                      pl.BlockSpec(memory_space=pl.ANY)],
            out_specs=[rows, rows],
            scratch_shapes=[pltpu.VMEM((PREFETCH + 1, TOP_K, tm, d), F32),
                            pltpu.SemaphoreType.DMA((PREFETCH + 1,))],
        ),
        out_shape=[jax.ShapeDtypeStruct((n, d), F32), jax.ShapeDtypeStruct((n, d), BF16)],
        compiler_params=pltpu.CompilerParams(
            dimension_semantics=("arbitrary",), vmem_limit_bytes=VMEM_LIMIT),
        name="combine_norm2",
    )(pos_flat, x1, gwc, g, b, ys)


def _spatial_tiles(w_s, b_s, seq):
    reps = ROW_SUB // seq
    wt = jnp.tile(w_s[:, :seq, :seq], (1, reps, reps))
    bt = jnp.tile(b_s[:, :seq], (1, reps))
    return wt, jnp.broadcast_to(bt[:, :, None], bt.shape + (LANES,))


def _history_rows(hist, nb):
    s, _, dc = hist.shape
    h = jnp.pad(hist, ((0, 0), (SUBLANES - (CONV_W - 1), 0), (0, 0)))
    return h.reshape(s, SUBLANES, nb, dc // nb).transpose(0, 2, 1, 3)


def _state_rows(cs):
    s, nb, _, cb = cs.shape
    return cs[:, :, SUBLANES - (CONV_W - 1):, :].transpose(0, 2, 1, 3).reshape(s, CONV_W - 1, nb * cb)


def kernel(x_prompt, x_sample, state_conv, w_in, b_in, conv_w, conv_b, sgu_ln_g, sgu_ln_b, w_spatial,
           b_spatial, w_a_out, w_b_out, w_o, ln1_g, ln1_b, router_w, router_bias, w_gate, w_up, w_down,
           ln2_g, ln2_b):
    depth, d, n_col = w_in.shape
    batch, seq, _ = x_prompt.shape
    dec_batch, dec_seq, _ = x_sample.shape
    dc = conv_w.shape[2]
    dsg = sgu_ln_g.shape[1]
    n_experts = router_w.shape[1]
    assert batch == 1 and n_col == 3 * dc + 2 * dsg + 2 * d
    assert dc % COL_BLOCK == 0 and dsg % COL_BLOCK == 0 and d % COL_BLOCK == 0
    assert dsg == SGU_GROUPS * LANES and n_experts % N_EXPERT_GROUPS == 0
    n_p, n_s = batch * seq, dec_batch * dec_seq
    n = n_p + n_s
    tm_p = min(MIX_ROWS, n_p)
    assert n_p % tm_p == 0 and n_s == ROW_SUB and ROW_SUB % dec_seq == 0 and n_p % TOK_ROWS == 0
    assert seq % SGU_CHUNK == 0 and SGU_CHUNK % dec_seq == 0
    alpha = (2.0 * depth) ** 0.25
    nb, ns, ng = dc // COL_BLOCK, dsg // COL_BLOCK, d // COL_BLOCK
    cfg = MixCfg(tm_p, n_p // tm_p, seq, dec_seq, n_s, nb, ns, ng)

    n_pairs = TOP_K * n
    n_tiles = n_pairs // TOK_ROWS + n_experts
    assert n_pairs % TOK_ROWS == 0

    x = jnp.concatenate([x_prompt.reshape(n_p, d), x_sample.reshape(n_s, d)], axis=0)
    xb = x.astype(BF16)
    rw = jnp.pad(router_w, ((0, 0), (0, LANES - n_experts))).astype(BF16)
    rb = jnp.broadcast_to(router_bias[:, None], (n_experts, TOK_ROWS))
    zero_hist = jnp.zeros((batch, CONV_W - 1, dc), F32)

    conv_p, conv_s, v_s = [], [], []
    for l in range(depth):
        w_in_b = w_in[l].astype(BF16)
        bias = b_in[l][None, :]
        cw = jnp.concatenate([conv_w[l], conv_b[l][None, :],
                              jnp.zeros((SUBLANES - CONV_W - 1, dc), F32)], axis=0)
        cw = cw.reshape(SUBLANES, nb, COL_BLOCK).transpose(1, 0, 2)
        lng, lnb = sgu_ln_g[l][None, :], sgu_ln_b[l][None, :]
        wt_p, bt_p = _spatial_tiles(w_spatial[l], b_spatial[l], min(seq, SGU_CHUNK))
        wt_s, bt_s = _spatial_tiles(w_spatial[l], b_spatial[l], min(dec_seq, SGU_CHUNK))

        ya, yb, ga, gb, cs_p, cs_s, v_rows = _mixer_call(
            xb, w_in_b, bias, _history_rows(zero_hist, nb), _history_rows(state_conv[l], nb), cw, lng, lnb,
            wt_p, bt_p, wt_s, bt_s, cfg)
        conv_p.append(_state_rows(cs_p))
        conv_s.append(_state_rows(cs_s))
        v_s.append(v_rows.reshape(dec_batch, dec_seq, dsg))

        x1, eidx, gw, rank, cnt = _merge_call(
            ya, yb, ga, gb, x, w_a_out[l].astype(BF16), w_b_out[l].astype(BF16), w_o[l].astype(BF16),
            ln1_g[l][None, :], ln1_b[l][None, :], rw, rb, alpha, n_experts)

        counts = cnt[:, 0].astype(jnp.int32)
        padded = ((counts + TOK_ROWS - 1) // TOK_ROWS) * TOK_ROWS
        seg_end = jnp.cumsum(padded)
        seg_start = seg_end - padded
        n_used = (seg_end[-1:] // TOK_ROWS).astype(jnp.int32)
        tile_row0 = jnp.arange(n_tiles, dtype=jnp.int32) * TOK_ROWS
        tile_expert = jnp.minimum(
            jnp.sum(tile_row0[:, None] >= seg_end[None, :], axis=1), n_experts - 1).astype(jnp.int32)
        e_ids = jnp.arange(n_experts, dtype=jnp.int32)
        pos = rank[:TOP_K] + jnp.sum(
            jnp.where(eidx[:TOP_K, :, None] == e_ids, seg_start.astype(jnp.int32), 0), axis=-1)

        n_t = (padded // TOK_ROWS).astype(jnp.int32)
        later = jnp.where(n_t > 0, e_ids, n_experts)
        nxt_e = jnp.concatenate([jnp.flip(lax.cummin(jnp.flip(later)))[1:],
                                 jnp.full((1,), n_experts, jnp.int32)])
        group = (jnp.cumsum(n_t > 0) - 1).astype(jnp.int32)
        tile_i = jnp.arange(n_tiles, dtype=jnp.int32)
        q = tile_i - (seg_start // TOK_ROWS).astype(jnp.int32)[tile_expert]
        g_t = jnp.maximum(n_t[tile_expert], 1)
        loads = jnp.logical_and(tile_i < n_used[0], nxt_e[tile_expert] < n_experts)
        t_lo = jnp.where(loads, (W_PARTS * q) // g_t, 0).astype(jnp.int32)
        t_hi = jnp.where(loads, (W_PARTS * (q + 1)) // g_t, 0).astype(jnp.int32)
        sched = (tile_expert, n_used, jnp.minimum(nxt_e[tile_expert], n_experts - 1).astype(jnp.int32),
                 t_lo, t_hi, (group[tile_expert] % 2).astype(jnp.int32))

        tok = jnp.tile(jnp.arange(n, dtype=jnp.int32), TOP_K)
        pos_flat = pos.reshape(-1)
        row_token = jnp.zeros((n_tiles * TOK_ROWS,), jnp.int32).at[pos_flat].set(tok)
        ys = _ffn_call(sched, row_token, x1, w_gate, w_up, w_down, l)
        gwc = jnp.pad(gw[:TOP_K].T, ((0, 0), (0, LANES - TOP_K)))
        x, xb = _norm2_call(pos_flat, x1, gwc, ln2_g[l][None, :], ln2_b[l][None, :], ys, alpha)

    return (x[:n_p].reshape(batch, seq, d), x[n_p:].reshape(dec_batch, dec_seq, d),
            jnp.stack(conv_p), jnp.stack(conv_s), jnp.stack(v_s))
```

```python
import functools
from typing import NamedTuple

import jax
import jax.numpy as jnp
from jax import lax
from jax.experimental import pallas as pl
from jax.experimental.pallas import tpu as pltpu

CHUNK = 64
SGU_CHUNK = 128
SGU_GROUPS = 8
N_EXPERT_GROUPS = 4
TOP_K = 2
LN_EPS = 1e-5
CONV_W = 3

LANES = 128
SUBLANES = 8
COL_BLOCK = 512
ROW_SUB = 256
MIX_ROWS = 1024
TOK_ROWS = 256
W_PARTS = 8
PREFETCH = 2
VMEM_LIMIT = 56 * 1024 * 1024

F32 = jnp.float32
BF16 = jnp.bfloat16


class MixCfg(NamedTuple):
    tm: int
    n_p_tiles: int
    seq: int
    dec_seq: int
    n_s: int
    nb: int
    ns: int
    ng: int


def _ln(z, g, b):
    mu = jnp.mean(z, axis=-1, keepdims=True)
    d = z - mu
    var = jnp.mean(d * d, axis=-1, keepdims=True)
    return d * lax.rsqrt(var + LN_EPS) * g + b


def _conv_rows(hin, p2, p1, cw_ref, jj):
    rows = lax.broadcasted_iota(jnp.int32, hin.shape, 0)
    r1 = pltpu.roll(hin, 1, 0)
    r2 = pltpu.roll(hin, 2, 0)
    sh1 = jnp.where(rows == 0, p1, r1)
    sh2 = jnp.where(rows == 0, p2, jnp.where(rows == 1, p1, r2))
    out = cw_ref[jj, 3:4, :] + cw_ref[jj, 0:1, :] * sh2
    out = out + cw_ref[jj, 1:2, :] * sh1
    return out + cw_ref[jj, 2:3, :] * hin


def _spatial_keep(seq):
    t = lax.broadcasted_iota(jnp.int32, (ROW_SUB, ROW_SUB), 0)
    s = lax.broadcasted_iota(jnp.int32, (ROW_SUB, ROW_SUB), 1)
    same = (t // seq) == (s // seq)
    causal = ((s % seq) // CHUNK) <= ((t % seq) // CHUNK)
    return jnp.logical_and(same, causal)


def _mixer_kernel(x_ref, w_ref, bias_ref, prevp_ref, prevs_ref, cw_ref, lng_ref, lnb_ref,
                  wtp_ref, btp_ref, wts_ref, bts_ref,
                  ya_ref, yb_ref, ga_ref, gb_ref, csp_ref, css_ref, v_ref,
                  bsc, csc, usc, vsc, carry, wm_sc, bt_sc, *, cfg: MixCfg):
    i = pl.program_id(0)
    j = pl.program_id(1)
    nb, ns, ng = cfg.nb, cfg.ns, cfg.ng
    p_c, p_h, p_u, p_v = nb, 2 * nb, 3 * nb, 3 * nb + ns
    p_ga = p_v + ns
    p_gb = p_ga + ng
    cb = w_ref.shape[1]
    gpb = cb // LANES
    is_prompt = i < cfg.n_p_tiles
    is_sample = i == cfg.n_p_tiles

    def sub_rows(r):
        return pl.ds(r * ROW_SUB, ROW_SUB)

    def proj(r):
        rows = sub_rows(r)
        acc = jnp.dot(x_ref[rows, :], w_ref[...], preferred_element_type=F32)
        return rows, acc + bias_ref[...]

    def for_sub(body):
        @pl.when(is_prompt)
        def _():
            for r in range(cfg.tm // ROW_SUB):
                body(r, False)

        @pl.when(is_sample)
        def _():
            for r in range(cfg.n_s // ROW_SUB):
                body(r, True)

    @pl.when(jnp.logical_and(i == 0, j == 0))
    def _():
        carry[...] = prevp_ref[0]

    @pl.when(j < p_c)
    def _():
        def body(r, sample):
            rows, a = proj(r)
            bsc[j, rows, :] = a
        for_sub(body)

    @pl.when(jnp.logical_and(j >= p_c, j < p_h))
    def _():
        def body(r, sample):
            rows, a = proj(r)
            csc[j - p_c, rows, :] = a
        for_sub(body)

    @pl.when(jnp.logical_and(j >= p_h, j < p_u))
    def _():
        jj = j - p_h

        def body(r, sample):
            rows, a = proj(r)
            hin = csc[jj, rows, :] * a
            if not sample:
                out = _conv_rows(hin, carry[jj, 6:7, :], carry[jj, 7:8, :], cw_ref, jj)
                ya_ref[rows, :] = (bsc[jj, rows, :] * out).astype(ya_ref.dtype)
                carry[jj] = hin[ROW_SUB - SUBLANES:, :]
                csp_ref[0, jj] = hin[ROW_SUB - SUBLANES:, :]
            else:
                sr = cfg.dec_seq
                for s in range(ROW_SUB // sr):
                    hs = hin[s * sr:(s + 1) * sr, :]
                    st = r * (ROW_SUB // sr) + s
                    out = _conv_rows(hs, prevs_ref[st, jj, 6:7, :], prevs_ref[st, jj, 7:8, :], cw_ref, jj)
                    srow = pl.ds(r * ROW_SUB + s * sr, sr)
                    ya_ref[srow, :] = (bsc[jj, srow, :] * out).astype(ya_ref.dtype)
                    css_ref[st, jj] = hs[sr - SUBLANES:, :]
        for_sub(body)

    @pl.when(jnp.logical_and(j >= p_u, j < p_v))
    def _():
        def body(r, sample):
            rows, a = proj(r)
            usc[j - p_u, rows, :] = jax.nn.gelu(a)
        for_sub(body)

    @pl.when(jnp.logical_and(j >= p_v, j < p_ga))
    def _():
        def body(r, sample):
            rows, a = proj(r)
            vsc[j - p_v, rows, :] = jax.nn.gelu(a)
        for_sub(body)

    @pl.when(j == p_ga - 1)
    def _():
        @pl.when(is_prompt)
        def _():
            keep = _spatial_keep(min(cfg.seq, SGU_CHUNK))
            for g in range(SGU_GROUPS):
                wm_sc[g] = jnp.where(keep, wtp_ref[g], 0.0).astype(wm_sc.dtype)
            bt_sc[...] = btp_ref[...]

        @pl.when(is_sample)
        def _():
            keep = _spatial_keep(min(cfg.dec_seq, SGU_CHUNK))
            for g in range(SGU_GROUPS):
                wm_sc[g] = jnp.where(keep, wts_ref[g], 0.0).astype(wm_sc.dtype)
            bt_sc[...] = bts_ref[...]

        ds = ns * cb

        def body(r, sample):
            rows = sub_rows(r)
            parts = [vsc[c, rows, :] for c in range(ns)]
            mu = sum(jnp.sum(p, axis=-1, keepdims=True) for p in parts) / ds
            cen = [p - mu for p in parts]
            var = sum(jnp.sum(c * c, axis=-1, keepdims=True) for c in cen) / ds
            inv = lax.rsqrt(var + LN_EPS)
            for c in range(ns):
                cols = slice(c * cb, (c + 1) * cb)
                vn = cen[c] * inv * lng_ref[:, cols] + lnb_ref[:, cols]
                if sample:
                    v_ref[rows, cols] = vn
                vnb = vn.astype(BF16)
                for q in range(gpb):
                    g = c * gpb + q
                    lanes = slice(q * LANES, (q + 1) * LANES)
                    sp = jnp.dot(wm_sc[g], vnb[:, lanes], preferred_element_type=F32) + bt_sc[g]
                    yb_ref[rows, g * LANES:(g + 1) * LANES] = (usc[c, rows, lanes] * sp).astype(yb_ref.dtype)
        for_sub(body)

    @pl.when(jnp.logical_and(j >= p_ga, j < p_gb))
    def _():
        def body(r, sample):
            rows, a = proj(r)
            ga_ref[rows, :] = jax.nn.sigmoid(a)
        for_sub(body)

    @pl.when(j >= p_gb)
    def _():
        def body(r, sample):
            rows, a = proj(r)
            gb_ref[rows, :] = jax.nn.sigmoid(a)
        for_sub(body)


def _mixer_call(xb, w_in, b_in, prev_p, prev_s, cw, lng, lnb, wt_p, bt_p, wt_s, bt_s, cfg):
    n, d = xb.shape
    cb = COL_BLOCK
    nb, ns, ng = cfg.nb, cfg.ns, cfg.ng
    dc, dsg = nb * cb, ns * cb
    n_j = w_in.shape[1] // cb
    n_i = cfg.n_p_tiles + 1
    p_h, p_ga, p_gb = 2 * nb, 3 * nb + 2 * ns, 3 * nb + 2 * ns + ng

    def const(shape):
        return pl.BlockSpec(shape, lambda i, j: (0,) * len(shape))

    in_specs = [
        pl.BlockSpec((cfg.tm, d), lambda i, j: (i, 0)),
        pl.BlockSpec((d, cb), lambda i, j: (0, j)),
        pl.BlockSpec((1, cb), lambda i, j: (0, j)),
        const(prev_p.shape), const(prev_s.shape), const(cw.shape), const(lng.shape), const(lnb.shape),
        const(wt_p.shape), const(bt_p.shape), const(wt_s.shape), const(bt_s.shape),
    ]
    out_shape = [
        jax.ShapeDtypeStruct((n, dc), BF16),
        jax.ShapeDtypeStruct((n, dsg), BF16),
        jax.ShapeDtypeStruct((n, d), F32),
        jax.ShapeDtypeStruct((n, d), F32),
        jax.ShapeDtypeStruct(prev_p.shape, F32),
        jax.ShapeDtypeStruct(prev_s.shape, F32),
        jax.ShapeDtypeStruct((cfg.n_s, dsg), F32),
    ]
    out_specs = [
        pl.BlockSpec((cfg.tm, cb), lambda i, j: (i, jnp.clip(j - p_h, 0, nb - 1))),
        pl.BlockSpec((cfg.tm, dsg), lambda i, j: (i, 0)),
        pl.BlockSpec((cfg.tm, cb), lambda i, j: (i, jnp.clip(j - p_ga, 0, ng - 1))),
        pl.BlockSpec((cfg.tm, cb), lambda i, j: (i, jnp.clip(j - p_gb, 0, ng - 1))),
        const(prev_p.shape), const(prev_s.shape), const((cfg.n_s, dsg)),
    ]
    scratch = [
        pltpu.VMEM((nb, cfg.tm, cb), F32),
        pltpu.VMEM((nb, cfg.tm, cb), F32),
        pltpu.VMEM((ns, cfg.tm, cb), F32),
        pltpu.VMEM((ns, cfg.tm, cb), F32),
        pltpu.VMEM((nb, SUBLANES, cb), F32),
        pltpu.VMEM((SGU_GROUPS, ROW_SUB, ROW_SUB), BF16),
        pltpu.VMEM((SGU_GROUPS, ROW_SUB, LANES), F32),
    ]
    return pl.pallas_call(
        functools.partial(_mixer_kernel, cfg=cfg),
        grid=(n_i, n_j),
        in_specs=in_specs,
        out_specs=out_specs,
        out_shape=out_shape,
        scratch_shapes=scratch,
        compiler_params=pltpu.CompilerParams(
            dimension_semantics=("arbitrary", "arbitrary"), vmem_limit_bytes=VMEM_LIMIT),
        name="mixer",
    )(xb, w_in, b_in, prev_p, prev_s, cw, lng, lnb, wt_p, bt_p, wt_s, bt_s)


def _merge_kernel(ya_ref, yb_ref, ga_ref, gb_ref, xa_ref, xb_ref, wa_ref, wb_ref, wo_ref, g1_ref, b1_ref,
                  rw_ref, rb_ref, x1_ref, eidx_ref, gw_ref, rank_ref, cnt_ref, run, *,
                  alpha, n_experts, n_a_tiles, split):
    i = pl.program_id(0)
    tm = xa_ref.shape[0]
    epg = n_experts // N_EXPERT_GROUPS

    @pl.when(i == 0)
    def _():
        run[...] = jnp.zeros_like(run)

    a = jnp.dot(ya_ref[...], wa_ref[...], preferred_element_type=F32)
    b = jnp.dot(yb_ref[...], wb_ref[...], preferred_element_type=F32)
    merged = ga_ref[...] * a + gb_ref[...] * b
    x = xa_ref[...]
    if split:
        x = jnp.where(i < n_a_tiles, x, xb_ref[...])
    z = alpha * x + jnp.dot(merged.astype(BF16), wo_ref[...], preferred_element_type=F32)
    x1 = _ln(z, g1_ref[...], b1_ref[...])
    x1_ref[...] = x1

    logits = jnp.dot(x1.astype(BF16), rw_ref[...], preferred_element_type=F32)
    lt = logits.T[:n_experts, :]
    ex = jnp.exp(lt - jnp.max(lt, axis=0, keepdims=True))
    sc = ex / jnp.sum(ex, axis=0, keepdims=True)
    sel = sc + rb_ref[...]
    sel_r = [sel[e:e + 1, :] for e in range(n_experts)]
    sc_r = [sc[e:e + 1, :] for e in range(n_experts)]

    def pair_max(v):
        best = None
        for p in range(len(v)):
            for q in range(p + 1, len(v)):
                s = v[p] + v[q]
                best = s if best is None else jnp.maximum(best, s)
        return best

    grp = [pair_max(sel_r[g * epg:(g + 1) * epg]) for g in range(N_EXPERT_GROUPS)]
    gi = jnp.zeros((1, tm), jnp.int32)
    gbest = grp[0]
    for g in range(1, N_EXPERT_GROUPS):
        better = grp[g] > gbest
        gi = jnp.where(better, g, gi)
        gbest = jnp.where(better, grp[g], gbest)

    def pick(rows_, idx):
        out = rows_[-1]
        for c in range(len(rows_) - 2, -1, -1):
            out = jnp.where(idx == c, rows_[c], out)
        return out

    vk = [pick([sel_r[g * epg + q] for g in range(N_EXPERT_GROUPS)], gi) for q in range(epg)]
    pk = [pick([sc_r[g * epg + q] for g in range(N_EXPERT_GROUPS)], gi) for q in range(epg)]

    i1 = jnp.zeros((1, tm), jnp.int32)
    b1 = vk[0]
    for q in range(1, epg):
        better = vk[q] > b1
        i1 = jnp.where(better, q, i1)
        b1 = jnp.where(better, vk[q], b1)
    i2 = jnp.zeros((1, tm), jnp.int32)
    b2 = jnp.full((1, tm), -jnp.inf, F32)
    for q in range(epg):
        cand = jnp.logical_and(i1 != q, vk[q] > b2)
        i2 = jnp.where(cand, q, i2)
        b2 = jnp.where(cand, vk[q], b2)
    p1 = pick(pk, i1)
    p2 = pick(pk, i2)
    den = p1 + p2
    e1 = gi * epg + i1
    e2 = gi * epg + i2

    eio = lax.broadcasted_iota(jnp.int32, (n_experts, tm), 0)
    oh1 = (eio == e1).astype(F32)
    oh2 = (eio == e2).astype(F32)
    ts = lax.broadcasted_iota(jnp.int32, (tm, tm), 0)
    tt = lax.broadcasted_iota(jnp.int32, (tm, tm), 1)
    upper = (ts <= tt).astype(BF16)
    inc1 = jnp.dot(oh1.astype(BF16), upper, preferred_element_type=F32)
    inc2 = jnp.dot(oh2.astype(BF16), upper, preferred_element_type=F32)
    tot1 = jnp.sum(oh1, axis=1, keepdims=True)
    tot2 = jnp.sum(oh2, axis=1, keepdims=True)
    base = run[:, 0:1]
    r1 = jnp.sum(oh1 * (inc1 - 1.0 + base), axis=0, keepdims=True)
    r2 = jnp.sum(oh2 * (inc2 - 1.0 + base + tot1), axis=0, keepdims=True)
    new_run = run[...] + tot1 + tot2
    run[...] = new_run
    cnt_ref[...] = new_run

    eidx_ref[...] = jnp.zeros_like(eidx_ref)
    eidx_ref[0:1, :] = e1
    eidx_ref[1:2, :] = e2
    gw_ref[...] = jnp.zeros_like(gw_ref)
    gw_ref[0:1, :] = p1 / den
    gw_ref[1:2, :] = p2 / den
    rank_ref[...] = jnp.zeros_like(rank_ref)
    rank_ref[0:1, :] = r1.astype(jnp.int32)
    rank_ref[1:2, :] = r2.astype(jnp.int32)


def _merge_call(ya, yb, ga, gb, xa, xb, wa, wb, wo, g1, b1, rw, rb, alpha, n_experts):
    n, d = ya.shape[0], xa.shape[1]
    tm = TOK_ROWS
    dc, dsg = ya.shape[1], yb.shape[1]
    n_a_tiles = xa.shape[0] // tm
    assert xa.shape[0] % tm == 0 and xb.shape[0] == tm and n_a_tiles in (n // tm, n // tm - 1)

    def rows(c):
        return pl.BlockSpec((tm, c), lambda i: (i, 0))

    def const(shape):
        return pl.BlockSpec(shape, lambda i: (0,) * len(shape))

    def lanes(r):
        return pl.BlockSpec((r, tm), lambda i: (0, i))

    return pl.pallas_call(
        functools.partial(_merge_kernel, alpha=alpha, n_experts=n_experts, n_a_tiles=n_a_tiles,
                          split=n_a_tiles < n // tm),
        grid=(n // tm,),
        in_specs=[rows(dc), rows(dsg), rows(d), rows(d),
                  pl.BlockSpec((tm, d), lambda i: (jnp.minimum(i, n_a_tiles - 1), 0)), const((tm, d)),
                  const(wa.shape), const(wb.shape), const(wo.shape), const(g1.shape), const(b1.shape),
                  const(rw.shape), const(rb.shape)],
        out_specs=[rows(d), lanes(SUBLANES), lanes(SUBLANES), lanes(SUBLANES),
                   const((n_experts, LANES))],
        out_shape=[jax.ShapeDtypeStruct((n, d), F32),
                   jax.ShapeDtypeStruct((SUBLANES, n), jnp.int32),
                   jax.ShapeDtypeStruct((SUBLANES, n), F32),
                   jax.ShapeDtypeStruct((SUBLANES, n), jnp.int32),
                   jax.ShapeDtypeStruct((n_experts, LANES), F32)],
        scratch_shapes=[pltpu.VMEM((n_experts, LANES), F32)],
        compiler_params=pltpu.CompilerParams(
            dimension_semantics=("arbitrary",), vmem_limit_bytes=VMEM_LIMIT),
        name="merge_route",
    )(ya, yb, ga, gb, xa, xb, wa, wb, wo, g1, b1, rw, rb)


def _row_gather(src_hbm, dst, sem, index_of, n_rows):
    for r in range(n_rows):
        pltpu.make_async_copy(src_hbm.at[pl.ds(index_of(r), 1)], dst.at[pl.ds(r, 1)], sem).start()


def _row_gather_wait(src_hbm, dst, sem):
    pltpu.make_async_copy(src_hbm.at[pl.ds(0, dst.shape[0])], dst, sem).wait()


def _ffn_kernel(te_ref, nu_ref, ne_ref, t0_ref, t1_ref, ws_ref, tok_ref,
                x1_hbm, wg_hbm, wu_hbm, wd_hbm, ys_ref,
                xbuf, xsem, wgb, wub, wdb, sg, su, sd, wsem, *, layer):
    i = pl.program_id(0)
    n_used = nu_ref[0]
    tm = xbuf.shape[1]
    slot = i % (PREFETCH + 1)

    def part_copies(e, t, s):
        return (pltpu.make_async_copy(wg_hbm.at[layer, e, t], sg.at[s], wsem.at[s]),
                pltpu.make_async_copy(wu_hbm.at[layer, e, t], su.at[s], wsem.at[s]),
                pltpu.make_async_copy(wd_hbm.at[layer, e, t], sd.at[s], wsem.at[s]))

    def start_part(e, t, s):
        for c in part_copies(e, t, s):
            c.start(priority=1)

    def finish_part(e, t, s, w):
        for c in part_copies(e, t, s):
            c.wait()
        rin, rout = sg.shape[1], sd.shape[1]
        wgb[w, pl.ds(pl.multiple_of(t * rin, rin), rin), :] = sg[s].astype(BF16)
        wub[w, pl.ds(pl.multiple_of(t * rin, rin), rin), :] = su[s].astype(BF16)
        wdb[w, pl.ds(pl.multiple_of(t * rout, rout), rout), :] = sd[s].astype(BF16)

    def load_parts(e, t_lo, t_hi, w, base):
        def step(t, c):
            s = base + (t - t_lo) % 2
            finish_part(e, t, s, w)

            @pl.when(t + 2 < t_hi)
            def _():
                start_part(e, t + 2, s)
            return c
        lax.fori_loop(t_lo, t_hi, step, 0)

    def start_first_parts(e, t_lo, t_hi, base):
        @pl.when(t_lo < t_hi)
        def _():
            start_part(e, t_lo, base)

        @pl.when(t_lo + 1 < t_hi)
        def _():
            start_part(e, t_lo + 1, base + 1)

    @pl.when(i == 0)
    def _():
        for a in range(PREFETCH):
            tile = jnp.minimum(a, n_used - 1)

            def first(r, c, a=a, tile=tile):
                pltpu.make_async_copy(x1_hbm.at[pl.ds(tok_ref[tile * tm + r], 1)], xbuf.at[a, pl.ds(r, 1)],
                                      xsem.at[a]).start()
                return c
            lax.fori_loop(0, tm, first, 0)
        start_first_parts(te_ref[0], 0, W_PARTS, 0)
        load_parts(te_ref[0], 0, W_PARTS, 0, 0)
        start_first_parts(ne_ref[0], t0_ref[0], t1_ref[0], 0)

    @pl.when(i < n_used)
    def _():
        w = ws_ref[i]
        e_next, t_lo, t_hi = ne_ref[i], t0_ref[i], t1_ref[i]
        base = 2 * (i % 2)

        @pl.when(i + 1 < n_used)
        def _():
            start_first_parts(ne_ref[i + 1], t0_ref[i + 1], t1_ref[i + 1], 2 - base)
        _row_gather_wait(x1_hbm, xbuf.at[slot], xsem.at[slot])
        x = xbuf[slot].astype(BF16)
        ahead = jnp.minimum(i + PREFETCH, n_used - 1)
        into = (i + PREFETCH) % (PREFETCH + 1)
        _row_gather(x1_hbm, xbuf.at[into], xsem.at[into], lambda r: tok_ref[ahead * tm + r], tm)
        g = jnp.dot(x, wgb[w], preferred_element_type=F32)
        u = jnp.dot(x, wub[w], preferred_element_type=F32)
        h = ((g * jax.nn.sigmoid(g)) * u).astype(BF16)
        ys_ref[...] = jnp.dot(h, wdb[w], preferred_element_type=F32)
        load_parts(e_next, t_lo, t_hi, 1 - w, base)

    @pl.when(i == n_used - 1)
    def _():
        for a in range(1, PREFETCH + 1):
            s = (i + a) % (PREFETCH + 1)
            _row_gather_wait(x1_hbm, xbuf.at[s], xsem.at[s])

    @pl.when(i >= n_used)
    def _():
        ys_ref[...] = jnp.zeros_like(ys_ref)


def _ffn_call(sched, row_token, x1, w_gate, w_up, w_down, layer):
    d = x1.shape[1]
    p = row_token.shape[0]
    depth, n_e, _, f = w_gate.shape
    tm = TOK_ROWS
    wg = w_gate.reshape(depth, n_e, W_PARTS, d // W_PARTS, f)
    wu = w_up.reshape(depth, n_e, W_PARTS, d // W_PARTS, f)
    wd = w_down.reshape(depth, n_e, W_PARTS, f // W_PARTS, d)
    any_spec = pl.BlockSpec(memory_space=pl.ANY)
    return pl.pallas_call(
        functools.partial(_ffn_kernel, layer=layer),
        grid_spec=pltpu.PrefetchScalarGridSpec(
            num_scalar_prefetch=7,
            grid=(p // tm,),
            in_specs=[any_spec, any_spec, any_spec, any_spec],
            out_specs=pl.BlockSpec((tm, d), lambda i, *_: (i, 0)),
            scratch_shapes=[pltpu.VMEM((PREFETCH + 1, tm, d), F32), pltpu.SemaphoreType.DMA((PREFETCH + 1,)),
                            pltpu.VMEM((2, d, f), BF16), pltpu.VMEM((2, d, f), BF16),
                            pltpu.VMEM((2, f, d), BF16),
                            pltpu.VMEM((4, d // W_PARTS, f), F32), pltpu.VMEM((4, d // W_PARTS, f), F32),
                            pltpu.VMEM((4, f // W_PARTS, d), F32), pltpu.SemaphoreType.DMA((4,))],
        ),
        out_shape=jax.ShapeDtypeStruct((p, d), F32),
        compiler_params=pltpu.CompilerParams(
            dimension_semantics=("arbitrary",), vmem_limit_bytes=VMEM_LIMIT),
        name="expert_ffn",
    )(*sched, row_token, x1, wg, wu, wd)


def _norm2_kernel(pos_ref, x1_ref, gw_ref, g_ref, b_ref, ys_hbm, out_a, out_b, ybuf, sem, *, alpha,
                  n_p_tiles):
    i = pl.program_id(0)
    n_steps = pl.num_programs(0)
    tm = x1_ref.shape[0]
    n = tm * n_steps
    slot = i % (PREFETCH + 1)

    @pl.when(i == 0)
    def _():
        for a in range(PREFETCH):
            tile = jnp.minimum(a, n_steps - 1)

            def first(r, c, a=a, tile=tile):
                for k in range(TOP_K):
                    pltpu.make_async_copy(ys_hbm.at[pl.ds(pos_ref[k * n + tile * tm + r], 1)],
                                          ybuf.at[a, k, pl.ds(r, 1)], sem.at[a]).start()
                return c
            lax.fori_loop(0, tm, first, 0)

    for k in range(TOP_K):
        _row_gather_wait(ys_hbm, ybuf.at[slot, k], sem.at[slot])
    ahead = jnp.minimum(i + PREFETCH, n_steps - 1)
    into = (i + PREFETCH) % (PREFETCH + 1)
    for k in range(TOP_K):
        _row_gather(ys_hbm, ybuf.at[into, k], sem.at[into],
                    lambda r, k=k: pos_ref[k * n + ahead * tm + r], tm)
    y = None
    for k in range(TOP_K):
        term = gw_ref[:, k:k + 1] * ybuf[slot, k]
        y = term if y is None else y + term
    x2 = _ln(alpha * x1_ref[...] + y, g_ref[...], b_ref[...])
    if n_p_tiles is None:
        out_a[...] = x2
        out_b[...] = x2.astype(out_b.dtype)
    else:
        @pl.when(i < n_p_tiles)
        def _():
            out_a[...] = x2

        @pl.when(i >= n_p_tiles)
        def _():
            out_b[...] = x2

    @pl.when(i == n_steps - 1)
    def _():
        for a in range(1, PREFETCH + 1):
            s = (i + a) % (PREFETCH + 1)
            for k in range(TOP_K):
                _row_gather_wait(ys_hbm, ybuf.at[s, k], sem.at[s])


def _norm2_call(pos_flat, x1, gwc, g, b, ys, alpha, n_p=None):
    n, d = x1.shape
    tm = TOK_ROWS
    rows = pl.BlockSpec((tm, d), lambda i, pos: (i, 0))
    const = pl.BlockSpec((1, d), lambda i, pos: (0, 0))
    if n_p is None:
        n_p_tiles = None
        out_specs = [rows, rows]
        out_shape = [jax.ShapeDtypeStruct((n, d), F32), jax.ShapeDtypeStruct((n, d), BF16)]
    else:
        assert n - n_p == tm
        n_p_tiles = n_p // tm
        out_specs = [pl.BlockSpec((tm, d), lambda i, pos: (jnp.minimum(i, n_p_tiles - 1), 0)),
                     pl.BlockSpec((tm, d), lambda i, pos: (0, 0))]
        out_shape = [jax.ShapeDtypeStruct((n_p, d), F32), jax.ShapeDtypeStruct((tm, d), F32)]
    return pl.pallas_call(
        functools.partial(_norm2_kernel, alpha=alpha, n_p_tiles=n_p_tiles),
        grid_spec=pltpu.PrefetchScalarGridSpec(
            num_scalar_prefetch=1,
            grid=(n // tm,),
            in_specs=[rows, pl.BlockSpec((tm, LANES), lambda i, pos: (i, 0)), const, const,
                      pl.BlockSpec(memory_space=pl.ANY)],
            out_specs=out_specs,
            scratch_shapes=[pltpu.VMEM((PREFETCH + 1, TOP_K, tm, d), F32),
                            pltpu.SemaphoreType.DMA((PREFETCH + 1,))],
        ),
        out_shape=out_shape,
        compiler_params=pltpu.CompilerParams(
            dimension_semantics=("arbitrary",), vmem_limit_bytes=VMEM_LIMIT),
        name="combine_norm2",
    )(pos_flat, x1, gwc, g, b, ys)


def _spatial_tiles(w_s, b_s, seq):
    reps = ROW_SUB // seq
    wt = jnp.tile(w_s[:, :seq, :seq], (1, reps, reps))
    bt = jnp.tile(b_s[:, :seq], (1, reps))
    return wt, jnp.broadcast_to(bt[:, :, None], bt.shape + (LANES,))


def _history_rows(hist, nb):
    s, _, dc = hist.shape
    h = jnp.pad(hist, ((0, 0), (SUBLANES - (CONV_W - 1), 0), (0, 0)))
    return h.reshape(s, SUBLANES, nb, dc // nb).transpose(0, 2, 1, 3)


def _state_rows(cs):
    s, nb, _, cb = cs.shape
    return cs[:, :, SUBLANES - (CONV_W - 1):, :].transpose(0, 2, 1, 3).reshape(s, CONV_W - 1, nb * cb)


def kernel(x_prompt, x_sample, state_conv, w_in, b_in, conv_w, conv_b, sgu_ln_g, sgu_ln_b, w_spatial,
           b_spatial, w_a_out, w_b_out, w_o, ln1_g, ln1_b, router_w, router_bias, w_gate, w_up, w_down,
           ln2_g, ln2_b):
    depth, d, n_col = w_in.shape
    batch, seq, _ = x_prompt.shape
    dec_batch, dec_seq, _ = x_sample.shape
    dc = conv_w.shape[2]
    dsg = sgu_ln_g.shape[1]
    n_experts = router_w.shape[1]
    assert batch == 1 and n_col == 3 * dc + 2 * dsg + 2 * d
    assert dc % COL_BLOCK == 0 and dsg % COL_BLOCK == 0 and d % COL_BLOCK == 0
    assert dsg == SGU_GROUPS * LANES and n_experts % N_EXPERT_GROUPS == 0
    n_p, n_s = batch * seq, dec_batch * dec_seq
    n = n_p + n_s
    tm_p = min(MIX_ROWS, n_p)
    assert n_p % tm_p == 0 and n_s == ROW_SUB and ROW_SUB % dec_seq == 0 and n_p % TOK_ROWS == 0
    assert seq % SGU_CHUNK == 0 and SGU_CHUNK % dec_seq == 0
    alpha = (2.0 * depth) ** 0.25
    nb, ns, ng = dc // COL_BLOCK, dsg // COL_BLOCK, d // COL_BLOCK
    cfg = MixCfg(tm_p, n_p // tm_p, seq, dec_seq, n_s, nb, ns, ng)

    n_pairs = TOP_K * n
    n_tiles = n_pairs // TOK_ROWS + n_experts
    assert n_pairs % TOK_ROWS == 0

    x, x_tail = x_prompt.reshape(n_p, d), x_sample.reshape(n_s, d)
    xb = jnp.concatenate([x.astype(BF16), x_tail.astype(BF16)], axis=0)
    rw = jnp.pad(router_w, ((0, 0), (0, LANES - n_experts))).astype(BF16)
    rb = jnp.broadcast_to(router_bias[:, None], (n_experts, TOK_ROWS))
    zero_hist = jnp.zeros((batch, CONV_W - 1, dc), F32)

    conv_p, conv_s, v_s = [], [], []
    for l in range(depth):
        w_in_b = w_in[l].astype(BF16)
        bias = b_in[l][None, :]
        cw = jnp.concatenate([conv_w[l], conv_b[l][None, :],
                              jnp.zeros((SUBLANES - CONV_W - 1, dc), F32)], axis=0)
        cw = cw.reshape(SUBLANES, nb, COL_BLOCK).transpose(1, 0, 2)
        lng, lnb = sgu_ln_g[l][None, :], sgu_ln_b[l][None, :]
        wt_p, bt_p = _spatial_tiles(w_spatial[l], b_spatial[l], min(seq, SGU_CHUNK))
        wt_s, bt_s = _spatial_tiles(w_spatial[l], b_spatial[l], min(dec_seq, SGU_CHUNK))

        ya, yb, ga, gb, cs_p, cs_s, v_rows = _mixer_call(
            xb, w_in_b, bias, _history_rows(zero_hist, nb), _history_rows(state_conv[l], nb), cw, lng, lnb,
            wt_p, bt_p, wt_s, bt_s, cfg)
        conv_p.append(_state_rows(cs_p))
        conv_s.append(_state_rows(cs_s))
        v_s.append(v_rows.reshape(dec_batch, dec_seq, dsg))

        x1, eidx, gw, rank, cnt = _merge_call(
            ya, yb, ga, gb, x, x_tail, w_a_out[l].astype(BF16), w_b_out[l].astype(BF16), w_o[l].astype(BF16),
            ln1_g[l][None, :], ln1_b[l][None, :], rw, rb, alpha, n_experts)

        counts = cnt[:, 0].astype(jnp.int32)
        padded = ((counts + TOK_ROWS - 1) // TOK_ROWS) * TOK_ROWS
        seg_end = jnp.cumsum(padded)
        seg_start = seg_end - padded
        n_used = (seg_end[-1:] // TOK_ROWS).astype(jnp.int32)
        tile_row0 = jnp.arange(n_tiles, dtype=jnp.int32) * TOK_ROWS
        tile_expert = jnp.minimum(
            jnp.sum(tile_row0[:, None] >= seg_end[None, :], axis=1), n_experts - 1).astype(jnp.int32)
        e_ids = jnp.arange(n_experts, dtype=jnp.int32)
        pos = rank[:TOP_K] + jnp.sum(
            jnp.where(eidx[:TOP_K, :, None] == e_ids, seg_start.astype(jnp.int32), 0), axis=-1)

        n_t = (padded // TOK_ROWS).astype(jnp.int32)
        later = jnp.where(n_t > 0, e_ids, n_experts)
        nxt_e = jnp.concatenate([jnp.flip(lax.cummin(jnp.flip(later)))[1:],
                                 jnp.full((1,), n_experts, jnp.int32)])
        group = (jnp.cumsum(n_t > 0) - 1).astype(jnp.int32)
        tile_i = jnp.arange(n_tiles, dtype=jnp.int32)
        q = tile_i - (seg_start // TOK_ROWS).astype(jnp.int32)[tile_expert]
        g_t = jnp.maximum(n_t[tile_expert], 1)
        loads = jnp.logical_and(tile_i < n_used[0], nxt_e[tile_expert] < n_experts)
        t_lo = jnp.where(loads, (W_PARTS * q) // g_t, 0).astype(jnp.int32)
        t_hi = jnp.where(loads, (W_PARTS * (q + 1)) // g_t, 0).astype(jnp.int32)
        sched = (tile_expert, n_used, jnp.minimum(nxt_e[tile_expert], n_experts - 1).astype(jnp.int32),
                 t_lo, t_hi, (group[tile_expert] % 2).astype(jnp.int32))

        tok = jnp.tile(jnp.arange(n, dtype=jnp.int32), TOP_K)
        pos_flat = pos.reshape(-1)
        row_token = jnp.zeros((n_tiles * TOK_ROWS,), jnp.int32).at[pos_flat].set(tok)
        ys = _ffn_call(sched, row_token, x1, w_gate, w_up, w_down, l)
        gwc = jnp.pad(gw[:TOP_K].T, ((0, 0), (0, LANES - TOP_K)))
        x, xb = _norm2_call(pos_flat, x1, gwc, ln2_g[l][None, :], ln2_b[l][None, :], ys, alpha,
                            n_p if l == depth - 1 else None)
        x_tail = x[:TOK_ROWS]

    return (x.reshape(batch, seq, d), xb.reshape(dec_batch, dec_seq, d),
            jnp.stack(conv_p), jnp.stack(conv_s), jnp.stack(v_s))
```

```python
import functools
from typing import NamedTuple

import jax
import jax.numpy as jnp
from jax import lax
from jax.experimental import pallas as pl
from jax.experimental.pallas import tpu as pltpu

CHUNK = 64
SGU_CHUNK = 128
SGU_GROUPS = 8
N_EXPERT_GROUPS = 4
TOP_K = 2
LN_EPS = 1e-5
CONV_W = 3

LANES = 128
SUBLANES = 8
COL_BLOCK = 512
ROW_SUB = 256
MIX_ROWS = 1024
TOK_ROWS = 256
W_PARTS = 8
PREFETCH = 2
CAST_BLOCK_BYTES = 10 * 1024 * 1024
VMEM_LIMIT = 56 * 1024 * 1024

F32 = jnp.float32
BF16 = jnp.bfloat16


class MixCfg(NamedTuple):
    tm: int
    n_p_tiles: int
    seq: int
    dec_seq: int
    n_s: int
    nb: int
    ns: int
    ng: int


def _ln(z, g, b):
    mu = jnp.mean(z, axis=-1, keepdims=True)
    d = z - mu
    var = jnp.mean(d * d, axis=-1, keepdims=True)
    return d * lax.rsqrt(var + LN_EPS) * g + b


def _conv_rows(hin, p2, p1, cw_ref, jj):
    rows = lax.broadcasted_iota(jnp.int32, hin.shape, 0)
    r1 = pltpu.roll(hin, 1, 0)
    r2 = pltpu.roll(hin, 2, 0)
    sh1 = jnp.where(rows == 0, p1, r1)
    sh2 = jnp.where(rows == 0, p2, jnp.where(rows == 1, p1, r2))
    out = cw_ref[jj, 3:4, :] + cw_ref[jj, 0:1, :] * sh2
    out = out + cw_ref[jj, 1:2, :] * sh1
    return out + cw_ref[jj, 2:3, :] * hin


def _spatial_keep(seq):
    t = lax.broadcasted_iota(jnp.int32, (ROW_SUB, ROW_SUB), 0)
    s = lax.broadcasted_iota(jnp.int32, (ROW_SUB, ROW_SUB), 1)
    same = (t // seq) == (s // seq)
    causal = ((s % seq) // CHUNK) <= ((t % seq) // CHUNK)
    return jnp.logical_and(same, causal)


def _mixer_kernel(x_ref, w_ref, bias_ref, prevp_ref, prevs_ref, cw_ref, lng_ref, lnb_ref,
                  wtp_ref, btp_ref, wts_ref, bts_ref,
                  ya_ref, yb_ref, ga_ref, gb_ref, csp_ref, css_ref, v_ref,
                  bsc, csc, usc, vsc, carry, wm_sc, bt_sc, *, cfg: MixCfg):
    i = pl.program_id(0)
    j = pl.program_id(1)
    nb, ns, ng = cfg.nb, cfg.ns, cfg.ng
    p_c, p_h, p_u, p_v = nb, 2 * nb, 3 * nb, 3 * nb + ns
    p_ga = p_v + ns
    p_gb = p_ga + ng
    cb = w_ref.shape[2]
    gpb = cb // LANES
    is_prompt = i < cfg.n_p_tiles
    is_sample = i == cfg.n_p_tiles

    def sub_rows(r):
        return pl.ds(r * ROW_SUB, ROW_SUB)

    def proj(r):
        rows = sub_rows(r)
        acc = jnp.dot(x_ref[rows, :], w_ref[0], preferred_element_type=F32)
        return rows, acc + bias_ref[...]

    def for_sub(body):
        @pl.when(is_prompt)
        def _():
            for r in range(cfg.tm // ROW_SUB):
                body(r, False)

        @pl.when(is_sample)
        def _():
            for r in range(cfg.n_s // ROW_SUB):
                body(r, True)

    @pl.when(jnp.logical_and(i == 0, j == 0))
    def _():
        carry[...] = prevp_ref[0]

    @pl.when(j < p_c)
    def _():
        def body(r, sample):
            rows, a = proj(r)
            bsc[j, rows, :] = a
        for_sub(body)

    @pl.when(jnp.logical_and(j >= p_c, j < p_h))
    def _():
        def body(r, sample):
            rows, a = proj(r)
            csc[j - p_c, rows, :] = a
        for_sub(body)

    @pl.when(jnp.logical_and(j >= p_h, j < p_u))
    def _():
        jj = j - p_h

        def body(r, sample):
            rows, a = proj(r)
            hin = csc[jj, rows, :] * a
            if not sample:
                out = _conv_rows(hin, carry[jj, 6:7, :], carry[jj, 7:8, :], cw_ref, jj)
                ya_ref[rows, :] = (bsc[jj, rows, :] * out).astype(ya_ref.dtype)
                carry[jj] = hin[ROW_SUB - SUBLANES:, :]
                csp_ref[0, jj] = hin[ROW_SUB - SUBLANES:, :]
            else:
                sr = cfg.dec_seq
                for s in range(ROW_SUB // sr):
                    hs = hin[s * sr:(s + 1) * sr, :]
                    st = r * (ROW_SUB // sr) + s
                    out = _conv_rows(hs, prevs_ref[st, jj, 6:7, :], prevs_ref[st, jj, 7:8, :], cw_ref, jj)
                    srow = pl.ds(r * ROW_SUB + s * sr, sr)
                    ya_ref[srow, :] = (bsc[jj, srow, :] * out).astype(ya_ref.dtype)
                    css_ref[st, jj] = hs[sr - SUBLANES:, :]
        for_sub(body)

    @pl.when(jnp.logical_and(j >= p_u, j < p_v))
    def _():
        def body(r, sample):
            rows, a = proj(r)
            usc[j - p_u, rows, :] = jax.nn.gelu(a)
        for_sub(body)

    @pl.when(jnp.logical_and(j >= p_v, j < p_ga))
    def _():
        def body(r, sample):
            rows, a = proj(r)
            vsc[j - p_v, rows, :] = jax.nn.gelu(a)
        for_sub(body)

    @pl.when(j == p_ga - 1)
    def _():
        @pl.when(is_prompt)
        def _():
            keep = _spatial_keep(min(cfg.seq, SGU_CHUNK))
            for g in range(SGU_GROUPS):
                wm_sc[g] = jnp.where(keep, wtp_ref[g], 0.0).astype(wm_sc.dtype)
            bt_sc[...] = btp_ref[...]

        @pl.when(is_sample)
        def _():
            keep = _spatial_keep(min(cfg.dec_seq, SGU_CHUNK))
            for g in range(SGU_GROUPS):
                wm_sc[g] = jnp.where(keep, wts_ref[g], 0.0).astype(wm_sc.dtype)
            bt_sc[...] = bts_ref[...]

        ds = ns * cb

        def body(r, sample):
            rows = sub_rows(r)
            parts = [vsc[c, rows, :] for c in range(ns)]
            mu = sum(jnp.sum(p, axis=-1, keepdims=True) for p in parts) / ds
            cen = [p - mu for p in parts]
            var = sum(jnp.sum(c * c, axis=-1, keepdims=True) for c in cen) / ds
            inv = lax.rsqrt(var + LN_EPS)
            for c in range(ns):
                cols = slice(c * cb, (c + 1) * cb)
                vn = cen[c] * inv * lng_ref[:, cols] + lnb_ref[:, cols]
                if sample:
                    v_ref[rows, cols] = vn
                vnb = vn.astype(BF16)
                for q in range(gpb):
                    g = c * gpb + q
                    lanes = slice(q * LANES, (q + 1) * LANES)
                    sp = jnp.dot(wm_sc[g], vnb[:, lanes], preferred_element_type=F32) + bt_sc[g]
                    yb_ref[rows, g * LANES:(g + 1) * LANES] = (usc[c, rows, lanes] * sp).astype(yb_ref.dtype)
        for_sub(body)

    @pl.when(jnp.logical_and(j >= p_ga, j < p_gb))
    def _():
        def body(r, sample):
            rows, a = proj(r)
            ga_ref[rows, :] = jax.nn.sigmoid(a)
        for_sub(body)

    @pl.when(j >= p_gb)
    def _():
        def body(r, sample):
            rows, a = proj(r)
            gb_ref[rows, :] = jax.nn.sigmoid(a)
        for_sub(body)


def _mixer_call(xb, w_in, layer, b_in, prev_p, prev_s, cw, lng, lnb, wt_p, bt_p, wt_s, bt_s, cfg):
    n, d = xb.shape
    cb = COL_BLOCK
    nb, ns, ng = cfg.nb, cfg.ns, cfg.ng
    dc, dsg = nb * cb, ns * cb
    n_j = w_in.shape[2] // cb
    n_i = cfg.n_p_tiles + 1
    p_h, p_ga, p_gb = 2 * nb, 3 * nb + 2 * ns, 3 * nb + 2 * ns + ng

    def const(shape):
        return pl.BlockSpec(shape, lambda i, j: (0,) * len(shape))

    in_specs = [
        pl.BlockSpec((cfg.tm, d), lambda i, j: (i, 0)),
        pl.BlockSpec((1, d, cb), lambda i, j: (layer, 0, j)),
        pl.BlockSpec((1, cb), lambda i, j: (0, j)),
        const(prev_p.shape), const(prev_s.shape), const(cw.shape), const(lng.shape), const(lnb.shape),
        const(wt_p.shape), const(bt_p.shape), const(wt_s.shape), const(bt_s.shape),
    ]
    out_shape = [
        jax.ShapeDtypeStruct((n, dc), BF16),
        jax.ShapeDtypeStruct((n, dsg), BF16),
        jax.ShapeDtypeStruct((n, d), F32),
        jax.ShapeDtypeStruct((n, d), F32),
        jax.ShapeDtypeStruct(prev_p.shape, F32),
        jax.ShapeDtypeStruct(prev_s.shape, F32),
        jax.ShapeDtypeStruct((cfg.n_s, dsg), F32),
    ]
    out_specs = [
        pl.BlockSpec((cfg.tm, cb), lambda i, j: (i, jnp.clip(j - p_h, 0, nb - 1))),
        pl.BlockSpec((cfg.tm, dsg), lambda i, j: (i, 0)),
        pl.BlockSpec((cfg.tm, cb), lambda i, j: (i, jnp.clip(j - p_ga, 0, ng - 1))),
        pl.BlockSpec((cfg.tm, cb), lambda i, j: (i, jnp.clip(j - p_gb, 0, ng - 1))),
        const(prev_p.shape), const(prev_s.shape), const((cfg.n_s, dsg)),
    ]
    scratch = [
        pltpu.VMEM((nb, cfg.tm, cb), F32),
        pltpu.VMEM((nb, cfg.tm, cb), F32),
        pltpu.VMEM((ns, cfg.tm, cb), F32),
        pltpu.VMEM((ns, cfg.tm, cb), F32),
        pltpu.VMEM((nb, SUBLANES, cb), F32),
        pltpu.VMEM((SGU_GROUPS, ROW_SUB, ROW_SUB), BF16),
        pltpu.VMEM((SGU_GROUPS, ROW_SUB, LANES), F32),
    ]
    return pl.pallas_call(
        functools.partial(_mixer_kernel, cfg=cfg),
        grid=(n_i, n_j),
        in_specs=in_specs,
        out_specs=out_specs,
        out_shape=out_shape,
        scratch_shapes=scratch,
        compiler_params=pltpu.CompilerParams(
            dimension_semantics=("arbitrary", "arbitrary"), vmem_limit_bytes=VMEM_LIMIT),
        name="mixer",
    )(xb, w_in, b_in, prev_p, prev_s, cw, lng, lnb, wt_p, bt_p, wt_s, bt_s)


def _merge_kernel(ya_ref, yb_ref, ga_ref, gb_ref, xa_ref, xb_ref, wa_ref, wb_ref, wo_ref, g1_ref, b1_ref,
                  rw_ref, rb_ref, x1_ref, eidx_ref, gw_ref, rank_ref, cnt_ref, run, *,
                  alpha, n_experts, n_a_tiles, split):
    i = pl.program_id(0)
    tm = xa_ref.shape[0]
    epg = n_experts // N_EXPERT_GROUPS

    @pl.when(i == 0)
    def _():
        run[...] = jnp.zeros_like(run)

    a = jnp.dot(ya_ref[...], wa_ref[0], preferred_element_type=F32)
    b = jnp.dot(yb_ref[...], wb_ref[0], preferred_element_type=F32)
    merged = ga_ref[...] * a + gb_ref[...] * b
    x = xa_ref[...]
    if split:
        x = jnp.where(i < n_a_tiles, x, xb_ref[...])
    z = alpha * x + jnp.dot(merged.astype(BF16), wo_ref[0], preferred_element_type=F32)
    x1 = _ln(z, g1_ref[...], b1_ref[...])
    x1_ref[...] = x1

    logits = jnp.dot(x1.astype(BF16), rw_ref[...], preferred_element_type=F32)
    lt = logits.T[:n_experts, :]
    ex = jnp.exp(lt - jnp.max(lt, axis=0, keepdims=True))
    sc = ex / jnp.sum(ex, axis=0, keepdims=True)
    sel = sc + rb_ref[...]
    sel_r = [sel[e:e + 1, :] for e in range(n_experts)]
    sc_r = [sc[e:e + 1, :] for e in range(n_experts)]

    def pair_max(v):
        best = None
        for p in range(len(v)):
            for q in range(p + 1, len(v)):
                s = v[p] + v[q]
                best = s if best is None else jnp.maximum(best, s)
        return best

    grp = [pair_max(sel_r[g * epg:(g + 1) * epg]) for g in range(N_EXPERT_GROUPS)]
    gi = jnp.zeros((1, tm), jnp.int32)
    gbest = grp[0]
    for g in range(1, N_EXPERT_GROUPS):
        better = grp[g] > gbest
        gi = jnp.where(better, g, gi)
        gbest = jnp.where(better, grp[g], gbest)

    def pick(rows_, idx):
        out = rows_[-1]
        for c in range(len(rows_) - 2, -1, -1):
            out = jnp.where(idx == c, rows_[c], out)
        return out

    vk = [pick([sel_r[g * epg + q] for g in range(N_EXPERT_GROUPS)], gi) for q in range(epg)]
    pk = [pick([sc_r[g * epg + q] for g in range(N_EXPERT_GROUPS)], gi) for q in range(epg)]

    i1 = jnp.zeros((1, tm), jnp.int32)
    b1 = vk[0]
    for q in range(1, epg):
        better = vk[q] > b1
        i1 = jnp.where(better, q, i1)
        b1 = jnp.where(better, vk[q], b1)
    i2 = jnp.zeros((1, tm), jnp.int32)
    b2 = jnp.full((1, tm), -jnp.inf, F32)
    for q in range(epg):
        cand = jnp.logical_and(i1 != q, vk[q] > b2)
        i2 = jnp.where(cand, q, i2)
        b2 = jnp.where(cand, vk[q], b2)
    p1 = pick(pk, i1)
    p2 = pick(pk, i2)
    den = p1 + p2
    e1 = gi * epg + i1
    e2 = gi * epg + i2

    eio = lax.broadcasted_iota(jnp.int32, (n_experts, tm), 0)
    oh1 = (eio == e1).astype(F32)
    oh2 = (eio == e2).astype(F32)
    ts = lax.broadcasted_iota(jnp.int32, (tm, tm), 0)
    tt = lax.broadcasted_iota(jnp.int32, (tm, tm), 1)
    upper = (ts <= tt).astype(BF16)
    inc1 = jnp.dot(oh1.astype(BF16), upper, preferred_element_type=F32)
    inc2 = jnp.dot(oh2.astype(BF16), upper, preferred_element_type=F32)
    tot1 = jnp.sum(oh1, axis=1, keepdims=True)
    tot2 = jnp.sum(oh2, axis=1, keepdims=True)
    base = run[:, 0:1]
    r1 = jnp.sum(oh1 * (inc1 - 1.0 + base), axis=0, keepdims=True)
    r2 = jnp.sum(oh2 * (inc2 - 1.0 + base + tot1), axis=0, keepdims=True)
    new_run = run[...] + tot1 + tot2
    run[...] = new_run
    cnt_ref[...] = new_run

    eidx_ref[...] = jnp.zeros_like(eidx_ref)
    eidx_ref[0:1, :] = e1
    eidx_ref[1:2, :] = e2
    gw_ref[...] = jnp.zeros_like(gw_ref)
    gw_ref[0:1, :] = p1 / den
    gw_ref[1:2, :] = p2 / den
    rank_ref[...] = jnp.zeros_like(rank_ref)
    rank_ref[0:1, :] = r1.astype(jnp.int32)
    rank_ref[1:2, :] = r2.astype(jnp.int32)


def _merge_call(ya, yb, ga, gb, xa, xb, wa, wb, wo, layer, g1, b1, rw, rb, alpha, n_experts):
    n, d = ya.shape[0], xa.shape[1]
    tm = TOK_ROWS
    dc, dsg = ya.shape[1], yb.shape[1]
    n_a_tiles = xa.shape[0] // tm
    assert xa.shape[0] % tm == 0 and xb.shape[0] == tm and n_a_tiles in (n // tm, n // tm - 1)

    def rows(c):
        return pl.BlockSpec((tm, c), lambda i: (i, 0))

    def const(shape):
        return pl.BlockSpec(shape, lambda i: (0,) * len(shape))

    def lanes(r):
        return pl.BlockSpec((r, tm), lambda i: (0, i))

    def layer_block(w):
        return pl.BlockSpec((1,) + w.shape[1:], lambda i: (layer, 0, 0))

    return pl.pallas_call(
        functools.partial(_merge_kernel, alpha=alpha, n_experts=n_experts, n_a_tiles=n_a_tiles,
                          split=n_a_tiles < n // tm),
        grid=(n // tm,),
        in_specs=[rows(dc), rows(dsg), rows(d), rows(d),
                  pl.BlockSpec((tm, d), lambda i: (jnp.minimum(i, n_a_tiles - 1), 0)), const((tm, d)),
                  layer_block(wa), layer_block(wb), layer_block(wo), const(g1.shape), const(b1.shape),
                  const(rw.shape), const(rb.shape)],
        out_specs=[rows(d), lanes(SUBLANES), lanes(SUBLANES), lanes(SUBLANES),
                   const((n_experts, LANES))],
        out_shape=[jax.ShapeDtypeStruct((n, d), F32),
                   jax.ShapeDtypeStruct((SUBLANES, n), jnp.int32),
                   jax.ShapeDtypeStruct((SUBLANES, n), F32),
                   jax.ShapeDtypeStruct((SUBLANES, n), jnp.int32),
                   jax.ShapeDtypeStruct((n_experts, LANES), F32)],
        scratch_shapes=[pltpu.VMEM((n_experts, LANES), F32)],
        compiler_params=pltpu.CompilerParams(
            dimension_semantics=("arbitrary",), vmem_limit_bytes=VMEM_LIMIT),
        name="merge_route",
    )(ya, yb, ga, gb, xa, xb, wa, wb, wo, g1, b1, rw, rb)


def _row_gather(src_hbm, dst, sem, index_of, n_rows):
    for r in range(n_rows):
        pltpu.make_async_copy(src_hbm.at[pl.ds(index_of(r), 1)], dst.at[pl.ds(r, 1)], sem).start()


def _row_gather_wait(src_hbm, dst, sem):
    pltpu.make_async_copy(src_hbm.at[pl.ds(0, dst.shape[0])], dst, sem).wait()


def _ffn_kernel(te_ref, nu_ref, ne_ref, t0_ref, t1_ref, ws_ref, tok_ref,
                x1_hbm, wg_hbm, wu_hbm, wd_hbm, ys_ref,
                xbuf, xsem, wgb, wub, wdb, sg, su, sd, wsem, *, layer):
    i = pl.program_id(0)
    n_used = nu_ref[0]
    tm = xbuf.shape[1]
    slot = i % (PREFETCH + 1)

    def part_copies(e, t, s):
        return (pltpu.make_async_copy(wg_hbm.at[layer, e, t], sg.at[s], wsem.at[s]),
                pltpu.make_async_copy(wu_hbm.at[layer, e, t], su.at[s], wsem.at[s]),
                pltpu.make_async_copy(wd_hbm.at[layer, e, t], sd.at[s], wsem.at[s]))

    def start_part(e, t, s):
        for c in part_copies(e, t, s):
            c.start(priority=1)

    def finish_part(e, t, s, w):
        for c in part_copies(e, t, s):
            c.wait()
        rin, rout = sg.shape[1], sd.shape[1]
        wgb[w, pl.ds(pl.multiple_of(t * rin, rin), rin), :] = sg[s].astype(BF16)
        wub[w, pl.ds(pl.multiple_of(t * rin, rin), rin), :] = su[s].astype(BF16)
        wdb[w, pl.ds(pl.multiple_of(t * rout, rout), rout), :] = sd[s].astype(BF16)

    def load_parts(e, t_lo, t_hi, w, base):
        def step(t, c):
            s = base + (t - t_lo) % 2
            finish_part(e, t, s, w)

            @pl.when(t + 2 < t_hi)
            def _():
                start_part(e, t + 2, s)
            return c
        lax.fori_loop(t_lo, t_hi, step, 0)

    def start_first_parts(e, t_lo, t_hi, base):
        @pl.when(t_lo < t_hi)
        def _():
            start_part(e, t_lo, base)

        @pl.when(t_lo + 1 < t_hi)
        def _():
            start_part(e, t_lo + 1, base + 1)

    @pl.when(i == 0)
    def _():
        for a in range(PREFETCH):
            tile = jnp.minimum(a, n_used - 1)

            def first(r, c, a=a, tile=tile):
                pltpu.make_async_copy(x1_hbm.at[pl.ds(tok_ref[tile * tm + r], 1)], xbuf.at[a, pl.ds(r, 1)],
                                      xsem.at[a]).start()
                return c
            lax.fori_loop(0, tm, first, 0)
        start_first_parts(te_ref[0], 0, W_PARTS, 0)
        load_parts(te_ref[0], 0, W_PARTS, 0, 0)

    @pl.when(i < n_used)
    def _():
        w = ws_ref[i]
        e_next, t_lo, t_hi = ne_ref[i], t0_ref[i], t1_ref[i]
        base = 0
        start_first_parts(e_next, t_lo, t_hi, base)
        _row_gather_wait(x1_hbm, xbuf.at[slot], xsem.at[slot])
        x = xbuf[slot].astype(BF16)
        ahead = jnp.minimum(i + PREFETCH, n_used - 1)
        into = (i + PREFETCH) % (PREFETCH + 1)
        _row_gather(x1_hbm, xbuf.at[into], xsem.at[into], lambda r: tok_ref[ahead * tm + r], tm)
        g = jnp.dot(x, wgb[w], preferred_element_type=F32)
        u = jnp.dot(x, wub[w], preferred_element_type=F32)
        h = ((g * jax.nn.sigmoid(g)) * u).astype(BF16)
        ys_ref[...] = jnp.dot(h, wdb[w], preferred_element_type=F32)
        load_parts(e_next, t_lo, t_hi, 1 - w, base)

    @pl.when(i == n_used - 1)
    def _():
        for a in range(1, PREFETCH + 1):
            s = (i + a) % (PREFETCH + 1)
            _row_gather_wait(x1_hbm, xbuf.at[s], xsem.at[s])

    @pl.when(i >= n_used)
    def _():
        ys_ref[...] = jnp.zeros_like(ys_ref)


def _ffn_call(sched, row_token, x1, w_gate, w_up, w_down, layer):
    d = x1.shape[1]
    p = row_token.shape[0]
    depth, n_e, _, f = w_gate.shape
    tm = TOK_ROWS
    wg = w_gate.reshape(depth, n_e, W_PARTS, d // W_PARTS, f)
    wu = w_up.reshape(depth, n_e, W_PARTS, d // W_PARTS, f)
    wd = w_down.reshape(depth, n_e, W_PARTS, f // W_PARTS, d)
    any_spec = pl.BlockSpec(memory_space=pl.ANY)
    return pl.pallas_call(
        functools.partial(_ffn_kernel, layer=layer),
        grid_spec=pltpu.PrefetchScalarGridSpec(
            num_scalar_prefetch=7,
            grid=(p // tm,),
            in_specs=[any_spec, any_spec, any_spec, any_spec],
            out_specs=pl.BlockSpec((tm, d), lambda i, *_: (i, 0)),
            scratch_shapes=[pltpu.VMEM((PREFETCH + 1, tm, d), F32), pltpu.SemaphoreType.DMA((PREFETCH + 1,)),
                            pltpu.VMEM((2, d, f), BF16), pltpu.VMEM((2, d, f), BF16),
                            pltpu.VMEM((2, f, d), BF16),
                            pltpu.VMEM((2, d // W_PARTS, f), F32), pltpu.VMEM((2, d // W_PARTS, f), F32),
                            pltpu.VMEM((2, f // W_PARTS, d), F32), pltpu.SemaphoreType.DMA((2,))],
        ),
        out_shape=jax.ShapeDtypeStruct((p, d), F32),
        compiler_params=pltpu.CompilerParams(
            dimension_semantics=("arbitrary",), vmem_limit_bytes=VMEM_LIMIT),
        name="expert_ffn",
    )(*sched, row_token, x1, wg, wu, wd)


def _norm2_kernel(pos_ref, x1_ref, gw_ref, g_ref, b_ref, ys_hbm, out_a, out_b, ybuf, sem, *, alpha,
                  n_p_tiles):
    i = pl.program_id(0)
    n_steps = pl.num_programs(0)
    tm = x1_ref.shape[0]
    n = tm * n_steps
    slot = i % (PREFETCH + 1)

    @pl.when(i == 0)
    def _():
        for a in range(PREFETCH):
            tile = jnp.minimum(a, n_steps - 1)

            def first(r, c, a=a, tile=tile):
                for k in range(TOP_K):
                    pltpu.make_async_copy(ys_hbm.at[pl.ds(pos_ref[k * n + tile * tm + r], 1)],
                                          ybuf.at[a, k, pl.ds(r, 1)], sem.at[a]).start()
                return c
            lax.fori_loop(0, tm, first, 0)

    for k in range(TOP_K):
        _row_gather_wait(ys_hbm, ybuf.at[slot, k], sem.at[slot])
    ahead = jnp.minimum(i + PREFETCH, n_steps - 1)
    into = (i + PREFETCH) % (PREFETCH + 1)
    for k in range(TOP_K):
        _row_gather(ys_hbm, ybuf.at[into, k], sem.at[into],
                    lambda r, k=k: pos_ref[k * n + ahead * tm + r], tm)
    y = None
    for k in range(TOP_K):
        term = gw_ref[:, k:k + 1] * ybuf[slot, k]
        y = term if y is None else y + term
    x2 = _ln(alpha * x1_ref[...] + y, g_ref[...], b_ref[...])
    if n_p_tiles is None:
        out_a[...] = x2
        out_b[...] = x2.astype(out_b.dtype)
    else:
        @pl.when(i < n_p_tiles)
        def _():
            out_a[...] = x2

        @pl.when(i >= n_p_tiles)
        def _():
            out_b[...] = x2

    @pl.when(i == n_steps - 1)
    def _():
        for a in range(1, PREFETCH + 1):
            s = (i + a) % (PREFETCH + 1)
            for k in range(TOP_K):
                _row_gather_wait(ys_hbm, ybuf.at[s, k], sem.at[s])


def _norm2_call(pos_flat, x1, gwc, g, b, ys, alpha, n_p=None):
    n, d = x1.shape
    tm = TOK_ROWS
    rows = pl.BlockSpec((tm, d), lambda i, pos: (i, 0))
    const = pl.BlockSpec((1, d), lambda i, pos: (0, 0))
    if n_p is None:
        n_p_tiles = None
        out_specs = [rows, rows]
        out_shape = [jax.ShapeDtypeStruct((n, d), F32), jax.ShapeDtypeStruct((n, d), BF16)]
    else:
        assert n - n_p == tm
        n_p_tiles = n_p // tm
        out_specs = [pl.BlockSpec((tm, d), lambda i, pos: (jnp.minimum(i, n_p_tiles - 1), 0)),
                     pl.BlockSpec((tm, d), lambda i, pos: (0, 0))]
        out_shape = [jax.ShapeDtypeStruct((n_p, d), F32), jax.ShapeDtypeStruct((tm, d), F32)]
    return pl.pallas_call(
        functools.partial(_norm2_kernel, alpha=alpha, n_p_tiles=n_p_tiles),
        grid_spec=pltpu.PrefetchScalarGridSpec(
            num_scalar_prefetch=1,
            grid=(n // tm,),
            in_specs=[rows, pl.BlockSpec((tm, LANES), lambda i, pos: (i, 0)), const, const,
                      pl.BlockSpec(memory_space=pl.ANY)],
            out_specs=out_specs,
            scratch_shapes=[pltpu.VMEM((PREFETCH + 1, TOP_K, tm, d), F32),
                            pltpu.SemaphoreType.DMA((PREFETCH + 1,))],
        ),
        out_shape=out_shape,
        compiler_params=pltpu.CompilerParams(
            dimension_semantics=("arbitrary",), vmem_limit_bytes=VMEM_LIMIT),
        name="combine_norm2",
    )(pos_flat, x1, gwc, g, b, ys)


def _cast_kernel(x_ref, o_ref):
    o_ref[...] = x_ref[...].astype(o_ref.dtype)


def _cast_bf16(w):
    cols = w.shape[-1]
    rows = w.size // cols
    br = rows
    while br * cols * 4 > CAST_BLOCK_BYTES and br % 32 == 0:
        br //= 2
    block = pl.BlockSpec((br, cols), lambda i: (i, 0))
    out = pl.pallas_call(
        _cast_kernel,
        grid=(rows // br,),
        in_specs=[block],
        out_specs=block,
        out_shape=jax.ShapeDtypeStruct((rows, cols), BF16),
        compiler_params=pltpu.CompilerParams(
            dimension_semantics=("arbitrary",), vmem_limit_bytes=VMEM_LIMIT),
        name="cast_weights",
    )(w.reshape(rows, cols))
    return out.reshape(w.shape)


def _spatial_tiles(w_s, b_s, seq):
    reps = ROW_SUB // seq
    wt = jnp.tile(w_s[:, :seq, :seq], (1, reps, reps))
    bt = jnp.tile(b_s[:, :seq], (1, reps))
    return wt, jnp.broadcast_to(bt[:, :, None], bt.shape + (LANES,))


def _history_rows(hist, nb):
    s, _, dc = hist.shape
    h = jnp.pad(hist, ((0, 0), (SUBLANES - (CONV_W - 1), 0), (0, 0)))
    return h.reshape(s, SUBLANES, nb, dc // nb).transpose(0, 2, 1, 3)


def _state_rows(cs):
    s, nb, _, cb = cs.shape
    return cs[:, :, SUBLANES - (CONV_W - 1):, :].transpose(0, 2, 1, 3).reshape(s, CONV_W - 1, nb * cb)


def kernel(x_prompt, x_sample, state_conv, w_in, b_in, conv_w, conv_b, sgu_ln_g, sgu_ln_b, w_spatial,
           b_spatial, w_a_out, w_b_out, w_o, ln1_g, ln1_b, router_w, router_bias, w_gate, w_up, w_down,
           ln2_g, ln2_b):
    depth, d, n_col = w_in.shape
    batch, seq, _ = x_prompt.shape
    dec_batch, dec_seq, _ = x_sample.shape
    dc = conv_w.shape[2]
    dsg = sgu_ln_g.shape[1]
    n_experts = router_w.shape[1]
    assert batch == 1 and n_col == 3 * dc + 2 * dsg + 2 * d
    assert dc % COL_BLOCK == 0 and dsg % COL_BLOCK == 0 and d % COL_BLOCK == 0
    assert dsg == SGU_GROUPS * LANES and n_experts % N_EXPERT_GROUPS == 0
    n_p, n_s = batch * seq, dec_batch * dec_seq
    n = n_p + n_s
    tm_p = min(MIX_ROWS, n_p)
    assert n_p % tm_p == 0 and n_s == ROW_SUB and ROW_SUB % dec_seq == 0 and n_p % TOK_ROWS == 0
    assert seq % SGU_CHUNK == 0 and SGU_CHUNK % dec_seq == 0
    alpha = (2.0 * depth) ** 0.25
    nb, ns, ng = dc // COL_BLOCK, dsg // COL_BLOCK, d // COL_BLOCK
    cfg = MixCfg(tm_p, n_p // tm_p, seq, dec_seq, n_s, nb, ns, ng)

    n_pairs = TOP_K * n
    n_tiles = n_pairs // TOK_ROWS + n_experts
    assert n_pairs % TOK_ROWS == 0

    x, x_tail = x_prompt.reshape(n_p, d), x_sample.reshape(n_s, d)
    xb = jnp.concatenate([x.astype(BF16), x_tail.astype(BF16)], axis=0)
    rw = jnp.pad(router_w, ((0, 0), (0, LANES - n_experts))).astype(BF16)
    rb = jnp.broadcast_to(router_bias[:, None], (n_experts, TOK_ROWS))
    zero_hist = jnp.zeros((batch, CONV_W - 1, dc), F32)

    w_in_b, w_a_b, w_b_b, w_o_b = (_cast_bf16(w) for w in (w_in, w_a_out, w_b_out, w_o))

    conv_p, conv_s, v_s = [], [], []
    for l in range(depth):
        bias = b_in[l][None, :]
        cw = jnp.concatenate([conv_w[l], conv_b[l][None, :],
                              jnp.zeros((SUBLANES - CONV_W - 1, dc), F32)], axis=0)
        cw = cw.reshape(SUBLANES, nb, COL_BLOCK).transpose(1, 0, 2)
        lng, lnb = sgu_ln_g[l][None, :], sgu_ln_b[l][None, :]
        wt_p, bt_p = _spatial_tiles(w_spatial[l], b_spatial[l], min(seq, SGU_CHUNK))
        wt_s, bt_s = _spatial_tiles(w_spatial[l], b_spatial[l], min(dec_seq, SGU_CHUNK))

        ya, yb, ga, gb, cs_p, cs_s, v_rows = _mixer_call(
            xb, w_in_b, l, bias, _history_rows(zero_hist, nb), _history_rows(state_conv[l], nb), cw, lng, lnb,
            wt_p, bt_p, wt_s, bt_s, cfg)
        conv_p.append(_state_rows(cs_p))
        conv_s.append(_state_rows(cs_s))
        v_s.append(v_rows.reshape(dec_batch, dec_seq, dsg))

        x1, eidx, gw, rank, cnt = _merge_call(
            ya, yb, ga, gb, x, x_tail, w_a_b, w_b_b, w_o_b, l,
            ln1_g[l][None, :], ln1_b[l][None, :], rw, rb, alpha, n_experts)

        counts = cnt[:, 0].astype(jnp.int32)
        padded = ((counts + TOK_ROWS - 1) // TOK_ROWS) * TOK_ROWS
        seg_end = jnp.cumsum(padded)
        seg_start = seg_end - padded
        n_used = (seg_end[-1:] // TOK_ROWS).astype(jnp.int32)
        tile_row0 = jnp.arange(n_tiles, dtype=jnp.int32) * TOK_ROWS
        tile_expert = jnp.minimum(
            jnp.sum(tile_row0[:, None] >= seg_end[None, :], axis=1), n_experts - 1).astype(jnp.int32)
        e_ids = jnp.arange(n_experts, dtype=jnp.int32)
        pos = rank[:TOP_K] + jnp.sum(
            jnp.where(eidx[:TOP_K, :, None] == e_ids, seg_start.astype(jnp.int32), 0), axis=-1)

        n_t = (padded // TOK_ROWS).astype(jnp.int32)
        later = jnp.where(n_t > 0, e_ids, n_experts)
        nxt_e = jnp.concatenate([jnp.flip(lax.cummin(jnp.flip(later)))[1:],
                                 jnp.full((1,), n_experts, jnp.int32)])
        group = (jnp.cumsum(n_t > 0) - 1).astype(jnp.int32)
        tile_i = jnp.arange(n_tiles, dtype=jnp.int32)
        q = tile_i - (seg_start // TOK_ROWS).astype(jnp.int32)[tile_expert]
        g_t = jnp.maximum(n_t[tile_expert], 1)
        loads = jnp.logical_and(tile_i < n_used[0], nxt_e[tile_expert] < n_experts)
        t_lo = jnp.where(loads, (W_PARTS * q) // g_t, 0).astype(jnp.int32)
        t_hi = jnp.where(loads, (W_PARTS * (q + 1)) // g_t, 0).astype(jnp.int32)
        sched = (tile_expert, n_used, jnp.minimum(nxt_e[tile_expert], n_experts - 1).astype(jnp.int32),
                 t_lo, t_hi, (group[tile_expert] % 2).astype(jnp.int32))

        tok = jnp.tile(jnp.arange(n, dtype=jnp.int32), TOP_K)
        pos_flat = pos.reshape(-1)
        row_token = jnp.zeros((n_tiles * TOK_ROWS,), jnp.int32).at[pos_flat].set(tok)
        ys = _ffn_call(sched, row_token, x1, w_gate, w_up, w_down, l)
        gwc = jnp.pad(gw[:TOP_K].T, ((0, 0), (0, LANES - TOP_K)))
        x, xb = _norm2_call(pos_flat, x1, gwc, ln2_g[l][None, :], ln2_b[l][None, :], ys, alpha,
                            n_p if l == depth - 1 else None)
        x_tail = x[:TOK_ROWS]

    return (x.reshape(batch, seq, d), xb.reshape(dec_batch, dec_seq, d),
            jnp.stack(conv_p), jnp.stack(conv_s), jnp.stack(v_s))
```

```python
import functools
from typing import NamedTuple

import jax
import jax.numpy as jnp
from jax import lax
from jax.experimental import pallas as pl
from jax.experimental.pallas import tpu as pltpu

CHUNK = 64
SGU_CHUNK = 128
SGU_GROUPS = 8
N_EXPERT_GROUPS = 4
TOP_K = 2
LN_EPS = 1e-5
CONV_W = 3

LANES = 128
SUBLANES = 8
COL_BLOCK = 1024
ROW_SUB = 256
MIX_ROWS = 1024
TOK_ROWS = 256
W_PARTS = 8
PREFETCH = 2
CAST_BLOCK_BYTES = 10 * 1024 * 1024
VMEM_LIMIT = 56 * 1024 * 1024

F32 = jnp.float32
BF16 = jnp.bfloat16


class MixCfg(NamedTuple):
    tm: int
    n_p_tiles: int
    seq: int
    dec_seq: int
    n_s: int
    ng: int


def _ln(z, g, b):
    mu = jnp.mean(z, axis=-1, keepdims=True)
    d = z - mu
    var = jnp.mean(d * d, axis=-1, keepdims=True)
    return d * lax.rsqrt(var + LN_EPS) * g + b


def _conv_rows(hin, p2, p1, cw_ref):
    rows = lax.broadcasted_iota(jnp.int32, hin.shape, 0)
    r1 = pltpu.roll(hin, 1, 0)
    r2 = pltpu.roll(hin, 2, 0)
    sh1 = jnp.where(rows == 0, p1, r1)
    sh2 = jnp.where(rows == 0, p2, jnp.where(rows == 1, p1, r2))
    out = cw_ref[3:4, :] + cw_ref[0:1, :] * sh2
    out = out + cw_ref[1:2, :] * sh1
    return out + cw_ref[2:3, :] * hin


def _spatial_keep(seq):
    t = lax.broadcasted_iota(jnp.int32, (ROW_SUB, ROW_SUB), 0)
    s = lax.broadcasted_iota(jnp.int32, (ROW_SUB, ROW_SUB), 1)
    same = (t // seq) == (s // seq)
    causal = ((s % seq) // CHUNK) <= ((t % seq) // CHUNK)
    return jnp.logical_and(same, causal)


BLK_B, BLK_C, BLK_H, BLK_U, BLK_V, BLK_GATES = 0, 1, 2, 3, 4, 5
STEP_H, STEP_C, STEP_B, STEP_V, STEP_U, STEP_GATES = 0, 1, 2, 3, 4, 5


def _mixer_block(j):
    return jnp.where(j < STEP_V, BLK_H - j, jnp.where(j < STEP_GATES, BLK_U + BLK_V - j, j))


def _mixer_kernel(x_ref, w_ref, bias_ref, prevp_ref, prevs_ref, cw_ref, lng_ref, lnb_ref,
                  wtp_ref, btp_ref, wts_ref, bts_ref,
                  ya_ref, yb_ref, gates_ref, csp_ref, css_ref, v_ref,
                  hsc, ssc, carry, wm_sc, bt_sc, *, cfg: MixCfg):
    i = pl.program_id(0)
    j = pl.program_id(1)
    is_prompt = i < cfg.n_p_tiles
    is_sample = i == cfg.n_p_tiles

    def sub_rows(r):
        return pl.ds(r * ROW_SUB, ROW_SUB)

    def proj(r):
        rows = sub_rows(r)
        acc = jnp.dot(x_ref[rows, :], w_ref[0], preferred_element_type=F32)
        return rows, acc + bias_ref[...]

    def for_sub(body):
        @pl.when(is_prompt)
        def _():
            for r in range(cfg.tm // ROW_SUB):
                body(r, False)

        @pl.when(is_sample)
        def _():
            for r in range(cfg.n_s // ROW_SUB):
                body(r, True)

    @pl.when(jnp.logical_and(i == 0, j == 0))
    def _():
        carry[...] = prevp_ref[0]

    @pl.when(j == STEP_H)
    def _():
        def body(r, sample):
            rows, a = proj(r)
            hsc[rows, :] = a
        for_sub(body)

    @pl.when(j == STEP_C)
    def _():
        def body(r, sample):
            rows, a = proj(r)
            hin = a * hsc[rows, :]
            if not sample:
                hsc[rows, :] = _conv_rows(hin, carry[6:7, :], carry[7:8, :], cw_ref)
                carry[...] = hin[ROW_SUB - SUBLANES:, :]
                csp_ref[0] = hin[ROW_SUB - SUBLANES:, :]
            else:
                sr = cfg.dec_seq
                for s in range(ROW_SUB // sr):
                    hs = hin[s * sr:(s + 1) * sr, :]
                    st = r * (ROW_SUB // sr) + s
                    srow = pl.ds(r * ROW_SUB + s * sr, sr)
                    hsc[srow, :] = _conv_rows(hs, prevs_ref[st, 6:7, :], prevs_ref[st, 7:8, :], cw_ref)
                    css_ref[st] = hs[sr - SUBLANES:, :]
        for_sub(body)

    @pl.when(j == STEP_B)
    def _():
        def body(r, sample):
            rows, a = proj(r)
            ya_ref[rows, :] = (a * hsc[rows, :]).astype(ya_ref.dtype)
        for_sub(body)

    @pl.when(j == STEP_V)
    def _():
        @pl.when(is_prompt)
        def _():
            keep = _spatial_keep(min(cfg.seq, SGU_CHUNK))
            for g in range(SGU_GROUPS):
                wm_sc[g] = jnp.where(keep, wtp_ref[g], 0.0).astype(wm_sc.dtype)
            bt_sc[...] = btp_ref[...]

        @pl.when(is_sample)
        def _():
            keep = _spatial_keep(min(cfg.dec_seq, SGU_CHUNK))
            for g in range(SGU_GROUPS):
                wm_sc[g] = jnp.where(keep, wts_ref[g], 0.0).astype(wm_sc.dtype)
            bt_sc[...] = bts_ref[...]

        def body(r, sample):
            rows, a = proj(r)
            vn = _ln(jax.nn.gelu(a), lng_ref[...], lnb_ref[...])
            if sample:
                v_ref[rows, :] = vn
            vnb = vn.astype(BF16)
            for g in range(SGU_GROUPS):
                lanes = slice(g * LANES, (g + 1) * LANES)
                ssc[rows, lanes] = jnp.dot(wm_sc[g], vnb[:, lanes], preferred_element_type=F32) + bt_sc[g]
        for_sub(body)

    @pl.when(j == STEP_U)
    def _():
        def body(r, sample):
            rows, a = proj(r)
            yb_ref[rows, :] = (jax.nn.gelu(a) * ssc[rows, :]).astype(yb_ref.dtype)
        for_sub(body)

    @pl.when(j >= STEP_GATES)
    def _():
        def body(r, sample):
            rows, a = proj(r)
            gates_ref[rows, :] = jax.nn.sigmoid(a)
        for_sub(body)


def _mixer_call(xb, w_in, layer, b_in, prev_p, prev_s, cw, lng, lnb, wt_p, bt_p, wt_s, bt_s, cfg):
    n, d = xb.shape
    cb = COL_BLOCK
    n_j = w_in.shape[2] // cb
    n_i = cfg.n_p_tiles + 1
    n_gate_blocks = 2 * cfg.ng

    def const(shape):
        return pl.BlockSpec(shape, lambda i, j: (0,) * len(shape))

    in_specs = [
        pl.BlockSpec((cfg.tm, d), lambda i, j: (i, 0)),
        pl.BlockSpec((1, d, cb), lambda i, j: (layer, 0, _mixer_block(j))),
        pl.BlockSpec((1, cb), lambda i, j: (0, _mixer_block(j))),
        const(prev_p.shape), const(prev_s.shape), const(cw.shape), const(lng.shape), const(lnb.shape),
        const(wt_p.shape), const(bt_p.shape), const(wt_s.shape), const(bt_s.shape),
    ]
    out_shape = [
        jax.ShapeDtypeStruct((n, cb), BF16),
        jax.ShapeDtypeStruct((n, cb), BF16),
        jax.ShapeDtypeStruct((n, n_gate_blocks * cb), F32),
        jax.ShapeDtypeStruct(prev_p.shape, F32),
        jax.ShapeDtypeStruct(prev_s.shape, F32),
        jax.ShapeDtypeStruct((cfg.n_s, cb), F32),
    ]
    out_specs = [
        pl.BlockSpec((cfg.tm, cb), lambda i, j: (i, 0)),
        pl.BlockSpec((cfg.tm, cb), lambda i, j: (i, 0)),
        pl.BlockSpec((cfg.tm, cb), lambda i, j: (i, jnp.clip(j - STEP_GATES, 0, n_gate_blocks - 1))),
        const(prev_p.shape), const(prev_s.shape), const((cfg.n_s, cb)),
    ]
    scratch = [
        pltpu.VMEM((cfg.tm, cb), F32),
        pltpu.VMEM((cfg.tm, cb), F32),
        pltpu.VMEM((SUBLANES, cb), F32),
        pltpu.VMEM((SGU_GROUPS, ROW_SUB, ROW_SUB), BF16),
        pltpu.VMEM((SGU_GROUPS, ROW_SUB, LANES), F32),
    ]
    assert n_j == BLK_GATES + n_gate_blocks
    return pl.pallas_call(
        functools.partial(_mixer_kernel, cfg=cfg),
        grid=(n_i, n_j),
        in_specs=in_specs,
        out_specs=out_specs,
        out_shape=out_shape,
        scratch_shapes=scratch,
        compiler_params=pltpu.CompilerParams(
            dimension_semantics=("arbitrary", "arbitrary"), vmem_limit_bytes=VMEM_LIMIT),
        name="mixer",
    )(xb, w_in, b_in, prev_p, prev_s, cw, lng, lnb, wt_p, bt_p, wt_s, bt_s)


def _merge_kernel(ya_ref, yb_ref, ga_ref, gb_ref, xa_ref, xb_ref, wa_ref, wb_ref, wo_ref, g1_ref, b1_ref,
                  rw_ref, rb_ref, x1_ref, eidx_ref, gw_ref, rank_ref, cnt_ref, run, *,
                  alpha, n_experts, n_a_tiles, split):
    i = pl.program_id(0)
    tm = xa_ref.shape[0]
    epg = n_experts // N_EXPERT_GROUPS

    @pl.when(i == 0)
    def _():
        run[...] = jnp.zeros_like(run)

    a = jnp.dot(ya_ref[...], wa_ref[0], preferred_element_type=F32)
    b = jnp.dot(yb_ref[...], wb_ref[0], preferred_element_type=F32)
    merged = ga_ref[...] * a + gb_ref[...] * b
    x = xa_ref[...]
    if split:
        x = jnp.where(i < n_a_tiles, x, xb_ref[...])
    z = alpha * x + jnp.dot(merged.astype(BF16), wo_ref[0], preferred_element_type=F32)
    x1 = _ln(z, g1_ref[...], b1_ref[...])
    x1_ref[...] = x1

    logits = jnp.dot(x1.astype(BF16), rw_ref[...], preferred_element_type=F32)
    lt = logits.T[:n_experts, :]
    ex = jnp.exp(lt - jnp.max(lt, axis=0, keepdims=True))
    sc = ex / jnp.sum(ex, axis=0, keepdims=True)
    sel = sc + rb_ref[...]
    sel_r = [sel[e:e + 1, :] for e in range(n_experts)]
    sc_r = [sc[e:e + 1, :] for e in range(n_experts)]

    def pair_max(v):
        best = None
        for p in range(len(v)):
            for q in range(p + 1, len(v)):
                s = v[p] + v[q]
                best = s if best is None else jnp.maximum(best, s)
        return best

    grp = [pair_max(sel_r[g * epg:(g + 1) * epg]) for g in range(N_EXPERT_GROUPS)]
    gi = jnp.zeros((1, tm), jnp.int32)
    gbest = grp[0]
    for g in range(1, N_EXPERT_GROUPS):
        better = grp[g] > gbest
        gi = jnp.where(better, g, gi)
        gbest = jnp.where(better, grp[g], gbest)

    def pick(rows_, idx):
        out = rows_[-1]
        for c in range(len(rows_) - 2, -1, -1):
            out = jnp.where(idx == c, rows_[c], out)
        return out

    vk = [pick([sel_r[g * epg + q] for g in range(N_EXPERT_GROUPS)], gi) for q in range(epg)]
    pk = [pick([sc_r[g * epg + q] for g in range(N_EXPERT_GROUPS)], gi) for q in range(epg)]

    i1 = jnp.zeros((1, tm), jnp.int32)
    b1 = vk[0]
    for q in range(1, epg):
        better = vk[q] > b1
        i1 = jnp.where(better, q, i1)
        b1 = jnp.where(better, vk[q], b1)
    i2 = jnp.zeros((1, tm), jnp.int32)
    b2 = jnp.full((1, tm), -jnp.inf, F32)
    for q in range(epg):
        cand = jnp.logical_and(i1 != q, vk[q] > b2)
        i2 = jnp.where(cand, q, i2)
        b2 = jnp.where(cand, vk[q], b2)
    p1 = pick(pk, i1)
    p2 = pick(pk, i2)
    den = p1 + p2
    e1 = gi * epg + i1
    e2 = gi * epg + i2

    eio = lax.broadcasted_iota(jnp.int32, (n_experts, tm), 0)
    oh1 = (eio == e1).astype(F32)
    oh2 = (eio == e2).astype(F32)
    ts = lax.broadcasted_iota(jnp.int32, (tm, tm), 0)
    tt = lax.broadcasted_iota(jnp.int32, (tm, tm), 1)
    upper = (ts <= tt).astype(BF16)
    inc1 = jnp.dot(oh1.astype(BF16), upper, preferred_element_type=F32)
    inc2 = jnp.dot(oh2.astype(BF16), upper, preferred_element_type=F32)
    tot1 = jnp.sum(oh1, axis=1, keepdims=True)
    tot2 = jnp.sum(oh2, axis=1, keepdims=True)
    base = run[:, 0:1]
    r1 = jnp.sum(oh1 * (inc1 - 1.0 + base), axis=0, keepdims=True)
    r2 = jnp.sum(oh2 * (inc2 - 1.0 + base + tot1), axis=0, keepdims=True)
    new_run = run[...] + tot1 + tot2
    run[...] = new_run
    cnt_ref[...] = new_run

    eidx_ref[...] = jnp.zeros_like(eidx_ref)
    eidx_ref[0:1, :] = e1
    eidx_ref[1:2, :] = e2
    gw_ref[...] = jnp.zeros_like(gw_ref)
    gw_ref[0:1, :] = p1 / den
    gw_ref[1:2, :] = p2 / den
    rank_ref[...] = jnp.zeros_like(rank_ref)
    rank_ref[0:1, :] = r1.astype(jnp.int32)
    rank_ref[1:2, :] = r2.astype(jnp.int32)


def _merge_call(ya, yb, gates, xa, xb, wa, wb, wo, layer, g1, b1, rw, rb, alpha, n_experts):
    n, d = ya.shape[0], xa.shape[1]
    tm = TOK_ROWS
    dc, dsg = ya.shape[1], yb.shape[1]
    n_a_tiles = xa.shape[0] // tm
    assert xa.shape[0] % tm == 0 and xb.shape[0] == tm and n_a_tiles in (n // tm, n // tm - 1)

    def rows(c):
        return pl.BlockSpec((tm, c), lambda i: (i, 0))

    def const(shape):
        return pl.BlockSpec(shape, lambda i: (0,) * len(shape))

    def lanes(r):
        return pl.BlockSpec((r, tm), lambda i: (0, i))

    def layer_block(w):
        return pl.BlockSpec((1,) + w.shape[1:], lambda i: (layer, 0, 0))

    return pl.pallas_call(
        functools.partial(_merge_kernel, alpha=alpha, n_experts=n_experts, n_a_tiles=n_a_tiles,
                          split=n_a_tiles < n // tm),
        grid=(n // tm,),
        in_specs=[rows(dc), rows(dsg), rows(d), pl.BlockSpec((tm, d), lambda i: (i, 1)),
                  pl.BlockSpec((tm, d), lambda i: (jnp.minimum(i, n_a_tiles - 1), 0)), const((tm, d)),
                  layer_block(wa), layer_block(wb), layer_block(wo), const(g1.shape), const(b1.shape),
                  const(rw.shape), const(rb.shape)],
        out_specs=[rows(d), lanes(SUBLANES), lanes(SUBLANES), lanes(SUBLANES),
                   const((n_experts, LANES))],
        out_shape=[jax.ShapeDtypeStruct((n, d), F32),
                   jax.ShapeDtypeStruct((SUBLANES, n), jnp.int32),
                   jax.ShapeDtypeStruct((SUBLANES, n), F32),
                   jax.ShapeDtypeStruct((SUBLANES, n), jnp.int32),
                   jax.ShapeDtypeStruct((n_experts, LANES), F32)],
        scratch_shapes=[pltpu.VMEM((n_experts, LANES), F32)],
        compiler_params=pltpu.CompilerParams(
            dimension_semantics=("arbitrary",), vmem_limit_bytes=VMEM_LIMIT),
        name="merge_route",
    )(ya, yb, gates, gates, xa, xb, wa, wb, wo, g1, b1, rw, rb)


def _row_gather(src_hbm, dst, sem, index_of, n_rows):
    for r in range(n_rows):
        pltpu.make_async_copy(src_hbm.at[pl.ds(index_of(r), 1)], dst.at[pl.ds(r, 1)], sem).start()


def _row_gather_wait(src_hbm, dst, sem):
    pltpu.make_async_copy(src_hbm.at[pl.ds(0, dst.shape[0])], dst, sem).wait()


def _ffn_kernel(te_ref, nu_ref, ne_ref, t0_ref, t1_ref, ws_ref, tok_ref,
                x1_hbm, wg_hbm, wu_hbm, wd_hbm, ys_ref,
                xbuf, xsem, wgb, wub, wdb, sg, su, sd, wsem, *, layer):
    i = pl.program_id(0)
    n_used = nu_ref[0]
    tm = xbuf.shape[1]
    slot = i % (PREFETCH + 1)

    def part_copies(e, t, s):
        return (pltpu.make_async_copy(wg_hbm.at[layer, e, t], sg.at[s], wsem.at[s]),
                pltpu.make_async_copy(wu_hbm.at[layer, e, t], su.at[s], wsem.at[s]),
                pltpu.make_async_copy(wd_hbm.at[layer, e, t], sd.at[s], wsem.at[s]))

    def start_part(e, t, s):
        for c in part_copies(e, t, s):
            c.start(priority=1)

    def finish_part(e, t, s, w):
        for c in part_copies(e, t, s):
            c.wait()
        rin, rout = sg.shape[1], sd.shape[1]
        wgb[w, pl.ds(pl.multiple_of(t * rin, rin), rin), :] = sg[s].astype(BF16)
        wub[w, pl.ds(pl.multiple_of(t * rin, rin), rin), :] = su[s].astype(BF16)
        wdb[w, pl.ds(pl.multiple_of(t * rout, rout), rout), :] = sd[s].astype(BF16)

    def load_parts(e, t_lo, t_hi, w):
        def step(t, c):
            s = (t - t_lo) % 2
            finish_part(e, t, s, w)

            @pl.when(t + 2 < t_hi)
            def _():
                start_part(e, t + 2, s)
            return c
        lax.fori_loop(t_lo, t_hi, step, 0)

    def start_first_parts(e, t_lo, t_hi):
        @pl.when(t_lo < t_hi)
        def _():
            start_part(e, t_lo, 0)

        @pl.when(t_lo + 1 < t_hi)
        def _():
            start_part(e, t_lo + 1, 1)

    @pl.when(i == 0)
    def _():
        for a in range(PREFETCH):
            tile = jnp.minimum(a, n_used - 1)

            def first(r, c, a=a, tile=tile):
                pltpu.make_async_copy(x1_hbm.at[pl.ds(tok_ref[tile * tm + r], 1)], xbuf.at[a, pl.ds(r, 1)],
                                      xsem.at[a]).start()
                return c
            lax.fori_loop(0, tm, first, 0)
        start_first_parts(te_ref[0], 0, W_PARTS)
        load_parts(te_ref[0], 0, W_PARTS, 0)

    @pl.when(i < n_used)
    def _():
        w = ws_ref[i]
        e_next, t_lo, t_hi = ne_ref[i], t0_ref[i], t1_ref[i]
        start_first_parts(e_next, t_lo, t_hi)
        _row_gather_wait(x1_hbm, xbuf.at[slot], xsem.at[slot])
        x = xbuf[slot].astype(BF16)
        ahead = jnp.minimum(i + PREFETCH, n_used - 1)
        into = (i + PREFETCH) % (PREFETCH + 1)
        _row_gather(x1_hbm, xbuf.at[into], xsem.at[into], lambda r: tok_ref[ahead * tm + r], tm)
        g = jnp.dot(x, wgb[w], preferred_element_type=F32)
        u = jnp.dot(x, wub[w], preferred_element_type=F32)
        h = ((g * jax.nn.sigmoid(g)) * u).astype(BF16)
        ys_ref[...] = jnp.dot(h, wdb[w], preferred_element_type=F32)
        load_parts(e_next, t_lo, t_hi, 1 - w)

    @pl.when(i == n_used - 1)
    def _():
        for a in range(1, PREFETCH + 1):
            s = (i + a) % (PREFETCH + 1)
            _row_gather_wait(x1_hbm, xbuf.at[s], xsem.at[s])

    @pl.when(i >= n_used)
    def _():
        ys_ref[...] = jnp.zeros_like(ys_ref)


def _ffn_call(sched, row_token, x1, w_gate, w_up, w_down, layer):
    d = x1.shape[1]
    p = row_token.shape[0]
    depth, n_e, _, f = w_gate.shape
    tm = TOK_ROWS
    wg = w_gate.reshape(depth, n_e, W_PARTS, d // W_PARTS, f)
    wu = w_up.reshape(depth, n_e, W_PARTS, d // W_PARTS, f)
    wd = w_down.reshape(depth, n_e, W_PARTS, f // W_PARTS, d)
    any_spec = pl.BlockSpec(memory_space=pl.ANY)
    return pl.pallas_call(
        functools.partial(_ffn_kernel, layer=layer),
        grid_spec=pltpu.PrefetchScalarGridSpec(
            num_scalar_prefetch=7,
            grid=(p // tm,),
            in_specs=[any_spec, any_spec, any_spec, any_spec],
            out_specs=pl.BlockSpec((tm, d), lambda i, *_: (i, 0)),
            scratch_shapes=[pltpu.VMEM((PREFETCH + 1, tm, d), F32), pltpu.SemaphoreType.DMA((PREFETCH + 1,)),
                            pltpu.VMEM((2, d, f), BF16), pltpu.VMEM((2, d, f), BF16),
                            pltpu.VMEM((2, f, d), BF16),
                            pltpu.VMEM((2, d // W_PARTS, f), F32), pltpu.VMEM((2, d // W_PARTS, f), F32),
                            pltpu.VMEM((2, f // W_PARTS, d), F32), pltpu.SemaphoreType.DMA((2,))],
        ),
        out_shape=jax.ShapeDtypeStruct((p, d), F32),
        compiler_params=pltpu.CompilerParams(
            dimension_semantics=("arbitrary",), vmem_limit_bytes=VMEM_LIMIT),
        name="expert_ffn",
    )(*sched, row_token, x1, wg, wu, wd)


def _norm2_kernel(pos_ref, x1_ref, gw_ref, g_ref, b_ref, ys_hbm, out_a, out_b, ybuf, sem, *, alpha,
                  n_p_tiles):
    i = pl.program_id(0)
    n_steps = pl.num_programs(0)
    tm = x1_ref.shape[0]
    n = tm * n_steps
    slot = i % (PREFETCH + 1)

    @pl.when(i == 0)
    def _():
        for a in range(PREFETCH):
            tile = jnp.minimum(a, n_steps - 1)

            def first(r, c, a=a, tile=tile):
                for k in range(TOP_K):
                    pltpu.make_async_copy(ys_hbm.at[pl.ds(pos_ref[k * n + tile * tm + r], 1)],
                                          ybuf.at[a, k, pl.ds(r, 1)], sem.at[a]).start()
                return c
            lax.fori_loop(0, tm, first, 0)

    for k in range(TOP_K):
        _row_gather_wait(ys_hbm, ybuf.at[slot, k], sem.at[slot])
    ahead = jnp.minimum(i + PREFETCH, n_steps - 1)
    into = (i + PREFETCH) % (PREFETCH + 1)
    for k in range(TOP_K):
        _row_gather(ys_hbm, ybuf.at[into, k], sem.at[into],
                    lambda r, k=k: pos_ref[k * n + ahead * tm + r], tm)
    y = None
    for k in range(TOP_K):
        term = gw_ref[:, k:k + 1] * ybuf[slot, k]
        y = term if y is None else y + term
    x2 = _ln(alpha * x1_ref[...] + y, g_ref[...], b_ref[...])
    if n_p_tiles is None:
        out_a[...] = x2
        out_b[...] = x2.astype(out_b.dtype)
    else:
        @pl.when(i < n_p_tiles)
        def _():
            out_a[...] = x2

        @pl.when(i >= n_p_tiles)
        def _():
            out_b[...] = x2

    @pl.when(i == n_steps - 1)
    def _():
        for a in range(1, PREFETCH + 1):
            s = (i + a) % (PREFETCH + 1)
            for k in range(TOP_K):
                _row_gather_wait(ys_hbm, ybuf.at[s, k], sem.at[s])


def _norm2_call(pos_flat, x1, gwc, g, b, ys, alpha, n_p=None):
    n, d = x1.shape
    tm = TOK_ROWS
    rows = pl.BlockSpec((tm, d), lambda i, pos: (i, 0))
    const = pl.BlockSpec((1, d), lambda i, pos: (0, 0))
    if n_p is None:
        n_p_tiles = None
        out_specs = [rows, rows]
        out_shape = [jax.ShapeDtypeStruct((n, d), F32), jax.ShapeDtypeStruct((n, d), BF16)]
    else:
        assert n - n_p == tm
        n_p_tiles = n_p // tm
        out_specs = [pl.BlockSpec((tm, d), lambda i, pos: (jnp.minimum(i, n_p_tiles - 1), 0)),
                     pl.BlockSpec((tm, d), lambda i, pos: (0, 0))]
        out_shape = [jax.ShapeDtypeStruct((n_p, d), F32), jax.ShapeDtypeStruct((tm, d), F32)]
    return pl.pallas_call(
        functools.partial(_norm2_kernel, alpha=alpha, n_p_tiles=n_p_tiles),
        grid_spec=pltpu.PrefetchScalarGridSpec(
            num_scalar_prefetch=1,
            grid=(n // tm,),
            in_specs=[rows, pl.BlockSpec((tm, LANES), lambda i, pos: (i, 0)), const, const,
                      pl.BlockSpec(memory_space=pl.ANY)],
            out_specs=out_specs,
            scratch_shapes=[pltpu.VMEM((PREFETCH + 1, TOP_K, tm, d), F32),
                            pltpu.SemaphoreType.DMA((PREFETCH + 1,))],
        ),
        out_shape=out_shape,
        compiler_params=pltpu.CompilerParams(
            dimension_semantics=("arbitrary",), vmem_limit_bytes=VMEM_LIMIT),
        name="combine_norm2",
    )(pos_flat, x1, gwc, g, b, ys)


def _cast_kernel(x_ref, o_ref):
    o_ref[...] = x_ref[...].astype(o_ref.dtype)


def _cast_bf16(w):
    cols = w.shape[-1]
    rows = w.size // cols
    br = rows
    while br * cols * 4 > CAST_BLOCK_BYTES and br % 32 == 0:
        br //= 2
    block = pl.BlockSpec((br, cols), lambda i: (i, 0))
    out = pl.pallas_call(
        _cast_kernel,
        grid=(rows // br,),
        in_specs=[block],
        out_specs=block,
        out_shape=jax.ShapeDtypeStruct((rows, cols), BF16),
        compiler_params=pltpu.CompilerParams(
            dimension_semantics=("arbitrary",), vmem_limit_bytes=VMEM_LIMIT),
        name="cast_weights",
    )(w.reshape(rows, cols))
    return out.reshape(w.shape)


def _stack_kernel(a_ref, b_ref, o_ref, *, n_a_tiles):
    i = pl.program_id(0)

    @pl.when(i < n_a_tiles)
    def _():
        o_ref[...] = a_ref[...].astype(o_ref.dtype)

    @pl.when(i >= n_a_tiles)
    def _():
        o_ref[...] = b_ref[...].astype(o_ref.dtype)


def _stack_bf16(a, b):
    tm = b.shape[0]
    d = a.shape[1]
    n_a_tiles = a.shape[0] // tm
    assert a.shape[0] % tm == 0
    return pl.pallas_call(
        functools.partial(_stack_kernel, n_a_tiles=n_a_tiles),
        grid=(n_a_tiles + 1,),
        in_specs=[pl.BlockSpec((tm, d), lambda i: (jnp.minimum(i, n_a_tiles - 1), 0)),
                  pl.BlockSpec((tm, d), lambda i: (0, 0))],
        out_specs=pl.BlockSpec((tm, d), lambda i: (i, 0)),
        out_shape=jax.ShapeDtypeStruct((a.shape[0] + tm, d), BF16),
        compiler_params=pltpu.CompilerParams(
            dimension_semantics=("arbitrary",), vmem_limit_bytes=VMEM_LIMIT),
        name="stack_tokens",
    )(a, b)


def _spatial_tiles(w_s, b_s, seq):
    reps = ROW_SUB // seq
    wt = jnp.tile(w_s[:, :seq, :seq], (1, reps, reps))
    bt = jnp.tile(b_s[:, :seq], (1, reps))
    return wt, jnp.broadcast_to(bt[:, :, None], bt.shape + (LANES,))


def _history_rows(hist):
    return jnp.pad(hist, ((0, 0), (SUBLANES - (CONV_W - 1), 0), (0, 0)))


def _state_rows(cs):
    return cs[:, SUBLANES - (CONV_W - 1):, :]


def kernel(x_prompt, x_sample, state_conv, w_in, b_in, conv_w, conv_b, sgu_ln_g, sgu_ln_b, w_spatial,
           b_spatial, w_a_out, w_b_out, w_o, ln1_g, ln1_b, router_w, router_bias, w_gate, w_up, w_down,
           ln2_g, ln2_b):
    depth, d, n_col = w_in.shape
    batch, seq, _ = x_prompt.shape
    dec_batch, dec_seq, _ = x_sample.shape
    dc = conv_w.shape[2]
    dsg = sgu_ln_g.shape[1]
    n_experts = router_w.shape[1]
    assert batch == 1 and n_col == 3 * dc + 2 * dsg + 2 * d
    assert dc == COL_BLOCK and dsg == COL_BLOCK and d % COL_BLOCK == 0
    assert dsg == SGU_GROUPS * LANES and n_experts % N_EXPERT_GROUPS == 0
    n_p, n_s = batch * seq, dec_batch * dec_seq
    n = n_p + n_s
    tm_p = min(MIX_ROWS, n_p)
    assert n_p % tm_p == 0 and n_s == ROW_SUB and ROW_SUB % dec_seq == 0 and n_p % TOK_ROWS == 0
    assert seq % SGU_CHUNK == 0 and SGU_CHUNK % dec_seq == 0
    alpha = (2.0 * depth) ** 0.25
    cfg = MixCfg(tm_p, n_p // tm_p, seq, dec_seq, n_s, d // COL_BLOCK)

    n_pairs = TOP_K * n
    n_tiles = n_pairs // TOK_ROWS + n_experts
    assert n_pairs % TOK_ROWS == 0

    x, x_tail = x_prompt.reshape(n_p, d), x_sample.reshape(n_s, d)
    xb = _stack_bf16(x, x_tail)
    rw = jnp.pad(router_w, ((0, 0), (0, LANES - n_experts))).astype(BF16)
    rb = jnp.broadcast_to(router_bias[:, None], (n_experts, TOK_ROWS))
    zero_hist = jnp.zeros((batch, CONV_W - 1, dc), F32)

    w_in_b, w_a_b, w_b_b, w_o_b = (_cast_bf16(w) for w in (w_in, w_a_out, w_b_out, w_o))

    conv_p, conv_s, v_s = [], [], []
    for l in range(depth):
        bias = b_in[l][None, :]
        cw = jnp.concatenate([conv_w[l], conv_b[l][None, :],
                              jnp.zeros((SUBLANES - CONV_W - 1, dc), F32)], axis=0)
        lng, lnb = sgu_ln_g[l][None, :], sgu_ln_b[l][None, :]
        wt_p, bt_p = _spatial_tiles(w_spatial[l], b_spatial[l], min(seq, SGU_CHUNK))
        wt_s, bt_s = _spatial_tiles(w_spatial[l], b_spatial[l], min(dec_seq, SGU_CHUNK))

        ya, yb, gates, cs_p, cs_s, v_rows = _mixer_call(
            xb, w_in_b, l, bias, _history_rows(zero_hist), _history_rows(state_conv[l]), cw, lng, lnb,
            wt_p, bt_p, wt_s, bt_s, cfg)
        conv_p.append(_state_rows(cs_p))
        conv_s.append(_state_rows(cs_s))
        v_s.append(v_rows.reshape(dec_batch, dec_seq, dsg))

        x1, eidx, gw, rank, cnt = _merge_call(
            ya, yb, gates, x, x_tail, w_a_b, w_b_b, w_o_b, l,
            ln1_g[l][None, :], ln1_b[l][None, :], rw, rb, alpha, n_experts)

        counts = cnt[:, 0].astype(jnp.int32)
        padded = ((counts + TOK_ROWS - 1) // TOK_ROWS) * TOK_ROWS
        seg_end = jnp.cumsum(padded)
        seg_start = seg_end - padded
        n_used = (seg_end[-1:] // TOK_ROWS).astype(jnp.int32)
        tile_row0 = jnp.arange(n_tiles, dtype=jnp.int32) * TOK_ROWS
        tile_expert = jnp.minimum(
            jnp.sum(tile_row0[:, None] >= seg_end[None, :], axis=1), n_experts - 1).astype(jnp.int32)
        e_ids = jnp.arange(n_experts, dtype=jnp.int32)
        pos = rank[:TOP_K] + jnp.sum(
            jnp.where(eidx[:TOP_K, :, None] == e_ids, seg_start.astype(jnp.int32), 0), axis=-1)

        n_t = (padded // TOK_ROWS).astype(jnp.int32)
        later = jnp.where(n_t > 0, e_ids, n_experts)
        nxt_e = jnp.concatenate([jnp.flip(lax.cummin(jnp.flip(later)))[1:],
                                 jnp.full((1,), n_experts, jnp.int32)])
        group = (jnp.cumsum(n_t > 0) - 1).astype(jnp.int32)
        tile_i = jnp.arange(n_tiles, dtype=jnp.int32)
        of_tile = tile_expert[:, None] == e_ids

        def lookup(table):
            return jnp.sum(jnp.where(of_tile, table.astype(jnp.int32), 0), axis=1)

        q = tile_i - lookup(seg_start // TOK_ROWS)
        g_t = jnp.maximum(lookup(n_t), 1)
        nxt_t = lookup(nxt_e)
        loads = jnp.logical_and(tile_i < n_used[0], nxt_t < n_experts)
        t_lo = jnp.where(loads, (W_PARTS * q) // g_t, 0).astype(jnp.int32)
        t_hi = jnp.where(loads, (W_PARTS * (q + 1)) // g_t, 0).astype(jnp.int32)
        sched = (tile_expert, n_used, jnp.minimum(nxt_t, n_experts - 1).astype(jnp.int32),
                 t_lo, t_hi, (lookup(group) % 2).astype(jnp.int32))

        tok = jnp.tile(jnp.arange(n, dtype=jnp.int32), TOP_K)
        pos_flat = pos.reshape(-1)
        row_token = jnp.zeros((n_tiles * TOK_ROWS,), jnp.int32).at[pos_flat].set(tok)
        ys = _ffn_call(sched, row_token, x1, w_gate, w_up, w_down, l)
        gwc = jnp.pad(gw[:TOP_K].T, ((0, 0), (0, LANES - TOP_K)))
        x, xb = _norm2_call(pos_flat, x1, gwc, ln2_g[l][None, :], ln2_b[l][None, :], ys, alpha,
                            n_p if l == depth - 1 else None)
        x_tail = x[:TOK_ROWS]

    return (x.reshape(batch, seq, d), xb.reshape(dec_batch, dec_seq, d),
            jnp.stack(conv_p), jnp.stack(conv_s), jnp.stack(v_s))
```

```python
import functools
from typing import NamedTuple

import jax
import jax.numpy as jnp
from jax import lax
from jax.experimental import pallas as pl
from jax.experimental.pallas import tpu as pltpu

CHUNK = 64
SGU_CHUNK = 128
SGU_GROUPS = 8
N_EXPERT_GROUPS = 4
TOP_K = 2
LN_EPS = 1e-5
CONV_W = 3

LANES = 128
SUBLANES = 8
COL_BLOCK = 1024
ROW_SUB = 256
MIX_ROWS = 1024
TOK_ROWS = 256
W_PARTS = 8
PREFETCH = 2
CAST_BLOCK_BYTES = 10 * 1024 * 1024
VMEM_LIMIT = 56 * 1024 * 1024

F32 = jnp.float32
BF16 = jnp.bfloat16


class MixCfg(NamedTuple):
    tm: int
    n_p_tiles: int
    seq: int
    dec_seq: int
    n_s: int
    ng: int


def _ln(z, g, b):
    mu = jnp.mean(z, axis=-1, keepdims=True)
    d = z - mu
    var = jnp.mean(d * d, axis=-1, keepdims=True)
    return d * lax.rsqrt(var + LN_EPS) * g + b


def _conv_rows(hin, p2, p1, cw_ref):
    rows = lax.broadcasted_iota(jnp.int32, hin.shape, 0)
    r1 = pltpu.roll(hin, 1, 0)
    r2 = pltpu.roll(hin, 2, 0)
    sh1 = jnp.where(rows == 0, p1, r1)
    sh2 = jnp.where(rows == 0, p2, jnp.where(rows == 1, p1, r2))
    out = cw_ref[3:4, :] + cw_ref[0:1, :] * sh2
    out = out + cw_ref[1:2, :] * sh1
    return out + cw_ref[2:3, :] * hin


def _spatial_keep(seq):
    t = lax.broadcasted_iota(jnp.int32, (ROW_SUB, ROW_SUB), 0)
    s = lax.broadcasted_iota(jnp.int32, (ROW_SUB, ROW_SUB), 1)
    same = (t // seq) == (s // seq)
    causal = ((s % seq) // CHUNK) <= ((t % seq) // CHUNK)
    return jnp.logical_and(same, causal)


BLK_B, BLK_C, BLK_H, BLK_U, BLK_V, BLK_GATES = 0, 1, 2, 3, 4, 5
STEP_H, STEP_C, STEP_B, STEP_V, STEP_U, STEP_GATES = 0, 1, 2, 3, 4, 5


def _mixer_block(j):
    return jnp.where(j < STEP_V, BLK_H - j, jnp.where(j < STEP_GATES, BLK_U + BLK_V - j, j))


def _mixer_kernel(x_ref, w_ref, bias_ref, prevp_ref, prevs_ref, cw_ref, lng_ref, lnb_ref,
                  wtp_ref, btp_ref, wts_ref, bts_ref,
                  ya_ref, yb_ref, gates_ref, csp_ref, css_ref, v_ref,
                  hsc, ssc, vsc, carry, wm_sc, bt_sc, *, cfg: MixCfg):
    i = pl.program_id(0)
    j = pl.program_id(1)
    is_prompt = i < cfg.n_p_tiles
    is_sample = i == cfg.n_p_tiles

    def sub_rows(r):
        return pl.ds(r * ROW_SUB, ROW_SUB)

    def proj(r):
        rows = sub_rows(r)
        acc = jnp.dot(x_ref[rows, :], w_ref[0], preferred_element_type=F32)
        return rows, acc + bias_ref[...]

    def for_sub(body):
        @pl.when(is_prompt)
        def _():
            for r in range(cfg.tm // ROW_SUB):
                body(r, False)

        @pl.when(is_sample)
        def _():
            for r in range(cfg.n_s // ROW_SUB):
                body(r, True)

    @pl.when(jnp.logical_and(i == 0, j == 0))
    def _():
        carry[...] = prevp_ref[0]

    @pl.when(j == STEP_H)
    def _():
        def body(r, sample):
            rows, a = proj(r)
            hsc[rows, :] = a
        for_sub(body)

    @pl.when(j == STEP_C)
    def _():
        def body(r, sample):
            rows, a = proj(r)
            hin = a * hsc[rows, :]
            if not sample:
                hsc[rows, :] = _conv_rows(hin, carry[6:7, :], carry[7:8, :], cw_ref)
                carry[...] = hin[ROW_SUB - SUBLANES:, :]
                csp_ref[0] = hin[ROW_SUB - SUBLANES:, :]
            else:
                sr = cfg.dec_seq
                for s in range(ROW_SUB // sr):
                    hs = hin[s * sr:(s + 1) * sr, :]
                    st = r * (ROW_SUB // sr) + s
                    srow = pl.ds(r * ROW_SUB + s * sr, sr)
                    hsc[srow, :] = _conv_rows(hs, prevs_ref[st, 6:7, :], prevs_ref[st, 7:8, :], cw_ref)
                    css_ref[st] = hs[sr - SUBLANES:, :]
        for_sub(body)

    @pl.when(j == STEP_B)
    def _():
        def body(r, sample):
            rows, a = proj(r)
            ya_ref[rows, :] = (a * hsc[rows, :]).astype(ya_ref.dtype)
        for_sub(body)

    @pl.when(j == STEP_V)
    def _():
        @pl.when(is_prompt)
        def _():
            keep = _spatial_keep(min(cfg.seq, SGU_CHUNK))
            for g in range(SGU_GROUPS):
                wm_sc[g] = jnp.where(keep, wtp_ref[g], 0.0).astype(wm_sc.dtype)
            bt_sc[...] = btp_ref[...]

        @pl.when(is_sample)
        def _():
            keep = _spatial_keep(min(cfg.dec_seq, SGU_CHUNK))
            for g in range(SGU_GROUPS):
                wm_sc[g] = jnp.where(keep, wts_ref[g], 0.0).astype(wm_sc.dtype)
            bt_sc[...] = bts_ref[...]

        def spatial(n_sub):
            for g in range(SGU_GROUPS):
                wide = jnp.dot(wm_sc[g], vsc[:, g * n_sub * LANES:(g + 1) * n_sub * LANES],
                               preferred_element_type=F32)
                for r in range(n_sub):
                    ssc[sub_rows(r), g * LANES:(g + 1) * LANES] = wide[:, r * LANES:(r + 1) * LANES] + bt_sc[g]

        def body(r, sample):
            n_sub = (cfg.n_s if sample else cfg.tm) // ROW_SUB
            rows, a = proj(r)
            vn = _ln(jax.nn.gelu(a), lng_ref[...], lnb_ref[...])
            if sample:
                v_ref[rows, :] = vn
            vnb = vn.astype(BF16)
            for g in range(SGU_GROUPS):
                col = (g * n_sub + r) * LANES
                vsc[:, col:col + LANES] = vnb[:, g * LANES:(g + 1) * LANES]
            if r == n_sub - 1:
                spatial(n_sub)
        for_sub(body)

    @pl.when(j == STEP_U)
    def _():
        def body(r, sample):
            rows, a = proj(r)
            yb_ref[rows, :] = (jax.nn.gelu(a) * ssc[rows, :]).astype(yb_ref.dtype)
        for_sub(body)

    @pl.when(j >= STEP_GATES)
    def _():
        def body(r, sample):
            rows, a = proj(r)
            gates_ref[rows, :] = jax.nn.sigmoid(a)
        for_sub(body)


def _mixer_call(xb, w_in, layer, b_in, prev_p, prev_s, cw, lng, lnb, wt_p, bt_p, wt_s, bt_s, cfg):
    n, d = xb.shape
    cb = COL_BLOCK
    n_j = w_in.shape[2] // cb
    n_i = cfg.n_p_tiles + 1
    n_gate_blocks = 2 * cfg.ng

    def const(shape):
        return pl.BlockSpec(shape, lambda i, j: (0,) * len(shape))

    in_specs = [
        pl.BlockSpec((cfg.tm, d), lambda i, j: (i, 0)),
        pl.BlockSpec((1, d, cb), lambda i, j: (layer, 0, _mixer_block(j))),
        pl.BlockSpec((1, cb), lambda i, j: (0, _mixer_block(j))),
        const(prev_p.shape), const(prev_s.shape), const(cw.shape), const(lng.shape), const(lnb.shape),
        const(wt_p.shape), const(bt_p.shape), const(wt_s.shape), const(bt_s.shape),
    ]
    out_shape = [
        jax.ShapeDtypeStruct((n, cb), BF16),
        jax.ShapeDtypeStruct((n, cb), BF16),
        jax.ShapeDtypeStruct((n, n_gate_blocks * cb), F32),
        jax.ShapeDtypeStruct(prev_p.shape, F32),
        jax.ShapeDtypeStruct(prev_s.shape, F32),
        jax.ShapeDtypeStruct((cfg.n_s, cb), F32),
    ]
    out_specs = [
        pl.BlockSpec((cfg.tm, cb), lambda i, j: (i, 0)),
        pl.BlockSpec((cfg.tm, cb), lambda i, j: (i, 0)),
        pl.BlockSpec((cfg.tm, cb), lambda i, j: (i, jnp.clip(j - STEP_GATES, 0, n_gate_blocks - 1))),
        const(prev_p.shape), const(prev_s.shape), const((cfg.n_s, cb)),
    ]
    scratch = [
        pltpu.VMEM((cfg.tm, cb), F32),
        pltpu.VMEM((cfg.tm, cb), F32),
        pltpu.VMEM((ROW_SUB, (cfg.tm // ROW_SUB) * cb), BF16),
        pltpu.VMEM((SUBLANES, cb), F32),
        pltpu.VMEM((SGU_GROUPS, ROW_SUB, ROW_SUB), BF16),
        pltpu.VMEM((SGU_GROUPS, ROW_SUB, LANES), F32),
    ]
    assert n_j == BLK_GATES + n_gate_blocks
    return pl.pallas_call(
        functools.partial(_mixer_kernel, cfg=cfg),
        grid=(n_i, n_j),
        in_specs=in_specs,
        out_specs=out_specs,
        out_shape=out_shape,
        scratch_shapes=scratch,
        compiler_params=pltpu.CompilerParams(
            dimension_semantics=("arbitrary", "arbitrary"), vmem_limit_bytes=VMEM_LIMIT),
        name="mixer",
    )(xb, w_in, b_in, prev_p, prev_s, cw, lng, lnb, wt_p, bt_p, wt_s, bt_s)


def _merge_kernel(ya_ref, yb_ref, ga_ref, gb_ref, xa_ref, xb_ref, wa_ref, wb_ref, wo_ref, g1_ref, b1_ref,
                  rw_ref, rb_ref, x1_ref, eidx_ref, gw_ref, rank_ref, cnt_ref, run, *,
                  alpha, n_experts, n_a_tiles, split):
    i = pl.program_id(0)
    tm = xa_ref.shape[0]
    epg = n_experts // N_EXPERT_GROUPS

    @pl.when(i == 0)
    def _():
        run[...] = jnp.zeros_like(run)

    a = jnp.dot(ya_ref[...], wa_ref[0], preferred_element_type=F32)
    b = jnp.dot(yb_ref[...], wb_ref[0], preferred_element_type=F32)
    merged = ga_ref[...] * a + gb_ref[...] * b
    x = xa_ref[...]
    if split:
        x = jnp.where(i < n_a_tiles, x, xb_ref[...])
    z = alpha * x + jnp.dot(merged.astype(BF16), wo_ref[0], preferred_element_type=F32)
    x1 = _ln(z, g1_ref[...], b1_ref[...])
    x1_ref[...] = x1

    logits = jnp.dot(x1.astype(BF16), rw_ref[...], preferred_element_type=F32)
    lt = logits.T[:n_experts, :]
    ex = jnp.exp(lt - jnp.max(lt, axis=0, keepdims=True))
    sc = ex / jnp.sum(ex, axis=0, keepdims=True)
    sel = sc + rb_ref[...]
    sel_r = [sel[e:e + 1, :] for e in range(n_experts)]
    sc_r = [sc[e:e + 1, :] for e in range(n_experts)]

    def pair_max(v):
        best = None
        for p in range(len(v)):
            for q in range(p + 1, len(v)):
                s = v[p] + v[q]
                best = s if best is None else jnp.maximum(best, s)
        return best

    grp = [pair_max(sel_r[g * epg:(g + 1) * epg]) for g in range(N_EXPERT_GROUPS)]
    gi = jnp.zeros((1, tm), jnp.int32)
    gbest = grp[0]
    for g in range(1, N_EXPERT_GROUPS):
        better = grp[g] > gbest
        gi = jnp.where(better, g, gi)
        gbest = jnp.where(better, grp[g], gbest)

    def pick(rows_, idx):
        out = rows_[-1]
        for c in range(len(rows_) - 2, -1, -1):
            out = jnp.where(idx == c, rows_[c], out)
        return out

    vk = [pick([sel_r[g * epg + q] for g in range(N_EXPERT_GROUPS)], gi) for q in range(epg)]
    pk = [pick([sc_r[g * epg + q] for g in range(N_EXPERT_GROUPS)], gi) for q in range(epg)]

    i1 = jnp.zeros((1, tm), jnp.int32)
    b1 = vk[0]
    for q in range(1, epg):
        better = vk[q] > b1
        i1 = jnp.where(better, q, i1)
        b1 = jnp.where(better, vk[q], b1)
    i2 = jnp.zeros((1, tm), jnp.int32)
    b2 = jnp.full((1, tm), -jnp.inf, F32)
    for q in range(epg):
        cand = jnp.logical_and(i1 != q, vk[q] > b2)
        i2 = jnp.where(cand, q, i2)
        b2 = jnp.where(cand, vk[q], b2)
    p1 = pick(pk, i1)
    p2 = pick(pk, i2)
    den = p1 + p2
    e1 = gi * epg + i1
    e2 = gi * epg + i2

    eio = lax.broadcasted_iota(jnp.int32, (n_experts, tm), 0)
    oh1 = (eio == e1).astype(F32)
    oh2 = (eio == e2).astype(F32)
    ts = lax.broadcasted_iota(jnp.int32, (tm, tm), 0)
    tt = lax.broadcasted_iota(jnp.int32, (tm, tm), 1)
    upper = (ts <= tt).astype(BF16)
    inc1 = jnp.dot(oh1.astype(BF16), upper, preferred_element_type=F32)
    inc2 = jnp.dot(oh2.astype(BF16), upper, preferred_element_type=F32)
    tot1 = jnp.sum(oh1, axis=1, keepdims=True)
    tot2 = jnp.sum(oh2, axis=1, keepdims=True)
    base = run[:, 0:1]
    r1 = jnp.sum(oh1 * (inc1 - 1.0 + base), axis=0, keepdims=True)
    r2 = jnp.sum(oh2 * (inc2 - 1.0 + base + tot1), axis=0, keepdims=True)
    new_run = run[...] + tot1 + tot2
    run[...] = new_run
    cnt_ref[...] = new_run

    eidx_ref[...] = jnp.zeros_like(eidx_ref)
    eidx_ref[0:1, :] = e1
    eidx_ref[1:2, :] = e2
    gw_ref[...] = jnp.zeros_like(gw_ref)
    gw_ref[0:1, :] = p1 / den
    gw_ref[1:2, :] = p2 / den
    rank_ref[...] = jnp.zeros_like(rank_ref)
    rank_ref[0:1, :] = r1.astype(jnp.int32)
    rank_ref[1:2, :] = r2.astype(jnp.int32)


def _merge_call(ya, yb, gates, xa, xb, wa, wb, wo, layer, g1, b1, rw, rb, alpha, n_experts):
    n, d = ya.shape[0], xa.shape[1]
    tm = TOK_ROWS
    dc, dsg = ya.shape[1], yb.shape[1]
    n_a_tiles = xa.shape[0] // tm
    assert xa.shape[0] % tm == 0 and xb.shape[0] == tm and n_a_tiles in (n // tm, n // tm - 1)

    def rows(c):
        return pl.BlockSpec((tm, c), lambda i: (i, 0))

    def const(shape):
        return pl.BlockSpec(shape, lambda i: (0,) * len(shape))

    def lanes(r):
        return pl.BlockSpec((r, tm), lambda i: (0, i))

    def layer_block(w):
        return pl.BlockSpec((1,) + w.shape[1:], lambda i: (layer, 0, 0))

    return pl.pallas_call(
        functools.partial(_merge_kernel, alpha=alpha, n_experts=n_experts, n_a_tiles=n_a_tiles,
                          split=n_a_tiles < n // tm),
        grid=(n // tm,),
        in_specs=[rows(dc), rows(dsg), rows(d), pl.BlockSpec((tm, d), lambda i: (i, 1)),
                  pl.BlockSpec((tm, d), lambda i: (jnp.minimum(i, n_a_tiles - 1), 0)), const((tm, d)),
                  layer_block(wa), layer_block(wb), layer_block(wo), const(g1.shape), const(b1.shape),
                  const(rw.shape), const(rb.shape)],
        out_specs=[rows(d), lanes(SUBLANES), lanes(SUBLANES), lanes(SUBLANES),
                   const((n_experts, LANES))],
        out_shape=[jax.ShapeDtypeStruct((n, d), F32),
                   jax.ShapeDtypeStruct((SUBLANES, n), jnp.int32),
                   jax.ShapeDtypeStruct((SUBLANES, n), F32),
                   jax.ShapeDtypeStruct((SUBLANES, n), jnp.int32),
                   jax.ShapeDtypeStruct((n_experts, LANES), F32)],
        scratch_shapes=[pltpu.VMEM((n_experts, LANES), F32)],
        compiler_params=pltpu.CompilerParams(
            dimension_semantics=("arbitrary",), vmem_limit_bytes=VMEM_LIMIT),
        name="merge_route",
    )(ya, yb, gates, gates, xa, xb, wa, wb, wo, g1, b1, rw, rb)


def _row_gather(src_hbm, dst, sem, index_of, n_rows):
    for r in range(n_rows):
        pltpu.make_async_copy(src_hbm.at[pl.ds(index_of(r), 1)], dst.at[pl.ds(r, 1)], sem).start()


def _row_gather_wait(src_hbm, dst, sem):
    pltpu.make_async_copy(src_hbm.at[pl.ds(0, dst.shape[0])], dst, sem).wait()


def _ffn_kernel(te_ref, nu_ref, ne_ref, t0_ref, t1_ref, ws_ref, tok_ref,
                x1_hbm, wg_hbm, wu_hbm, wd_hbm, ys_ref,
                xbuf, xsem, wgb, wub, wdb, sg, su, sd, wsem, *, layer):
    i = pl.program_id(0)
    n_used = nu_ref[0]
    tm = xbuf.shape[1]
    slot = i % (PREFETCH + 1)

    def part_copies(e, t, s):
        return (pltpu.make_async_copy(wg_hbm.at[layer, e, t], sg.at[s], wsem.at[s]),
                pltpu.make_async_copy(wu_hbm.at[layer, e, t], su.at[s], wsem.at[s]),
                pltpu.make_async_copy(wd_hbm.at[layer, e, t], sd.at[s], wsem.at[s]))

    def start_part(e, t, s):
        for c in part_copies(e, t, s):
            c.start(priority=1)

    def finish_part(e, t, s, w):
        for c in part_copies(e, t, s):
            c.wait()
        rin, rout = sg.shape[1], sd.shape[1]
        wgb[w, pl.ds(pl.multiple_of(t * rin, rin), rin), :] = sg[s].astype(BF16)
        wub[w, pl.ds(pl.multiple_of(t * rin, rin), rin), :] = su[s].astype(BF16)
        wdb[w, pl.ds(pl.multiple_of(t * rout, rout), rout), :] = sd[s].astype(BF16)

    def load_parts(e, t_lo, t_hi, w):
        def step(t, c):
            s = (t - t_lo) % 2
            finish_part(e, t, s, w)

            @pl.when(t + 2 < t_hi)
            def _():
                start_part(e, t + 2, s)
            return c
        lax.fori_loop(t_lo, t_hi, step, 0)

    def start_first_parts(e, t_lo, t_hi):
        @pl.when(t_lo < t_hi)
        def _():
            start_part(e, t_lo, 0)

        @pl.when(t_lo + 1 < t_hi)
        def _():
            start_part(e, t_lo + 1, 1)

    @pl.when(i == 0)
    def _():
        for a in range(PREFETCH):
            tile = jnp.minimum(a, n_used - 1)

            def first(r, c, a=a, tile=tile):
                pltpu.make_async_copy(x1_hbm.at[pl.ds(tok_ref[tile * tm + r], 1)], xbuf.at[a, pl.ds(r, 1)],
                                      xsem.at[a]).start()
                return c
            lax.fori_loop(0, tm, first, 0)
        start_first_parts(te_ref[0], 0, W_PARTS)
        load_parts(te_ref[0], 0, W_PARTS, 0)

    @pl.when(i < n_used)
    def _():
        w = ws_ref[i]
        e_next, t_lo, t_hi = ne_ref[i], t0_ref[i], t1_ref[i]
        start_first_parts(e_next, t_lo, t_hi)
        _row_gather_wait(x1_hbm, xbuf.at[slot], xsem.at[slot])
        x = xbuf[slot].astype(BF16)
        ahead = jnp.minimum(i + PREFETCH, n_used - 1)
        into = (i + PREFETCH) % (PREFETCH + 1)
        _row_gather(x1_hbm, xbuf.at[into], xsem.at[into], lambda r: tok_ref[ahead * tm + r], tm)
        g = jnp.dot(x, wgb[w], preferred_element_type=F32)
        u = jnp.dot(x, wub[w], preferred_element_type=F32)
        h = ((g * jax.nn.sigmoid(g)) * u).astype(BF16)
        ys_ref[...] = jnp.dot(h, wdb[w], preferred_element_type=F32)
        load_parts(e_next, t_lo, t_hi, 1 - w)

    @pl.when(i == n_used - 1)
    def _():
        for a in range(1, PREFETCH + 1):
            s = (i + a) % (PREFETCH + 1)
            _row_gather_wait(x1_hbm, xbuf.at[s], xsem.at[s])

    @pl.when(i >= n_used)
    def _():
        ys_ref[...] = jnp.zeros_like(ys_ref)


def _ffn_call(sched, row_token, x1, w_gate, w_up, w_down, layer):
    d = x1.shape[1]
    p = row_token.shape[0]
    depth, n_e, _, f = w_gate.shape
    tm = TOK_ROWS
    wg = w_gate.reshape(depth, n_e, W_PARTS, d // W_PARTS, f)
    wu = w_up.reshape(depth, n_e, W_PARTS, d // W_PARTS, f)
    wd = w_down.reshape(depth, n_e, W_PARTS, f // W_PARTS, d)
    any_spec = pl.BlockSpec(memory_space=pl.ANY)
    return pl.pallas_call(
        functools.partial(_ffn_kernel, layer=layer),
        grid_spec=pltpu.PrefetchScalarGridSpec(
            num_scalar_prefetch=7,
            grid=(p // tm,),
            in_specs=[any_spec, any_spec, any_spec, any_spec],
            out_specs=pl.BlockSpec((tm, d), lambda i, *_: (i, 0)),
            scratch_shapes=[pltpu.VMEM((PREFETCH + 1, tm, d), F32), pltpu.SemaphoreType.DMA((PREFETCH + 1,)),
                            pltpu.VMEM((2, d, f), BF16), pltpu.VMEM((2, d, f), BF16),
                            pltpu.VMEM((2, f, d), BF16),
                            pltpu.VMEM((2, d // W_PARTS, f), F32), pltpu.VMEM((2, d // W_PARTS, f), F32),
                            pltpu.VMEM((2, f // W_PARTS, d), F32), pltpu.SemaphoreType.DMA((2,))],
        ),
        out_shape=jax.ShapeDtypeStruct((p, d), F32),
        compiler_params=pltpu.CompilerParams(
            dimension_semantics=("arbitrary",), vmem_limit_bytes=VMEM_LIMIT),
        name="expert_ffn",
    )(*sched, row_token, x1, wg, wu, wd)


def _norm2_kernel(pos_ref, x1_ref, gw_ref, g_ref, b_ref, ys_hbm, out_a, out_b, ybuf, sem, *, alpha,
                  n_p_tiles):
    i = pl.program_id(0)
    n_steps = pl.num_programs(0)
    tm = x1_ref.shape[0]
    n = tm * n_steps
    slot = i % (PREFETCH + 1)

    @pl.when(i == 0)
    def _():
        for a in range(PREFETCH):
            tile = jnp.minimum(a, n_steps - 1)

            def first(r, c, a=a, tile=tile):
                for k in range(TOP_K):
                    pltpu.make_async_copy(ys_hbm.at[pl.ds(pos_ref[k * n + tile * tm + r], 1)],
                                          ybuf.at[a, k, pl.ds(r, 1)], sem.at[a]).start()
                return c
            lax.fori_loop(0, tm, first, 0)

    for k in range(TOP_K):
        _row_gather_wait(ys_hbm, ybuf.at[slot, k], sem.at[slot])
    ahead = jnp.minimum(i + PREFETCH, n_steps - 1)
    into = (i + PREFETCH) % (PREFETCH + 1)
    for k in range(TOP_K):
        _row_gather(ys_hbm, ybuf.at[into, k], sem.at[into],
                    lambda r, k=k: pos_ref[k * n + ahead * tm + r], tm)
    y = None
    for k in range(TOP_K):
        term = gw_ref[:, k:k + 1] * ybuf[slot, k]
        y = term if y is None else y + term
    x2 = _ln(alpha * x1_ref[...] + y, g_ref[...], b_ref[...])
    if n_p_tiles is None:
        out_a[...] = x2
        out_b[...] = x2.astype(out_b.dtype)
    else:
        @pl.when(i < n_p_tiles)
        def _():
            out_a[...] = x2

        @pl.when(i >= n_p_tiles)
        def _():
            out_b[...] = x2

    @pl.when(i == n_steps - 1)
    def _():
        for a in range(1, PREFETCH + 1):
            s = (i + a) % (PREFETCH + 1)
            for k in range(TOP_K):
                _row_gather_wait(ys_hbm, ybuf.at[s, k], sem.at[s])


def _norm2_call(pos_flat, x1, gwc, g, b, ys, alpha, n_p=None):
    n, d = x1.shape
    tm = TOK_ROWS
    rows = pl.BlockSpec((tm, d), lambda i, pos: (i, 0))
    const = pl.BlockSpec((1, d), lambda i, pos: (0, 0))
    if n_p is None:
        n_p_tiles = None
        out_specs = [rows, rows]
        out_shape = [jax.ShapeDtypeStruct((n, d), F32), jax.ShapeDtypeStruct((n, d), BF16)]
    else:
        assert n - n_p == tm
        n_p_tiles = n_p // tm
        out_specs = [pl.BlockSpec((tm, d), lambda i, pos: (jnp.minimum(i, n_p_tiles - 1), 0)),
                     pl.BlockSpec((tm, d), lambda i, pos: (0, 0))]
        out_shape = [jax.ShapeDtypeStruct((n_p, d), F32), jax.ShapeDtypeStruct((tm, d), F32)]
    return pl.pallas_call(
        functools.partial(_norm2_kernel, alpha=alpha, n_p_tiles=n_p_tiles),
        grid_spec=pltpu.PrefetchScalarGridSpec(
            num_scalar_prefetch=1,
            grid=(n // tm,),
            in_specs=[rows, pl.BlockSpec((tm, LANES), lambda i, pos: (i, 0)), const, const,
                      pl.BlockSpec(memory_space=pl.ANY)],
            out_specs=out_specs,
            scratch_shapes=[pltpu.VMEM((PREFETCH + 1, TOP_K, tm, d), F32),
                            pltpu.SemaphoreType.DMA((PREFETCH + 1,))],
        ),
        out_shape=out_shape,
        compiler_params=pltpu.CompilerParams(
            dimension_semantics=("arbitrary",), vmem_limit_bytes=VMEM_LIMIT),
        name="combine_norm2",
    )(pos_flat, x1, gwc, g, b, ys)


def _invert_kernel(pos_ref, tok_ref, *, n):
    def clear(p, c):
        tok_ref[p] = 0
        return c
    lax.fori_loop(0, tok_ref.shape[0], clear, 0, unroll=8)

    def put(t, c):
        for k in range(TOP_K):
            tok_ref[pos_ref[k * n + t]] = t
        return c
    lax.fori_loop(0, n, put, 0, unroll=4)


def _invert_call(pos_flat, n, n_rows):
    smem = pl.BlockSpec(memory_space=pltpu.SMEM)
    return pl.pallas_call(
        functools.partial(_invert_kernel, n=n),
        in_specs=[smem],
        out_specs=smem,
        out_shape=jax.ShapeDtypeStruct((n_rows,), jnp.int32),
        name="row_tokens",
    )(pos_flat)


def _cast_kernel(x_ref, o_ref):
    o_ref[...] = x_ref[...].astype(o_ref.dtype)


def _cast_bf16(w):
    cols = w.shape[-1]
    rows = w.size // cols
    br = rows
    while br * cols * 4 > CAST_BLOCK_BYTES and br % 32 == 0:
        br //= 2
    block = pl.BlockSpec((br, cols), lambda i: (i, 0))
    out = pl.pallas_call(
        _cast_kernel,
        grid=(rows // br,),
        in_specs=[block],
        out_specs=block,
        out_shape=jax.ShapeDtypeStruct((rows, cols), BF16),
        compiler_params=pltpu.CompilerParams(
            dimension_semantics=("arbitrary",), vmem_limit_bytes=VMEM_LIMIT),
        name="cast_weights",
    )(w.reshape(rows, cols))
    return out.reshape(w.shape)


def _stack_kernel(a_ref, b_ref, o_ref, *, n_a_tiles):
    i = pl.program_id(0)

    @pl.when(i < n_a_tiles)
    def _():
        o_ref[...] = a_ref[...].astype(o_ref.dtype)

    @pl.when(i >= n_a_tiles)
    def _():
        o_ref[...] = b_ref[...].astype(o_ref.dtype)


def _stack_bf16(a, b):
    tm = b.shape[0]
    d = a.shape[1]
    n_a_tiles = a.shape[0] // tm
    assert a.shape[0] % tm == 0
    return pl.pallas_call(
        functools.partial(_stack_kernel, n_a_tiles=n_a_tiles),
        grid=(n_a_tiles + 1,),
        in_specs=[pl.BlockSpec((tm, d), lambda i: (jnp.minimum(i, n_a_tiles - 1), 0)),
                  pl.BlockSpec((tm, d), lambda i: (0, 0))],
        out_specs=pl.BlockSpec((tm, d), lambda i: (i, 0)),
        out_shape=jax.ShapeDtypeStruct((a.shape[0] + tm, d), BF16),
        compiler_params=pltpu.CompilerParams(
            dimension_semantics=("arbitrary",), vmem_limit_bytes=VMEM_LIMIT),
        name="stack_tokens",
    )(a, b)


def _spatial_tiles(w_s, b_s, seq):
    rep = (jnp.arange(ROW_SUB)[:, None] % seq == jnp.arange(seq)[None, :]).astype(F32)
    exact = lax.Precision.HIGHEST
    wt = jnp.einsum("ta,gab,sb->gts", rep, w_s[:, :seq, :seq], rep, precision=exact)
    bt = jnp.einsum("ta,ga->gt", rep, b_s[:, :seq], precision=exact)
    return wt, jnp.broadcast_to(bt[:, :, None], bt.shape + (LANES,))


def _history_rows(hist):
    return jnp.pad(hist, ((0, 0), (SUBLANES - (CONV_W - 1), 0), (0, 0)))


def _state_rows(cs):
    return cs[:, SUBLANES - (CONV_W - 1):, :]


def kernel(x_prompt, x_sample, state_conv, w_in, b_in, conv_w, conv_b, sgu_ln_g, sgu_ln_b, w_spatial,
           b_spatial, w_a_out, w_b_out, w_o, ln1_g, ln1_b, router_w, router_bias, w_gate, w_up, w_down,
           ln2_g, ln2_b):
    depth, d, n_col = w_in.shape
    batch, seq, _ = x_prompt.shape
    dec_batch, dec_seq, _ = x_sample.shape
    dc = conv_w.shape[2]
    dsg = sgu_ln_g.shape[1]
    n_experts = router_w.shape[1]
    assert batch == 1 and n_col == 3 * dc + 2 * dsg + 2 * d
    assert dc == COL_BLOCK and dsg == COL_BLOCK and d % COL_BLOCK == 0
    assert dsg == SGU_GROUPS * LANES and n_experts % N_EXPERT_GROUPS == 0
    n_p, n_s = batch * seq, dec_batch * dec_seq
    n = n_p + n_s
    tm_p = min(MIX_ROWS, n_p)
    assert n_p % tm_p == 0 and n_s == ROW_SUB and ROW_SUB % dec_seq == 0 and n_p % TOK_ROWS == 0
    assert seq % SGU_CHUNK == 0 and SGU_CHUNK % dec_seq == 0
    alpha = (2.0 * depth) ** 0.25
    cfg = MixCfg(tm_p, n_p // tm_p, seq, dec_seq, n_s, d // COL_BLOCK)

    n_pairs = TOP_K * n
    n_tiles = n_pairs // TOK_ROWS + n_experts
    assert n_pairs % TOK_ROWS == 0

    x, x_tail = x_prompt.reshape(n_p, d), x_sample.reshape(n_s, d)
    xb = _stack_bf16(x, x_tail)
    rw = jnp.pad(router_w, ((0, 0), (0, LANES - n_experts))).astype(BF16)
    rb = jnp.broadcast_to(router_bias[:, None], (n_experts, TOK_ROWS))
    zero_hist = jnp.zeros((batch, CONV_W - 1, dc), F32)

    w_in_b, w_a_b, w_b_b, w_o_b = (_cast_bf16(w) for w in (w_in, w_a_out, w_b_out, w_o))

    conv_p, conv_s, v_s = [], [], []
    for l in range(depth):
        bias = b_in[l][None, :]
        cw = jnp.concatenate([conv_w[l], conv_b[l][None, :],
                              jnp.zeros((SUBLANES - CONV_W - 1, dc), F32)], axis=0)
        lng, lnb = sgu_ln_g[l][None, :], sgu_ln_b[l][None, :]
        wt_p, bt_p = _spatial_tiles(w_spatial[l], b_spatial[l], min(seq, SGU_CHUNK))
        wt_s, bt_s = _spatial_tiles(w_spatial[l], b_spatial[l], min(dec_seq, SGU_CHUNK))

        ya, yb, gates, cs_p, cs_s, v_rows = _mixer_call(
            xb, w_in_b, l, bias, _history_rows(zero_hist), _history_rows(state_conv[l]), cw, lng, lnb,
            wt_p, bt_p, wt_s, bt_s, cfg)
        conv_p.append(_state_rows(cs_p))
        conv_s.append(_state_rows(cs_s))
        v_s.append(v_rows.reshape(dec_batch, dec_seq, dsg))

        x1, eidx, gw, rank, cnt = _merge_call(
            ya, yb, gates, x, x_tail, w_a_b, w_b_b, w_o_b, l,
            ln1_g[l][None, :], ln1_b[l][None, :], rw, rb, alpha, n_experts)

        counts = cnt[:, 0].astype(jnp.int32)
        padded = ((counts + TOK_ROWS - 1) // TOK_ROWS) * TOK_ROWS
        seg_end = jnp.cumsum(padded)
        seg_start = seg_end - padded
        n_used = (seg_end[-1:] // TOK_ROWS).astype(jnp.int32)
        tile_row0 = jnp.arange(n_tiles, dtype=jnp.int32) * TOK_ROWS
        tile_expert = jnp.minimum(
            jnp.sum(tile_row0[:, None] >= seg_end[None, :], axis=1), n_experts - 1).astype(jnp.int32)
        e_ids = jnp.arange(n_experts, dtype=jnp.int32)
        pos = rank[:TOP_K] + jnp.sum(
            jnp.where(eidx[:TOP_K, :, None] == e_ids, seg_start.astype(jnp.int32), 0), axis=-1)

        n_t = (padded // TOK_ROWS).astype(jnp.int32)
        later = jnp.where(n_t > 0, e_ids, n_experts)
        nxt_e = jnp.concatenate([jnp.flip(lax.cummin(jnp.flip(later)))[1:],
                                 jnp.full((1,), n_experts, jnp.int32)])
        group = (jnp.cumsum(n_t > 0) - 1).astype(jnp.int32)
        tile_i = jnp.arange(n_tiles, dtype=jnp.int32)
        of_tile = tile_expert[:, None] == e_ids

        def lookup(table):
            return jnp.sum(jnp.where(of_tile, table.astype(jnp.int32), 0), axis=1)

        q = tile_i - lookup(seg_start // TOK_ROWS)
        g_t = jnp.maximum(lookup(n_t), 1)
        nxt_t = lookup(nxt_e)
        loads = jnp.logical_and(tile_i < n_used[0], nxt_t < n_experts)
        t_lo = jnp.where(loads, (W_PARTS * q) // g_t, 0).astype(jnp.int32)
        t_hi = jnp.where(loads, (W_PARTS * (q + 1)) // g_t, 0).astype(jnp.int32)
        sched = (tile_expert, n_used, jnp.minimum(nxt_t, n_experts - 1).astype(jnp.int32),
                 t_lo, t_hi, (lookup(group) % 2).astype(jnp.int32))

        pos_flat = pos.reshape(-1)
        row_token = _invert_call(pos_flat, n, n_tiles * TOK_ROWS)
        ys = _ffn_call(sched, row_token, x1, w_gate, w_up, w_down, l)
        gwc = jnp.pad(gw[:TOP_K].T, ((0, 0), (0, LANES - TOP_K)))
        x, xb = _norm2_call(pos_flat, x1, gwc, ln2_g[l][None, :], ln2_b[l][None, :], ys, alpha,
                            n_p if l == depth - 1 else None)
        x_tail = x[:TOK_ROWS]

    return (x.reshape(batch, seq, d), xb.reshape(dec_batch, dec_seq, d),
            jnp.stack(conv_p), jnp.stack(conv_s), jnp.stack(v_s))
```

```python
import functools
from typing import NamedTuple

import jax
import jax.numpy as jnp
from jax import lax
from jax.experimental import pallas as pl
from jax.experimental.pallas import tpu as pltpu

CHUNK = 64
SGU_CHUNK = 128
SGU_GROUPS = 8
N_EXPERT_GROUPS = 4
TOP_K = 2
LN_EPS = 1e-5
CONV_W = 3

LANES = 128
SUBLANES = 8
COL_BLOCK = 1024
ROW_SUB = 256
MIX_ROWS = 1024
TOK_ROWS = 256
W_PARTS = 8
PREFETCH = 2
CAST_BLOCK_BYTES = 10 * 1024 * 1024
VMEM_LIMIT = 56 * 1024 * 1024

F32 = jnp.float32
BF16 = jnp.bfloat16


class MixCfg(NamedTuple):
    tm: int
    n_p_tiles: int
    seq: int
    dec_seq: int
    n_s: int
    ng: int


def _ln(z, g, b):
    mu = jnp.mean(z, axis=-1, keepdims=True)
    d = z - mu
    var = jnp.mean(d * d, axis=-1, keepdims=True)
    return d * lax.rsqrt(var + LN_EPS) * g + b


def _conv_rows(hin, p2, p1, cw_ref):
    rows = lax.broadcasted_iota(jnp.int32, hin.shape, 0)
    r1 = pltpu.roll(hin, 1, 0)
    r2 = pltpu.roll(hin, 2, 0)
    sh1 = jnp.where(rows == 0, p1, r1)
    sh2 = jnp.where(rows == 0, p2, jnp.where(rows == 1, p1, r2))
    out = cw_ref[3:4, :] + cw_ref[0:1, :] * sh2
    out = out + cw_ref[1:2, :] * sh1
    return out + cw_ref[2:3, :] * hin


def _spatial_keep(seq):
    t = lax.broadcasted_iota(jnp.int32, (ROW_SUB, ROW_SUB), 0)
    s = lax.broadcasted_iota(jnp.int32, (ROW_SUB, ROW_SUB), 1)
    same = (t // seq) == (s // seq)
    causal = ((s % seq) // CHUNK) <= ((t % seq) // CHUNK)
    return jnp.logical_and(same, causal)


BLK_B, BLK_C, BLK_H, BLK_U, BLK_V, BLK_GATES = 0, 1, 2, 3, 4, 5
STEP_H, STEP_C, STEP_B, STEP_V, STEP_U, STEP_GATES = 0, 1, 2, 3, 4, 5


def _mixer_block(j):
    return jnp.where(j < STEP_V, BLK_H - j, jnp.where(j < STEP_GATES, BLK_U + BLK_V - j, j))


def _mixer_kernel(x_ref, w_ref, bias_ref, prevp_ref, prevs_ref, cw_ref, lng_ref, lnb_ref,
                  wtp_ref, btp_ref, wts_ref, bts_ref,
                  ya_ref, yb_ref, gates_ref, csp_ref, css_ref, v_ref,
                  hsc, ssc, vsc, carry, wm_sc, bt_sc, *, cfg: MixCfg):
    i = pl.program_id(0)
    j = pl.program_id(1)
    is_prompt = i < cfg.n_p_tiles
    is_sample = i == cfg.n_p_tiles

    def sub_rows(r):
        return pl.ds(r * ROW_SUB, ROW_SUB)

    def proj(r):
        rows = sub_rows(r)
        acc = jnp.dot(x_ref[rows, :], w_ref[0], preferred_element_type=F32)
        return rows, acc + bias_ref[...]

    def for_sub(body):
        @pl.when(is_prompt)
        def _():
            for r in range(cfg.tm // ROW_SUB):
                body(r, False)

        @pl.when(is_sample)
        def _():
            for r in range(cfg.n_s // ROW_SUB):
                body(r, True)

    @pl.when(jnp.logical_and(i == 0, j == 0))
    def _():
        carry[...] = prevp_ref[0]

    @pl.when(j == STEP_H)
    def _():
        def body(r, sample):
            rows, a = proj(r)
            hsc[rows, :] = a
        for_sub(body)

    @pl.when(j == STEP_C)
    def _():
        def body(r, sample):
            rows, a = proj(r)
            hin = a * hsc[rows, :]
            if not sample:
                hsc[rows, :] = _conv_rows(hin, carry[6:7, :], carry[7:8, :], cw_ref)
                carry[...] = hin[ROW_SUB - SUBLANES:, :]
                csp_ref[0] = hin[ROW_SUB - SUBLANES:, :]
            else:
                sr = cfg.dec_seq
                for s in range(ROW_SUB // sr):
                    hs = hin[s * sr:(s + 1) * sr, :]
                    st = r * (ROW_SUB // sr) + s
                    srow = pl.ds(r * ROW_SUB + s * sr, sr)
                    hsc[srow, :] = _conv_rows(hs, prevs_ref[st, 6:7, :], prevs_ref[st, 7:8, :], cw_ref)
                    css_ref[st] = hs[sr - SUBLANES:, :]
        for_sub(body)

    @pl.when(j == STEP_B)
    def _():
        def body(r, sample):
            rows, a = proj(r)
            ya_ref[rows, :] = (a * hsc[rows, :]).astype(ya_ref.dtype)
        for_sub(body)

    @pl.when(j == STEP_V)
    def _():
        @pl.when(is_prompt)
        def _():
            keep = _spatial_keep(min(cfg.seq, SGU_CHUNK))
            for g in range(SGU_GROUPS):
                wm_sc[g] = jnp.where(keep, wtp_ref[g], 0.0).astype(wm_sc.dtype)
            bt_sc[...] = btp_ref[...]

        @pl.when(is_sample)
        def _():
            keep = _spatial_keep(min(cfg.dec_seq, SGU_CHUNK))
            for g in range(SGU_GROUPS):
                wm_sc[g] = jnp.where(keep, wts_ref[g], 0.0).astype(wm_sc.dtype)
            bt_sc[...] = bts_ref[...]

        def spatial(n_sub):
            for g in range(SGU_GROUPS):
                wide = jnp.dot(wm_sc[g], vsc[:, g * n_sub * LANES:(g + 1) * n_sub * LANES],
                               preferred_element_type=F32)
                for r in range(n_sub):
                    ssc[sub_rows(r), g * LANES:(g + 1) * LANES] = wide[:, r * LANES:(r + 1) * LANES] + bt_sc[g]

        def body(r, sample):
            n_sub = (cfg.n_s if sample else cfg.tm) // ROW_SUB
            rows, a = proj(r)
            vn = _ln(jax.nn.gelu(a), lng_ref[...], lnb_ref[...])
            if sample:
                v_ref[rows, :] = vn
            vnb = vn.astype(BF16)
            for g in range(SGU_GROUPS):
                col = (g * n_sub + r) * LANES
                vsc[:, col:col + LANES] = vnb[:, g * LANES:(g + 1) * LANES]
            if r == n_sub - 1:
                spatial(n_sub)
        for_sub(body)

    @pl.when(j == STEP_U)
    def _():
        def body(r, sample):
            rows, a = proj(r)
            yb_ref[rows, :] = (jax.nn.gelu(a) * ssc[rows, :]).astype(yb_ref.dtype)
        for_sub(body)

    @pl.when(j >= STEP_GATES)
    def _():
        def body(r, sample):
            rows, a = proj(r)
            gates_ref[rows, :] = jax.nn.sigmoid(a)
        for_sub(body)


def _mixer_call(xb, w_in, layer, b_in, prev_p, prev_s, cw, lng, lnb, wt_p, bt_p, wt_s, bt_s, cfg):
    n, d = xb.shape
    cb = COL_BLOCK
    n_j = w_in.shape[2] // cb
    n_i = cfg.n_p_tiles + 1
    n_gate_blocks = 2 * cfg.ng

    def const(shape):
        return pl.BlockSpec(shape, lambda i, j: (0,) * len(shape))

    in_specs = [
        pl.BlockSpec((cfg.tm, d), lambda i, j: (i, 0)),
        pl.BlockSpec((1, d, cb), lambda i, j: (layer, 0, _mixer_block(j))),
        pl.BlockSpec((1, cb), lambda i, j: (0, _mixer_block(j))),
        const(prev_p.shape), const(prev_s.shape), const(cw.shape), const(lng.shape), const(lnb.shape),
        const(wt_p.shape), const(bt_p.shape), const(wt_s.shape), const(bt_s.shape),
    ]
    out_shape = [
        jax.ShapeDtypeStruct((n, cb), BF16),
        jax.ShapeDtypeStruct((n, cb), BF16),
        jax.ShapeDtypeStruct((n, n_gate_blocks * cb), F32),
        jax.ShapeDtypeStruct(prev_p.shape, F32),
        jax.ShapeDtypeStruct(prev_s.shape, F32),
        jax.ShapeDtypeStruct((cfg.n_s, cb), F32),
    ]
    out_specs = [
        pl.BlockSpec((cfg.tm, cb), lambda i, j: (i, 0)),
        pl.BlockSpec((cfg.tm, cb), lambda i, j: (i, 0)),
        pl.BlockSpec((cfg.tm, cb), lambda i, j: (i, jnp.clip(j - STEP_GATES, 0, n_gate_blocks - 1))),
        const(prev_p.shape), const(prev_s.shape), const((cfg.n_s, cb)),
    ]
    scratch = [
        pltpu.VMEM((cfg.tm, cb), F32),
        pltpu.VMEM((cfg.tm, cb), F32),
        pltpu.VMEM((ROW_SUB, (cfg.tm // ROW_SUB) * cb), BF16),
        pltpu.VMEM((SUBLANES, cb), F32),
        pltpu.VMEM((SGU_GROUPS, ROW_SUB, ROW_SUB), BF16),
        pltpu.VMEM((SGU_GROUPS, ROW_SUB, LANES), F32),
    ]
    assert n_j == BLK_GATES + n_gate_blocks
    return pl.pallas_call(
        functools.partial(_mixer_kernel, cfg=cfg),
        grid=(n_i, n_j),
        in_specs=in_specs,
        out_specs=out_specs,
        out_shape=out_shape,
        scratch_shapes=scratch,
        compiler_params=pltpu.CompilerParams(
            dimension_semantics=("arbitrary", "arbitrary"), vmem_limit_bytes=VMEM_LIMIT),
        name="mixer",
    )(xb, w_in, b_in, prev_p, prev_s, cw, lng, lnb, wt_p, bt_p, wt_s, bt_s)


def _merge_kernel(ya_ref, yb_ref, ga_ref, gb_ref, xa_ref, xb_ref, wa_ref, wb_ref, wo_ref, g1_ref, b1_ref,
                  rw_ref, rb_ref, x1_ref, eidx_ref, gw_ref, rank_ref, cnt_ref, run, *,
                  alpha, n_experts, n_a_tiles, split):
    i = pl.program_id(0)
    tm = xa_ref.shape[0]
    epg = n_experts // N_EXPERT_GROUPS

    @pl.when(i == 0)
    def _():
        run[...] = jnp.zeros_like(run)

    a = jnp.dot(ya_ref[...], wa_ref[0], preferred_element_type=F32)
    b = jnp.dot(yb_ref[...], wb_ref[0], preferred_element_type=F32)
    merged = ga_ref[...] * a + gb_ref[...] * b
    x = xa_ref[...]
    if split:
        x = jnp.where(i < n_a_tiles, x, xb_ref[...])
    z = alpha * x + jnp.dot(merged.astype(BF16), wo_ref[0], preferred_element_type=F32)
    x1 = _ln(z, g1_ref[...], b1_ref[...])
    x1_ref[...] = x1

    logits = jnp.dot(x1.astype(BF16), rw_ref[...], preferred_element_type=F32)
    lt = logits.T[:n_experts, :]
    ex = jnp.exp(lt - jnp.max(lt, axis=0, keepdims=True))
    sc = ex / jnp.sum(ex, axis=0, keepdims=True)
    sel = sc + rb_ref[...]
    sel_r = [sel[e:e + 1, :] for e in range(n_experts)]
    sc_r = [sc[e:e + 1, :] for e in range(n_experts)]

    def pair_max(v):
        best = None
        for p in range(len(v)):
            for q in range(p + 1, len(v)):
                s = v[p] + v[q]
                best = s if best is None else jnp.maximum(best, s)
        return best

    grp = [pair_max(sel_r[g * epg:(g + 1) * epg]) for g in range(N_EXPERT_GROUPS)]
    gi = jnp.zeros((1, tm), jnp.int32)
    gbest = grp[0]
    for g in range(1, N_EXPERT_GROUPS):
        better = grp[g] > gbest
        gi = jnp.where(better, g, gi)
        gbest = jnp.where(better, grp[g], gbest)

    def pick(rows_, idx):
        out = rows_[-1]
        for c in range(len(rows_) - 2, -1, -1):
            out = jnp.where(idx == c, rows_[c], out)
        return out

    vk = [pick([sel_r[g * epg + q] for g in range(N_EXPERT_GROUPS)], gi) for q in range(epg)]
    pk = [pick([sc_r[g * epg + q] for g in range(N_EXPERT_GROUPS)], gi) for q in range(epg)]

    i1 = jnp.zeros((1, tm), jnp.int32)
    b1 = vk[0]
    for q in range(1, epg):
        better = vk[q] > b1
        i1 = jnp.where(better, q, i1)
        b1 = jnp.where(better, vk[q], b1)
    i2 = jnp.zeros((1, tm), jnp.int32)
    b2 = jnp.full((1, tm), -jnp.inf, F32)
    for q in range(epg):
        cand = jnp.logical_and(i1 != q, vk[q] > b2)
        i2 = jnp.where(cand, q, i2)
        b2 = jnp.where(cand, vk[q], b2)
    p1 = pick(pk, i1)
    p2 = pick(pk, i2)
    den = p1 + p2
    e1 = gi * epg + i1
    e2 = gi * epg + i2

    eio = lax.broadcasted_iota(jnp.int32, (n_experts, tm), 0)
    oh1 = (eio == e1).astype(F32)
    oh2 = (eio == e2).astype(F32)
    ts = lax.broadcasted_iota(jnp.int32, (tm, tm), 0)
    tt = lax.broadcasted_iota(jnp.int32, (tm, tm), 1)
    upper = (ts <= tt).astype(BF16)
    inc1 = jnp.dot(oh1.astype(BF16), upper, preferred_element_type=F32)
    inc2 = jnp.dot(oh2.astype(BF16), upper, preferred_element_type=F32)
    tot1 = jnp.sum(oh1, axis=1, keepdims=True)
    tot2 = jnp.sum(oh2, axis=1, keepdims=True)
    base = run[:, 0:1]
    r1 = jnp.sum(oh1 * (inc1 - 1.0 + base), axis=0, keepdims=True)
    r2 = jnp.sum(oh2 * (inc2 - 1.0 + base + tot1), axis=0, keepdims=True)
    new_run = run[...] + tot1 + tot2
    run[...] = new_run
    cnt_ref[...] = new_run

    eidx_ref[...] = jnp.zeros_like(eidx_ref)
    eidx_ref[0:1, :] = e1
    eidx_ref[1:2, :] = e2
    gw_ref[...] = jnp.zeros_like(gw_ref)
    gw_ref[0:1, :] = p1 / den
    gw_ref[1:2, :] = p2 / den
    rank_ref[...] = jnp.zeros_like(rank_ref)
    rank_ref[0:1, :] = r1.astype(jnp.int32)
    rank_ref[1:2, :] = r2.astype(jnp.int32)


def _merge_call(ya, yb, gates, xa, xb, wa, wb, wo, layer, g1, b1, rw, rb, alpha, n_experts):
    n, d = ya.shape[0], xa.shape[1]
    tm = TOK_ROWS
    dc, dsg = ya.shape[1], yb.shape[1]
    n_a_tiles = xa.shape[0] // tm
    assert xa.shape[0] % tm == 0 and xb.shape[0] == tm and n_a_tiles in (n // tm, n // tm - 1)

    def rows(c):
        return pl.BlockSpec((tm, c), lambda i: (i, 0))

    def const(shape):
        return pl.BlockSpec(shape, lambda i: (0,) * len(shape))

    def lanes(r):
        return pl.BlockSpec((r, tm), lambda i: (0, i))

    def layer_block(w):
        return pl.BlockSpec((1,) + w.shape[1:], lambda i: (layer, 0, 0))

    return pl.pallas_call(
        functools.partial(_merge_kernel, alpha=alpha, n_experts=n_experts, n_a_tiles=n_a_tiles,
                          split=n_a_tiles < n // tm),
        grid=(n // tm,),
        in_specs=[rows(dc), rows(dsg), rows(d), pl.BlockSpec((tm, d), lambda i: (i, 1)),
                  pl.BlockSpec((tm, d), lambda i: (jnp.minimum(i, n_a_tiles - 1), 0)), const((tm, d)),
                  layer_block(wa), layer_block(wb), layer_block(wo), const(g1.shape), const(b1.shape),
                  const(rw.shape), const(rb.shape)],
        out_specs=[rows(d), lanes(SUBLANES), lanes(SUBLANES), lanes(SUBLANES),
                   const((n_experts, LANES))],
        out_shape=[jax.ShapeDtypeStruct((n, d), F32),
                   jax.ShapeDtypeStruct((SUBLANES, n), jnp.int32),
                   jax.ShapeDtypeStruct((SUBLANES, n), F32),
                   jax.ShapeDtypeStruct((SUBLANES, n), jnp.int32),
                   jax.ShapeDtypeStruct((n_experts, LANES), F32)],
        scratch_shapes=[pltpu.VMEM((n_experts, LANES), F32)],
        compiler_params=pltpu.CompilerParams(
            dimension_semantics=("arbitrary",), vmem_limit_bytes=VMEM_LIMIT),
        name="merge_route",
    )(ya, yb, gates, gates, xa, xb, wa, wb, wo, g1, b1, rw, rb)


def _row_gather(src_hbm, dst, sem, index_of, n_rows):
    for r in range(n_rows):
        pltpu.make_async_copy(src_hbm.at[pl.ds(index_of(r), 1)], dst.at[pl.ds(r, 1)], sem).start()


def _row_gather_wait(src_hbm, dst, sem):
    pltpu.make_async_copy(src_hbm.at[pl.ds(0, dst.shape[0])], dst, sem).wait()


def _ffn_kernel(te_ref, nu_ref, ne_ref, t0_ref, t1_ref, ws_ref, tok_ref,
                x1_hbm, wg_hbm, wu_hbm, wd_hbm, ys_ref,
                xbuf, xsem, wgb, wub, wdb, sg, su, sd, wsem, *, layer):
    i = pl.program_id(0)
    n_used = nu_ref[0]
    tm = xbuf.shape[1]
    slot = i % (PREFETCH + 1)

    def part_copies(e, t, s):
        return (pltpu.make_async_copy(wg_hbm.at[layer, e, t], sg.at[s], wsem.at[s]),
                pltpu.make_async_copy(wu_hbm.at[layer, e, t], su.at[s], wsem.at[s]),
                pltpu.make_async_copy(wd_hbm.at[layer, e, t], sd.at[s], wsem.at[s]))

    def start_part(e, t, s):
        for c in part_copies(e, t, s):
            c.start(priority=1)

    def finish_part(e, t, s, w):
        for c in part_copies(e, t, s):
            c.wait()
        rin, rout = sg.shape[1], sd.shape[1]
        wgb[w, pl.ds(pl.multiple_of(t * rin, rin), rin), :] = sg[s].astype(BF16)
        wub[w, pl.ds(pl.multiple_of(t * rin, rin), rin), :] = su[s].astype(BF16)
        wdb[w, pl.ds(pl.multiple_of(t * rout, rout), rout), :] = sd[s].astype(BF16)

    def load_parts(e, t_lo, t_hi, w):
        def step(t, c):
            s = (t - t_lo) % 2
            finish_part(e, t, s, w)

            @pl.when(t + 2 < t_hi)
            def _():
                start_part(e, t + 2, s)
            return c
        lax.fori_loop(t_lo, t_hi, step, 0)

    def start_first_parts(e, t_lo, t_hi):
        @pl.when(t_lo < t_hi)
        def _():
            start_part(e, t_lo, 0)

        @pl.when(t_lo + 1 < t_hi)
        def _():
            start_part(e, t_lo + 1, 1)

    @pl.when(i == 0)
    def _():
        for a in range(PREFETCH):
            tile = jnp.minimum(a, n_used - 1)

            def first(r, c, a=a, tile=tile):
                pltpu.make_async_copy(x1_hbm.at[pl.ds(tok_ref[tile * tm + r], 1)], xbuf.at[a, pl.ds(r, 1)],
                                      xsem.at[a]).start()
                return c
            lax.fori_loop(0, tm, first, 0)
        start_first_parts(te_ref[0], 0, W_PARTS)
        load_parts(te_ref[0], 0, W_PARTS, 0)

    @pl.when(i < n_used)
    def _():
        w = ws_ref[i]
        e_next, t_lo, t_hi = ne_ref[i], t0_ref[i], t1_ref[i]
        start_first_parts(e_next, t_lo, t_hi)
        _row_gather_wait(x1_hbm, xbuf.at[slot], xsem.at[slot])
        x = xbuf[slot].astype(BF16)
        ahead = jnp.minimum(i + PREFETCH, n_used - 1)
        into = (i + PREFETCH) % (PREFETCH + 1)
        _row_gather(x1_hbm, xbuf.at[into], xsem.at[into], lambda r: tok_ref[ahead * tm + r], tm)
        g = jnp.dot(x, wgb[w], preferred_element_type=F32)
        u = jnp.dot(x, wub[w], preferred_element_type=F32)
        h = ((g * jax.nn.sigmoid(g)) * u).astype(BF16)
        ys_ref[...] = jnp.dot(h, wdb[w], preferred_element_type=F32)
        load_parts(e_next, t_lo, t_hi, 1 - w)

    @pl.when(i == n_used - 1)
    def _():
        for a in range(1, PREFETCH + 1):
            s = (i + a) % (PREFETCH + 1)
            _row_gather_wait(x1_hbm, xbuf.at[s], xsem.at[s])

    @pl.when(i >= n_used)
    def _():
        ys_ref[...] = jnp.zeros_like(ys_ref)


def _ffn_call(sched, row_token, x1, w_gate, w_up, w_down, layer):
    d = x1.shape[1]
    p = row_token.shape[0]
    depth, n_e, _, f = w_gate.shape
    tm = TOK_ROWS
    wg = w_gate.reshape(depth, n_e, W_PARTS, d // W_PARTS, f)
    wu = w_up.reshape(depth, n_e, W_PARTS, d // W_PARTS, f)
    wd = w_down.reshape(depth, n_e, W_PARTS, f // W_PARTS, d)
    any_spec = pl.BlockSpec(memory_space=pl.ANY)
    return pl.pallas_call(
        functools.partial(_ffn_kernel, layer=layer),
        grid_spec=pltpu.PrefetchScalarGridSpec(
            num_scalar_prefetch=7,
            grid=(p // tm,),
            in_specs=[any_spec, any_spec, any_spec, any_spec],
            out_specs=pl.BlockSpec((tm, d), lambda i, *_: (i, 0)),
            scratch_shapes=[pltpu.VMEM((PREFETCH + 1, tm, d), F32), pltpu.SemaphoreType.DMA((PREFETCH + 1,)),
                            pltpu.VMEM((2, d, f), BF16), pltpu.VMEM((2, d, f), BF16),
                            pltpu.VMEM((2, f, d), BF16),
                            pltpu.VMEM((2, d // W_PARTS, f), F32), pltpu.VMEM((2, d // W_PARTS, f), F32),
                            pltpu.VMEM((2, f // W_PARTS, d), F32), pltpu.SemaphoreType.DMA((2,))],
        ),
        out_shape=jax.ShapeDtypeStruct((p, d), F32),
        compiler_params=pltpu.CompilerParams(
            dimension_semantics=("arbitrary",), vmem_limit_bytes=VMEM_LIMIT),
        name="expert_ffn",
    )(*sched, row_token, x1, wg, wu, wd)


def _norm2_kernel(pos_ref, x1_ref, gw_ref, g_ref, b_ref, ys_hbm, out_a, out_b, ybuf, sem, *, alpha,
                  n_p_tiles):
    i = pl.program_id(0)
    n_steps = pl.num_programs(0)
    tm = x1_ref.shape[0]
    n = tm * n_steps
    slot = i % (PREFETCH + 1)

    @pl.when(i == 0)
    def _():
        for a in range(PREFETCH):
            tile = jnp.minimum(a, n_steps - 1)

            def first(r, c, a=a, tile=tile):
                for k in range(TOP_K):
                    pltpu.make_async_copy(ys_hbm.at[pl.ds(pos_ref[k * n + tile * tm + r], 1)],
                                          ybuf.at[a, k, pl.ds(r, 1)], sem.at[a]).start()
                return c
            lax.fori_loop(0, tm, first, 0)

    for k in range(TOP_K):
        _row_gather_wait(ys_hbm, ybuf.at[slot, k], sem.at[slot])
    ahead = jnp.minimum(i + PREFETCH, n_steps - 1)
    into = (i + PREFETCH) % (PREFETCH + 1)
    for k in range(TOP_K):
        _row_gather(ys_hbm, ybuf.at[into, k], sem.at[into],
                    lambda r, k=k: pos_ref[k * n + ahead * tm + r], tm)
    y = None
    for k in range(TOP_K):
        term = gw_ref[:, k:k + 1] * ybuf[slot, k]
        y = term if y is None else y + term
    x2 = _ln(alpha * x1_ref[...] + y, g_ref[...], b_ref[...])
    if n_p_tiles is None:
        out_a[...] = x2
        out_b[...] = x2.astype(out_b.dtype)
    else:
        @pl.when(i < n_p_tiles)
        def _():
            out_a[...] = x2

        @pl.when(i >= n_p_tiles)
        def _():
            out_b[...] = x2

    @pl.when(i == n_steps - 1)
    def _():
        for a in range(1, PREFETCH + 1):
            s = (i + a) % (PREFETCH + 1)
            for k in range(TOP_K):
                _row_gather_wait(ys_hbm, ybuf.at[s, k], sem.at[s])


def _norm2_call(pos_flat, x1, gwc, g, b, ys, alpha, n_p=None):
    n, d = x1.shape
    tm = TOK_ROWS
    rows = pl.BlockSpec((tm, d), lambda i, pos: (i, 0))
    const = pl.BlockSpec((1, d), lambda i, pos: (0, 0))
    if n_p is None:
        n_p_tiles = None
        out_specs = [rows, rows]
        out_shape = [jax.ShapeDtypeStruct((n, d), F32), jax.ShapeDtypeStruct((n, d), BF16)]
    else:
        assert n - n_p == tm
        n_p_tiles = n_p // tm
        out_specs = [pl.BlockSpec((tm, d), lambda i, pos: (jnp.minimum(i, n_p_tiles - 1), 0)),
                     pl.BlockSpec((tm, d), lambda i, pos: (0, 0))]
        out_shape = [jax.ShapeDtypeStruct((n_p, d), F32), jax.ShapeDtypeStruct((tm, d), F32)]
    return pl.pallas_call(
        functools.partial(_norm2_kernel, alpha=alpha, n_p_tiles=n_p_tiles),
        grid_spec=pltpu.PrefetchScalarGridSpec(
            num_scalar_prefetch=1,
            grid=(n // tm,),
            in_specs=[rows, pl.BlockSpec((tm, LANES), lambda i, pos: (i, 0)), const, const,
                      pl.BlockSpec(memory_space=pl.ANY)],
            out_specs=out_specs,
            scratch_shapes=[pltpu.VMEM((PREFETCH + 1, TOP_K, tm, d), F32),
                            pltpu.SemaphoreType.DMA((PREFETCH + 1,))],
        ),
        out_shape=out_shape,
        compiler_params=pltpu.CompilerParams(
            dimension_semantics=("arbitrary",), vmem_limit_bytes=VMEM_LIMIT),
        name="combine_norm2",
    )(pos_flat, x1, gwc, g, b, ys)


def _invert_kernel(pos_ref, zeros_hbm, tok_ref, sem, *, n):
    clear = pltpu.make_async_copy(zeros_hbm, tok_ref, sem)
    clear.start()
    clear.wait()

    def put(t, c):
        for k in range(TOP_K):
            tok_ref[pos_ref[k * n + t]] = t
        return c
    lax.fori_loop(0, n, put, 0, unroll=16)


def _invert_call(pos_flat, n, n_rows):
    smem = pl.BlockSpec(memory_space=pltpu.SMEM)
    return pl.pallas_call(
        functools.partial(_invert_kernel, n=n),
        in_specs=[smem, pl.BlockSpec(memory_space=pl.ANY)],
        out_specs=smem,
        out_shape=jax.ShapeDtypeStruct((n_rows,), jnp.int32),
        scratch_shapes=[pltpu.SemaphoreType.DMA(())],
        name="row_tokens",
    )(pos_flat, jnp.zeros((n_rows,), jnp.int32))


def _cast_kernel(x_ref, o_ref):
    o_ref[...] = x_ref[...].astype(o_ref.dtype)


def _cast_bf16(w):
    cols = w.shape[-1]
    rows = w.size // cols
    br = rows
    while br * cols * 4 > CAST_BLOCK_BYTES and br % 32 == 0:
        br //= 2
    block = pl.BlockSpec((br, cols), lambda i: (i, 0))
    out = pl.pallas_call(
        _cast_kernel,
        grid=(rows // br,),
        in_specs=[block],
        out_specs=block,
        out_shape=jax.ShapeDtypeStruct((rows, cols), BF16),
        compiler_params=pltpu.CompilerParams(
            dimension_semantics=("arbitrary",), vmem_limit_bytes=VMEM_LIMIT),
        name="cast_weights",
    )(w.reshape(rows, cols))
    return out.reshape(w.shape)


def _stack_kernel(a_ref, b_ref, o_ref, *, n_a_tiles):
    i = pl.program_id(0)

    @pl.when(i < n_a_tiles)
    def _():
        o_ref[...] = a_ref[...].astype(o_ref.dtype)

    @pl.when(i >= n_a_tiles)
    def _():
        o_ref[...] = b_ref[...].astype(o_ref.dtype)


def _stack_bf16(a, b):
    tm = b.shape[0]
    d = a.shape[1]
    n_a_tiles = a.shape[0] // tm
    assert a.shape[0] % tm == 0
    return pl.pallas_call(
        functools.partial(_stack_kernel, n_a_tiles=n_a_tiles),
        grid=(n_a_tiles + 1,),
        in_specs=[pl.BlockSpec((tm, d), lambda i: (jnp.minimum(i, n_a_tiles - 1), 0)),
                  pl.BlockSpec((tm, d), lambda i: (0, 0))],
        out_specs=pl.BlockSpec((tm, d), lambda i: (i, 0)),
        out_shape=jax.ShapeDtypeStruct((a.shape[0] + tm, d), BF16),
        compiler_params=pltpu.CompilerParams(
            dimension_semantics=("arbitrary",), vmem_limit_bytes=VMEM_LIMIT),
        name="stack_tokens",
    )(a, b)


def _spatial_tiles(w_s, b_s, seq):
    rep = (jnp.arange(ROW_SUB)[:, None] % seq == jnp.arange(seq)[None, :]).astype(F32)
    exact = lax.Precision.HIGHEST
    wt = jnp.einsum("ta,gab,sb->gts", rep, w_s[:, :seq, :seq], rep, precision=exact)
    bt = jnp.einsum("ta,ga->gt", rep, b_s[:, :seq], precision=exact)
    return wt, jnp.broadcast_to(bt[:, :, None], bt.shape + (LANES,))


def _history_rows(hist):
    return jnp.pad(hist, ((0, 0), (SUBLANES - (CONV_W - 1), 0), (0, 0)))


def _state_rows(cs):
    return cs[:, SUBLANES - (CONV_W - 1):, :]


def kernel(x_prompt, x_sample, state_conv, w_in, b_in, conv_w, conv_b, sgu_ln_g, sgu_ln_b, w_spatial,
           b_spatial, w_a_out, w_b_out, w_o, ln1_g, ln1_b, router_w, router_bias, w_gate, w_up, w_down,
           ln2_g, ln2_b):
    depth, d, n_col = w_in.shape
    batch, seq, _ = x_prompt.shape
    dec_batch, dec_seq, _ = x_sample.shape
    dc = conv_w.shape[2]
    dsg = sgu_ln_g.shape[1]
    n_experts = router_w.shape[1]
    assert batch == 1 and n_col == 3 * dc + 2 * dsg + 2 * d
    assert dc == COL_BLOCK and dsg == COL_BLOCK and d % COL_BLOCK == 0
    assert dsg == SGU_GROUPS * LANES and n_experts % N_EXPERT_GROUPS == 0
    n_p, n_s = batch * seq, dec_batch * dec_seq
    n = n_p + n_s
    tm_p = min(MIX_ROWS, n_p)
    assert n_p % tm_p == 0 and n_s == ROW_SUB and ROW_SUB % dec_seq == 0 and n_p % TOK_ROWS == 0
    assert seq % SGU_CHUNK == 0 and SGU_CHUNK % dec_seq == 0
    alpha = (2.0 * depth) ** 0.25
    cfg = MixCfg(tm_p, n_p // tm_p, seq, dec_seq, n_s, d // COL_BLOCK)

    n_pairs = TOP_K * n
    n_tiles = n_pairs // TOK_ROWS + n_experts
    assert n_pairs % TOK_ROWS == 0

    x, x_tail = x_prompt.reshape(n_p, d), x_sample.reshape(n_s, d)
    xb = _stack_bf16(x, x_tail)
    rw = jnp.pad(router_w, ((0, 0), (0, LANES - n_experts))).astype(BF16)
    rb = jnp.broadcast_to(router_bias[:, None], (n_experts, TOK_ROWS))
    zero_hist = jnp.zeros((batch, CONV_W - 1, dc), F32)

    w_in_b, w_a_b, w_b_b, w_o_b = (_cast_bf16(w) for w in (w_in, w_a_out, w_b_out, w_o))

    conv_p, conv_s, v_s = [], [], []
    for l in range(depth):
        bias = b_in[l][None, :]
        cw = jnp.concatenate([conv_w[l], conv_b[l][None, :],
                              jnp.zeros((SUBLANES - CONV_W - 1, dc), F32)], axis=0)
        lng, lnb = sgu_ln_g[l][None, :], sgu_ln_b[l][None, :]
        wt_p, bt_p = _spatial_tiles(w_spatial[l], b_spatial[l], min(seq, SGU_CHUNK))
        wt_s, bt_s = _spatial_tiles(w_spatial[l], b_spatial[l], min(dec_seq, SGU_CHUNK))

        ya, yb, gates, cs_p, cs_s, v_rows = _mixer_call(
            xb, w_in_b, l, bias, _history_rows(zero_hist), _history_rows(state_conv[l]), cw, lng, lnb,
            wt_p, bt_p, wt_s, bt_s, cfg)
        conv_p.append(_state_rows(cs_p))
        conv_s.append(_state_rows(cs_s))
        v_s.append(v_rows.reshape(dec_batch, dec_seq, dsg))

        x1, eidx, gw, rank, cnt = _merge_call(
            ya, yb, gates, x, x_tail, w_a_b, w_b_b, w_o_b, l,
            ln1_g[l][None, :], ln1_b[l][None, :], rw, rb, alpha, n_experts)

        counts = cnt[:, 0].astype(jnp.int32)
        padded = ((counts + TOK_ROWS - 1) // TOK_ROWS) * TOK_ROWS
        seg_end = jnp.cumsum(padded)
        seg_start = seg_end - padded
        n_used = (seg_end[-1:] // TOK_ROWS).astype(jnp.int32)
        tile_row0 = jnp.arange(n_tiles, dtype=jnp.int32) * TOK_ROWS
        tile_expert = jnp.minimum(
            jnp.sum(tile_row0[:, None] >= seg_end[None, :], axis=1), n_experts - 1).astype(jnp.int32)
        e_ids = jnp.arange(n_experts, dtype=jnp.int32)
        pos = rank[:TOP_K] + jnp.sum(
            jnp.where(eidx[:TOP_K, :, None] == e_ids, seg_start.astype(jnp.int32), 0), axis=-1)

        n_t = (padded // TOK_ROWS).astype(jnp.int32)
        later = jnp.where(n_t > 0, e_ids, n_experts)
        nxt_e = jnp.concatenate([jnp.flip(lax.cummin(jnp.flip(later)))[1:],
                                 jnp.full((1,), n_experts, jnp.int32)])
        group = (jnp.cumsum(n_t > 0) - 1).astype(jnp.int32)
        tile_i = jnp.arange(n_tiles, dtype=jnp.int32)
        of_tile = tile_expert[:, None] == e_ids

        def lookup(table):
            return jnp.sum(jnp.where(of_tile, table.astype(jnp.int32), 0), axis=1)

        q = tile_i - lookup(seg_start // TOK_ROWS)
        g_t = jnp.maximum(lookup(n_t), 1)
        nxt_t = lookup(nxt_e)
        loads = jnp.logical_and(tile_i < n_used[0], nxt_t < n_experts)
        t_lo = jnp.where(loads, (W_PARTS * q) // g_t, 0).astype(jnp.int32)
        t_hi = jnp.where(loads, (W_PARTS * (q + 1)) // g_t, 0).astype(jnp.int32)
        sched = (tile_expert, n_used, jnp.minimum(nxt_t, n_experts - 1).astype(jnp.int32),
                 t_lo, t_hi, (lookup(group) % 2).astype(jnp.int32))

        pos_flat = pos.reshape(-1)
        row_token = _invert_call(pos_flat, n, n_tiles * TOK_ROWS)
        ys = _ffn_call(sched, row_token, x1, w_gate, w_up, w_down, l)
        gwc = jnp.pad(gw[:TOP_K].T, ((0, 0), (0, LANES - TOP_K)))
        x, xb = _norm2_call(pos_flat, x1, gwc, ln2_g[l][None, :], ln2_b[l][None, :], ys, alpha,
                            n_p if l == depth - 1 else None)
        x_tail = x[:TOK_ROWS]

    return (x.reshape(batch, seq, d), xb.reshape(dec_batch, dec_seq, d),
            jnp.stack(conv_p), jnp.stack(conv_s), jnp.stack(v_s))
```

```python
import functools
from typing import NamedTuple

import jax
import jax.numpy as jnp
from jax import lax
from jax.experimental import pallas as pl
from jax.experimental.pallas import tpu as pltpu

CHUNK = 64
SGU_CHUNK = 128
SGU_GROUPS = 8
N_EXPERT_GROUPS = 4
TOP_K = 2
LN_EPS = 1e-5
CONV_W = 3

LANES = 128
SUBLANES = 8
COL_BLOCK = 1024
ROW_SUB = 256
MIX_ROWS = 1024
TOK_ROWS = 256
W_PARTS = 8
PREFETCH = 2
CAST_BLOCK_BYTES = 10 * 1024 * 1024
VMEM_LIMIT = 56 * 1024 * 1024

F32 = jnp.float32
BF16 = jnp.bfloat16


class MixCfg(NamedTuple):
    tm: int
    n_p_tiles: int
    seq: int
    dec_seq: int
    n_s: int
    ng: int


def _ln(z, g, b):
    mu = jnp.mean(z, axis=-1, keepdims=True)
    d = z - mu
    var = jnp.mean(d * d, axis=-1, keepdims=True)
    return d * lax.rsqrt(var + LN_EPS) * g + b


def _conv_rows(hin, p2, p1, cw_ref):
    rows = lax.broadcasted_iota(jnp.int32, hin.shape, 0)
    r1 = pltpu.roll(hin, 1, 0)
    r2 = pltpu.roll(hin, 2, 0)
    sh1 = jnp.where(rows == 0, p1, r1)
    sh2 = jnp.where(rows == 0, p2, jnp.where(rows == 1, p1, r2))
    out = cw_ref[3:4, :] + cw_ref[0:1, :] * sh2
    out = out + cw_ref[1:2, :] * sh1
    return out + cw_ref[2:3, :] * hin


def _spatial_keep(seq):
    t = lax.broadcasted_iota(jnp.int32, (ROW_SUB, ROW_SUB), 0)
    s = lax.broadcasted_iota(jnp.int32, (ROW_SUB, ROW_SUB), 1)
    same = (t // seq) == (s // seq)
    causal = ((s % seq) // CHUNK) <= ((t % seq) // CHUNK)
    return jnp.logical_and(same, causal)


BLK_B, BLK_C, BLK_H, BLK_U, BLK_V, BLK_GATES = 0, 1, 2, 3, 4, 5
STEP_H, STEP_C, STEP_B, STEP_V, STEP_U, STEP_GATES = 0, 1, 2, 3, 4, 5


def _mixer_block(j):
    return jnp.where(j < STEP_V, BLK_H - j, jnp.where(j < STEP_GATES, BLK_U + BLK_V - j, j))


def _mixer_kernel(x_ref, w_ref, bias_ref, prevp_ref, prevs_ref, cw_ref, lng_ref, lnb_ref,
                  wtp_ref, btp_ref, wts_ref, bts_ref,
                  ya_ref, yb_ref, gates_ref, csp_ref, css_ref, v_ref,
                  hsc, ssc, vsc, carry, wm_sc, bt_sc, *, cfg: MixCfg):
    i = pl.program_id(0)
    j = pl.program_id(1)
    is_prompt = i < cfg.n_p_tiles
    is_sample = i == cfg.n_p_tiles

    def sub_rows(r):
        return pl.ds(r * ROW_SUB, ROW_SUB)

    def proj(r):
        rows = sub_rows(r)
        acc = jnp.dot(x_ref[rows, :], w_ref[0], preferred_element_type=F32)
        return rows, acc + bias_ref[...]

    def for_sub(body):
        @pl.when(is_prompt)
        def _():
            for r in range(cfg.tm // ROW_SUB):
                body(r, False)

        @pl.when(is_sample)
        def _():
            for r in range(cfg.n_s // ROW_SUB):
                body(r, True)

    @pl.when(jnp.logical_and(i == 0, j == 0))
    def _():
        carry[...] = prevp_ref[0]

    @pl.when(j == STEP_H)
    def _():
        def body(r, sample):
            rows, a = proj(r)
            hsc[rows, :] = a
        for_sub(body)

    @pl.when(j == STEP_C)
    def _():
        def body(r, sample):
            rows, a = proj(r)
            hin = a * hsc[rows, :]
            if not sample:
                hsc[rows, :] = _conv_rows(hin, carry[6:7, :], carry[7:8, :], cw_ref)
                carry[...] = hin[ROW_SUB - SUBLANES:, :]
                csp_ref[0] = hin[ROW_SUB - SUBLANES:, :]
            else:
                sr = cfg.dec_seq
                for s in range(ROW_SUB // sr):
                    hs = hin[s * sr:(s + 1) * sr, :]
                    st = r * (ROW_SUB // sr) + s
                    srow = pl.ds(r * ROW_SUB + s * sr, sr)
                    hsc[srow, :] = _conv_rows(hs, prevs_ref[st, 6:7, :], prevs_ref[st, 7:8, :], cw_ref)
                    css_ref[st] = hs[sr - SUBLANES:, :]
        for_sub(body)

    @pl.when(j == STEP_B)
    def _():
        def body(r, sample):
            rows, a = proj(r)
            ya_ref[rows, :] = (a * hsc[rows, :]).astype(ya_ref.dtype)
        for_sub(body)

    @pl.when(j == STEP_V)
    def _():
        @pl.when(is_prompt)
        def _():
            keep = _spatial_keep(min(cfg.seq, SGU_CHUNK))
            for g in range(SGU_GROUPS):
                wm_sc[g] = jnp.where(keep, wtp_ref[g], 0.0).astype(wm_sc.dtype)
            bt_sc[...] = btp_ref[...]

        @pl.when(is_sample)
        def _():
            keep = _spatial_keep(min(cfg.dec_seq, SGU_CHUNK))
            for g in range(SGU_GROUPS):
                wm_sc[g] = jnp.where(keep, wts_ref[g], 0.0).astype(wm_sc.dtype)
            bt_sc[...] = bts_ref[...]

        def spatial(n_sub):
            for g in range(SGU_GROUPS):
                wide = jnp.dot(wm_sc[g], vsc[:, g * n_sub * LANES:(g + 1) * n_sub * LANES],
                               preferred_element_type=F32)
                for r in range(n_sub):
                    ssc[sub_rows(r), g * LANES:(g + 1) * LANES] = wide[:, r * LANES:(r + 1) * LANES] + bt_sc[g]

        def body(r, sample):
            n_sub = (cfg.n_s if sample else cfg.tm) // ROW_SUB
            rows, a = proj(r)
            vn = _ln(jax.nn.gelu(a), lng_ref[...], lnb_ref[...])
            if sample:
                v_ref[rows, :] = vn
            vnb = vn.astype(BF16)
            for g in range(SGU_GROUPS):
                col = (g * n_sub + r) * LANES
                vsc[:, col:col + LANES] = vnb[:, g * LANES:(g + 1) * LANES]
            if r == n_sub - 1:
                spatial(n_sub)
        for_sub(body)

    @pl.when(j == STEP_U)
    def _():
        def body(r, sample):
            rows, a = proj(r)
            yb_ref[rows, :] = (jax.nn.gelu(a) * ssc[rows, :]).astype(yb_ref.dtype)
        for_sub(body)

    @pl.when(j >= STEP_GATES)
    def _():
        def body(r, sample):
            rows, a = proj(r)
            gates_ref[rows, :] = jax.nn.sigmoid(a)
        for_sub(body)


def _mixer_call(xb, w_in, layer, b_in, prev_p, prev_s, cw, lng, lnb, wt_p, bt_p, wt_s, bt_s, cfg):
    n, d = xb.shape
    cb = COL_BLOCK
    n_j = w_in.shape[2] // cb
    n_i = cfg.n_p_tiles + 1
    n_gate_blocks = 2 * cfg.ng

    def const(shape):
        return pl.BlockSpec(shape, lambda i, j: (0,) * len(shape))

    in_specs = [
        pl.BlockSpec((cfg.tm, d), lambda i, j: (i, 0)),
        pl.BlockSpec((1, d, cb), lambda i, j: (layer, 0, _mixer_block(j))),
        pl.BlockSpec((1, cb), lambda i, j: (0, _mixer_block(j))),
        const(prev_p.shape), const(prev_s.shape), const(cw.shape), const(lng.shape), const(lnb.shape),
        const(wt_p.shape), const(bt_p.shape), const(wt_s.shape), const(bt_s.shape),
    ]
    out_shape = [
        jax.ShapeDtypeStruct((n, cb), BF16),
        jax.ShapeDtypeStruct((n, cb), BF16),
        jax.ShapeDtypeStruct((n, n_gate_blocks * cb), F32),
        jax.ShapeDtypeStruct(prev_p.shape, F32),
        jax.ShapeDtypeStruct(prev_s.shape, F32),
        jax.ShapeDtypeStruct((cfg.n_s, cb), F32),
    ]
    out_specs = [
        pl.BlockSpec((cfg.tm, cb), lambda i, j: (i, 0)),
        pl.BlockSpec((cfg.tm, cb), lambda i, j: (i, 0)),
        pl.BlockSpec((cfg.tm, cb), lambda i, j: (i, jnp.clip(j - STEP_GATES, 0, n_gate_blocks - 1))),
        const(prev_p.shape), const(prev_s.shape), const((cfg.n_s, cb)),
    ]
    scratch = [
        pltpu.VMEM((cfg.tm, cb), F32),
        pltpu.VMEM((cfg.tm, cb), F32),
        pltpu.VMEM((ROW_SUB, (cfg.tm // ROW_SUB) * cb), BF16),
        pltpu.VMEM((SUBLANES, cb), F32),
        pltpu.VMEM((SGU_GROUPS, ROW_SUB, ROW_SUB), BF16),
        pltpu.VMEM((SGU_GROUPS, ROW_SUB, LANES), F32),
    ]
    assert n_j == BLK_GATES + n_gate_blocks
    return pl.pallas_call(
        functools.partial(_mixer_kernel, cfg=cfg),
        grid=(n_i, n_j),
        in_specs=in_specs,
        out_specs=out_specs,
        out_shape=out_shape,
        scratch_shapes=scratch,
        compiler_params=pltpu.CompilerParams(
            dimension_semantics=("arbitrary", "arbitrary"), vmem_limit_bytes=VMEM_LIMIT),
        name="mixer",
    )(xb, w_in, b_in, prev_p, prev_s, cw, lng, lnb, wt_p, bt_p, wt_s, bt_s)


def _merge_kernel(ya_ref, yb_ref, ga_ref, gb_ref, xa_ref, xb_ref, wa_ref, wb_ref, wo_ref, g1_ref, b1_ref,
                  rw_ref, rb_ref, x1_ref, eidx_ref, gw_ref, rank_ref, cnt_ref, run, *,
                  alpha, n_experts, n_a_tiles, split):
    i = pl.program_id(0)
    tm = xa_ref.shape[0]
    epg = n_experts // N_EXPERT_GROUPS

    @pl.when(i == 0)
    def _():
        run[...] = jnp.zeros_like(run)

    a = jnp.dot(ya_ref[...], wa_ref[0], preferred_element_type=F32)
    b = jnp.dot(yb_ref[...], wb_ref[0], preferred_element_type=F32)
    merged = ga_ref[...] * a + gb_ref[...] * b
    x = xa_ref[...]
    if split:
        x = jnp.where(i < n_a_tiles, x, xb_ref[...])
    z = alpha * x + jnp.dot(merged.astype(BF16), wo_ref[0], preferred_element_type=F32)
    x1 = _ln(z, g1_ref[...], b1_ref[...])
    x1_ref[...] = x1

    logits = jnp.dot(x1.astype(BF16), rw_ref[...], preferred_element_type=F32)
    lt = logits.T[:n_experts, :]
    ex = jnp.exp(lt - jnp.max(lt, axis=0, keepdims=True))
    sc = ex / jnp.sum(ex, axis=0, keepdims=True)
    sel = sc + rb_ref[...]
    sel_r = [sel[e:e + 1, :] for e in range(n_experts)]
    sc_r = [sc[e:e + 1, :] for e in range(n_experts)]

    def pair_max(v):
        best = None
        for p in range(len(v)):
            for q in range(p + 1, len(v)):
                s = v[p] + v[q]
                best = s if best is None else jnp.maximum(best, s)
        return best

    grp = [pair_max(sel_r[g * epg:(g + 1) * epg]) for g in range(N_EXPERT_GROUPS)]
    gi = jnp.zeros((1, tm), jnp.int32)
    gbest = grp[0]
    for g in range(1, N_EXPERT_GROUPS):
        better = grp[g] > gbest
        gi = jnp.where(better, g, gi)
        gbest = jnp.where(better, grp[g], gbest)

    def pick(rows_, idx):
        out = rows_[-1]
        for c in range(len(rows_) - 2, -1, -1):
            out = jnp.where(idx == c, rows_[c], out)
        return out

    vk = [pick([sel_r[g * epg + q] for g in range(N_EXPERT_GROUPS)], gi) for q in range(epg)]
    pk = [pick([sc_r[g * epg + q] for g in range(N_EXPERT_GROUPS)], gi) for q in range(epg)]

    i1 = jnp.zeros((1, tm), jnp.int32)
    b1 = vk[0]
    for q in range(1, epg):
        better = vk[q] > b1
        i1 = jnp.where(better, q, i1)
        b1 = jnp.where(better, vk[q], b1)
    i2 = jnp.zeros((1, tm), jnp.int32)
    b2 = jnp.full((1, tm), -jnp.inf, F32)
    for q in range(epg):
        cand = jnp.logical_and(i1 != q, vk[q] > b2)
        i2 = jnp.where(cand, q, i2)
        b2 = jnp.where(cand, vk[q], b2)
    p1 = pick(pk, i1)
    p2 = pick(pk, i2)
    den = p1 + p2
    e1 = gi * epg + i1
    e2 = gi * epg + i2

    eio = lax.broadcasted_iota(jnp.int32, (n_experts, tm), 0)
    oh1 = (eio == e1).astype(F32)
    oh2 = (eio == e2).astype(F32)
    ts = lax.broadcasted_iota(jnp.int32, (tm, tm), 0)
    tt = lax.broadcasted_iota(jnp.int32, (tm, tm), 1)
    upper = (ts <= tt).astype(BF16)
    inc1 = jnp.dot(oh1.astype(BF16), upper, preferred_element_type=F32)
    inc2 = jnp.dot(oh2.astype(BF16), upper, preferred_element_type=F32)
    tot1 = jnp.sum(oh1, axis=1, keepdims=True)
    tot2 = jnp.sum(oh2, axis=1, keepdims=True)
    base = run[:, 0:1]
    r1 = jnp.sum(oh1 * (inc1 - 1.0 + base), axis=0, keepdims=True)
    r2 = jnp.sum(oh2 * (inc2 - 1.0 + base + tot1), axis=0, keepdims=True)
    new_run = run[...] + tot1 + tot2
    run[...] = new_run
    cnt_ref[...] = new_run

    eidx_ref[...] = jnp.zeros_like(eidx_ref)
    eidx_ref[0:1, :] = e1
    eidx_ref[1:2, :] = e2
    gw_ref[...] = jnp.zeros_like(gw_ref)
    gw_ref[0:1, :] = p1 / den
    gw_ref[1:2, :] = p2 / den
    rank_ref[...] = jnp.zeros_like(rank_ref)
    rank_ref[0:1, :] = r1.astype(jnp.int32)
    rank_ref[1:2, :] = r2.astype(jnp.int32)


def _merge_call(ya, yb, gates, xa, xb, wa, wb, wo, layer, g1, b1, rw, rb, alpha, n_experts):
    n, d = ya.shape[0], xa.shape[1]
    tm = TOK_ROWS
    dc, dsg = ya.shape[1], yb.shape[1]
    n_a_tiles = xa.shape[0] // tm
    assert xa.shape[0] % tm == 0 and xb.shape[0] == tm and n_a_tiles in (n // tm, n // tm - 1)

    def rows(c):
        return pl.BlockSpec((tm, c), lambda i: (i, 0))

    def const(shape):
        return pl.BlockSpec(shape, lambda i: (0,) * len(shape))

    def lanes(r):
        return pl.BlockSpec((r, tm), lambda i: (0, i))

    def layer_block(w):
        return pl.BlockSpec((1,) + w.shape[1:], lambda i: (layer, 0, 0))

    return pl.pallas_call(
        functools.partial(_merge_kernel, alpha=alpha, n_experts=n_experts, n_a_tiles=n_a_tiles,
                          split=n_a_tiles < n // tm),
        grid=(n // tm,),
        in_specs=[rows(dc), rows(dsg), rows(d), pl.BlockSpec((tm, d), lambda i: (i, 1)),
                  pl.BlockSpec((tm, d), lambda i: (jnp.minimum(i, n_a_tiles - 1), 0)), const((tm, d)),
                  layer_block(wa), layer_block(wb), layer_block(wo), const(g1.shape), const(b1.shape),
                  const(rw.shape), const(rb.shape)],
        out_specs=[rows(d), lanes(SUBLANES), lanes(SUBLANES), lanes(SUBLANES),
                   const((n_experts, LANES))],
        out_shape=[jax.ShapeDtypeStruct((n, d), F32),
                   jax.ShapeDtypeStruct((SUBLANES, n), jnp.int32),
                   jax.ShapeDtypeStruct((SUBLANES, n), F32),
                   jax.ShapeDtypeStruct((SUBLANES, n), jnp.int32),
                   jax.ShapeDtypeStruct((n_experts, LANES), F32)],
        scratch_shapes=[pltpu.VMEM((n_experts, LANES), F32)],
        compiler_params=pltpu.CompilerParams(
            dimension_semantics=("arbitrary",), vmem_limit_bytes=VMEM_LIMIT),
        name="merge_route",
    )(ya, yb, gates, gates, xa, xb, wa, wb, wo, g1, b1, rw, rb)


def _row_gather(src_hbm, dst, sem, index_of, n_rows):
    for r in range(n_rows):
        pltpu.make_async_copy(src_hbm.at[pl.ds(index_of(r), 1)], dst.at[pl.ds(r, 1)], sem).start()


def _row_gather_wait(src_hbm, dst, sem):
    pltpu.make_async_copy(src_hbm.at[pl.ds(0, dst.shape[0])], dst, sem).wait()


def _ffn_kernel(te_ref, nu_ref, ne_ref, t0_ref, t1_ref, ws_ref, tok_ref,
                x1_hbm, wg_hbm, wu_hbm, wd_hbm, ys_ref,
                xbuf, xsem, wgb, wub, wdb, sg, su, sd, wsem, *, layer):
    i = pl.program_id(0)
    n_used = nu_ref[0]
    tm = xbuf.shape[1]
    slot = i % (PREFETCH + 1)

    def part_copies(e, t, s):
        return (pltpu.make_async_copy(wg_hbm.at[layer, e, t], sg.at[s], wsem.at[s]),
                pltpu.make_async_copy(wu_hbm.at[layer, e, t], su.at[s], wsem.at[s]),
                pltpu.make_async_copy(wd_hbm.at[layer, e, t], sd.at[s], wsem.at[s]))

    def start_part(e, t, s):
        for c in part_copies(e, t, s):
            c.start(priority=1)

    def finish_part(e, t, s, w):
        for c in part_copies(e, t, s):
            c.wait()
        rin, rout = sg.shape[1], sd.shape[1]
        wgb[w, pl.ds(pl.multiple_of(t * rin, rin), rin), :] = sg[s].astype(BF16)
        wub[w, pl.ds(pl.multiple_of(t * rin, rin), rin), :] = su[s].astype(BF16)
        wdb[w, pl.ds(pl.multiple_of(t * rout, rout), rout), :] = sd[s].astype(BF16)

    def load_parts(e, t_lo, t_hi, w):
        def step(t, c):
            s = (t - t_lo) % 2
            finish_part(e, t, s, w)

            @pl.when(t + 2 < t_hi)
            def _():
                start_part(e, t + 2, s)
            return c
        lax.fori_loop(t_lo, t_hi, step, 0)

    def start_first_parts(e, t_lo, t_hi):
        @pl.when(t_lo < t_hi)
        def _():
            start_part(e, t_lo, 0)

        @pl.when(t_lo + 1 < t_hi)
        def _():
            start_part(e, t_lo + 1, 1)

    @pl.when(i == 0)
    def _():
        for a in range(PREFETCH):
            tile = jnp.minimum(a, n_used - 1)

            def first(r, c, a=a, tile=tile):
                pltpu.make_async_copy(x1_hbm.at[pl.ds(tok_ref[tile * tm + r], 1)], xbuf.at[a, pl.ds(r, 1)],
                                      xsem.at[a]).start()
                return c
            lax.fori_loop(0, tm, first, 0)
        start_first_parts(te_ref[0], 0, W_PARTS)
        load_parts(te_ref[0], 0, W_PARTS, 0)

    @pl.when(i < n_used)
    def _():
        w = ws_ref[i]
        e_next, t_lo, t_hi = ne_ref[i], t0_ref[i], t1_ref[i]
        start_first_parts(e_next, t_lo, t_hi)
        _row_gather_wait(x1_hbm, xbuf.at[slot], xsem.at[slot])
        x = xbuf[slot].astype(BF16)
        ahead = jnp.minimum(i + PREFETCH, n_used - 1)
        into = (i + PREFETCH) % (PREFETCH + 1)
        _row_gather(x1_hbm, xbuf.at[into], xsem.at[into], lambda r: tok_ref[ahead * tm + r], tm)
        g = jnp.dot(x, wgb[w], preferred_element_type=F32)
        u = jnp.dot(x, wub[w], preferred_element_type=F32)
        h = ((g * jax.nn.sigmoid(g)) * u).astype(BF16)
        ys_ref[...] = jnp.dot(h, wdb[w], preferred_element_type=F32)
        load_parts(e_next, t_lo, t_hi, 1 - w)

    @pl.when(i == n_used - 1)
    def _():
        for a in range(1, PREFETCH + 1):
            s = (i + a) % (PREFETCH + 1)
            _row_gather_wait(x1_hbm, xbuf.at[s], xsem.at[s])

    @pl.when(i >= n_used)
    def _():
        ys_ref[...] = jnp.zeros_like(ys_ref)


def _ffn_call(sched, row_token, x1, w_gate, w_up, w_down, layer):
    d = x1.shape[1]
    p = row_token.shape[0]
    depth, n_e, _, f = w_gate.shape
    tm = TOK_ROWS
    wg = w_gate.reshape(depth, n_e, W_PARTS, d // W_PARTS, f)
    wu = w_up.reshape(depth, n_e, W_PARTS, d // W_PARTS, f)
    wd = w_down.reshape(depth, n_e, W_PARTS, f // W_PARTS, d)
    any_spec = pl.BlockSpec(memory_space=pl.ANY)
    return pl.pallas_call(
        functools.partial(_ffn_kernel, layer=layer),
        grid_spec=pltpu.PrefetchScalarGridSpec(
            num_scalar_prefetch=7,
            grid=(p // tm,),
            in_specs=[any_spec, any_spec, any_spec, any_spec],
            out_specs=pl.BlockSpec((tm, d), lambda i, *_: (i, 0)),
            scratch_shapes=[pltpu.VMEM((PREFETCH + 1, tm, d), F32), pltpu.SemaphoreType.DMA((PREFETCH + 1,)),
                            pltpu.VMEM((2, d, f), BF16), pltpu.VMEM((2, d, f), BF16),
                            pltpu.VMEM((2, f, d), BF16),
                            pltpu.VMEM((2, d // W_PARTS, f), F32), pltpu.VMEM((2, d // W_PARTS, f), F32),
                            pltpu.VMEM((2, f // W_PARTS, d), F32), pltpu.SemaphoreType.DMA((2,))],
        ),
        out_shape=jax.ShapeDtypeStruct((p, d), F32),
        compiler_params=pltpu.CompilerParams(
            dimension_semantics=("arbitrary",), vmem_limit_bytes=VMEM_LIMIT),
        name="expert_ffn",
    )(*sched, row_token, x1, wg, wu, wd)


def _norm2_kernel(pos_ref, x1_ref, gw_ref, g_ref, b_ref, ys_hbm, out_a, out_b, ybuf, sem, *, alpha,
                  n_p_tiles):
    i = pl.program_id(0)
    n_steps = pl.num_programs(0)
    tm = x1_ref.shape[0]
    n = tm * n_steps
    slot = i % (PREFETCH + 1)

    @pl.when(i == 0)
    def _():
        for a in range(PREFETCH):
            tile = jnp.minimum(a, n_steps - 1)

            def first(r, c, a=a, tile=tile):
                for k in range(TOP_K):
                    pltpu.make_async_copy(ys_hbm.at[pl.ds(pos_ref[k * n + tile * tm + r], 1)],
                                          ybuf.at[a, k, pl.ds(r, 1)], sem.at[a]).start()
                return c
            lax.fori_loop(0, tm, first, 0)

    for k in range(TOP_K):
        _row_gather_wait(ys_hbm, ybuf.at[slot, k], sem.at[slot])
    ahead = jnp.minimum(i + PREFETCH, n_steps - 1)
    into = (i + PREFETCH) % (PREFETCH + 1)
    for k in range(TOP_K):
        _row_gather(ys_hbm, ybuf.at[into, k], sem.at[into],
                    lambda r, k=k: pos_ref[k * n + ahead * tm + r], tm)
    y = None
    for k in range(TOP_K):
        term = gw_ref[:, k:k + 1] * ybuf[slot, k]
        y = term if y is None else y + term
    x2 = _ln(alpha * x1_ref[...] + y, g_ref[...], b_ref[...])
    if n_p_tiles is None:
        out_a[...] = x2
        out_b[...] = x2.astype(out_b.dtype)
    else:
        @pl.when(i < n_p_tiles)
        def _():
            out_a[...] = x2

        @pl.when(i >= n_p_tiles)
        def _():
            out_b[...] = x2

    @pl.when(i == n_steps - 1)
    def _():
        for a in range(1, PREFETCH + 1):
            s = (i + a) % (PREFETCH + 1)
            for k in range(TOP_K):
                _row_gather_wait(ys_hbm, ybuf.at[s, k], sem.at[s])


def _norm2_call(pos_flat, x1, gwc, g, b, ys, alpha, n_p=None):
    n, d = x1.shape
    tm = TOK_ROWS
    rows = pl.BlockSpec((tm, d), lambda i, pos: (i, 0))
    const = pl.BlockSpec((1, d), lambda i, pos: (0, 0))
    if n_p is None:
        n_p_tiles = None
        out_specs = [rows, rows]
        out_shape = [jax.ShapeDtypeStruct((n, d), F32), jax.ShapeDtypeStruct((n, d), BF16)]
    else:
        assert n - n_p == tm
        n_p_tiles = n_p // tm
        out_specs = [pl.BlockSpec((tm, d), lambda i, pos: (jnp.minimum(i, n_p_tiles - 1), 0)),
                     pl.BlockSpec((tm, d), lambda i, pos: (0, 0))]
        out_shape = [jax.ShapeDtypeStruct((n_p, d), F32), jax.ShapeDtypeStruct((tm, d), F32)]
    return pl.pallas_call(
        functools.partial(_norm2_kernel, alpha=alpha, n_p_tiles=n_p_tiles),
        grid_spec=pltpu.PrefetchScalarGridSpec(
            num_scalar_prefetch=1,
            grid=(n // tm,),
            in_specs=[rows, pl.BlockSpec((tm, LANES), lambda i, pos: (i, 0)), const, const,
                      pl.BlockSpec(memory_space=pl.ANY)],
            out_specs=out_specs,
            scratch_shapes=[pltpu.VMEM((PREFETCH + 1, TOP_K, tm, d), F32),
                            pltpu.SemaphoreType.DMA((PREFETCH + 1,))],
        ),
        out_shape=out_shape,
        compiler_params=pltpu.CompilerParams(
            dimension_semantics=("arbitrary",), vmem_limit_bytes=VMEM_LIMIT),
        name="combine_norm2",
    )(pos_flat, x1, gwc, g, b, ys)


def _invert_kernel(pos_ref, zeros_hbm, tok_ref, sem, *, n):
    clear = pltpu.make_async_copy(zeros_hbm, tok_ref, sem)
    clear.start()
    clear.wait()

    def put(t, c):
        for k in range(TOP_K):
            tok_ref[pos_ref[k * n + t]] = t
        return c
    lax.fori_loop(0, n, put, 0, unroll=16)


def _invert_call(pos_flat, n, n_rows):
    smem = pl.BlockSpec(memory_space=pltpu.SMEM)
    return pl.pallas_call(
        functools.partial(_invert_kernel, n=n),
        in_specs=[smem, pl.BlockSpec(memory_space=pl.ANY)],
        out_specs=smem,
        out_shape=jax.ShapeDtypeStruct((n_rows,), jnp.int32),
        scratch_shapes=[pltpu.SemaphoreType.DMA(())],
        name="row_tokens",
    )(pos_flat, jnp.zeros((n_rows,), jnp.int32))


def _cast_kernel(x_ref, o_ref):
    o_ref[...] = x_ref[...].astype(o_ref.dtype)


def _cast_bf16(w):
    cols = w.shape[-1]
    rows = w.size // cols
    br = rows
    while br * cols * 4 > CAST_BLOCK_BYTES and br % 32 == 0:
        br //= 2
    block = pl.BlockSpec((br, cols), lambda i: (i, 0))
    out = pl.pallas_call(
        _cast_kernel,
        grid=(rows // br,),
        in_specs=[block],
        out_specs=block,
        out_shape=jax.ShapeDtypeStruct((rows, cols), BF16),
        compiler_params=pltpu.CompilerParams(
            dimension_semantics=("arbitrary",), vmem_limit_bytes=VMEM_LIMIT),
        name="cast_weights",
    )(w.reshape(rows, cols))
    return out.reshape(w.shape)


def _stack_kernel(a_ref, b_ref, o_ref, *, n_a_tiles):
    i = pl.program_id(0)

    @pl.when(i < n_a_tiles)
    def _():
        o_ref[...] = a_ref[...].astype(o_ref.dtype)

    @pl.when(i >= n_a_tiles)
    def _():
        o_ref[:b_ref.shape[0], :] = b_ref[...].astype(o_ref.dtype)


def _stack_bf16(a, b):
    tm = min(MIX_ROWS, a.shape[0])
    d = a.shape[1]
    n_a_tiles = a.shape[0] // tm
    assert a.shape[0] % tm == 0 and b.shape[0] <= tm
    return pl.pallas_call(
        functools.partial(_stack_kernel, n_a_tiles=n_a_tiles),
        grid=(n_a_tiles + 1,),
        in_specs=[pl.BlockSpec((tm, d), lambda i: (jnp.minimum(i, n_a_tiles - 1), 0)),
                  pl.BlockSpec(b.shape, lambda i: (0, 0))],
        out_specs=pl.BlockSpec((tm, d), lambda i: (i, 0)),
        out_shape=jax.ShapeDtypeStruct((a.shape[0] + b.shape[0], d), BF16),
        compiler_params=pltpu.CompilerParams(
            dimension_semantics=("arbitrary",), vmem_limit_bytes=VMEM_LIMIT),
        name="stack_tokens",
    )(a, b)


def _spatial_tiles(w_s, b_s, seq):
    rep = (jnp.arange(ROW_SUB)[:, None] % seq == jnp.arange(seq)[None, :]).astype(F32)
    exact = lax.Precision.HIGHEST
    wt = jnp.einsum("ta,gab,sb->gts", rep, w_s[:, :seq, :seq], rep, precision=exact)
    bt = jnp.einsum("ta,ga->gt", rep, b_s[:, :seq], precision=exact)
    return wt, jnp.broadcast_to(bt[:, :, None], bt.shape + (LANES,))


def _history_rows(hist):
    return jnp.pad(hist, ((0, 0), (SUBLANES - (CONV_W - 1), 0), (0, 0)))


def _state_rows(cs):
    return cs[:, SUBLANES - (CONV_W - 1):, :]


def kernel(x_prompt, x_sample, state_conv, w_in, b_in, conv_w, conv_b, sgu_ln_g, sgu_ln_b, w_spatial,
           b_spatial, w_a_out, w_b_out, w_o, ln1_g, ln1_b, router_w, router_bias, w_gate, w_up, w_down,
           ln2_g, ln2_b):
    depth, d, n_col = w_in.shape
    batch, seq, _ = x_prompt.shape
    dec_batch, dec_seq, _ = x_sample.shape
    dc = conv_w.shape[2]
    dsg = sgu_ln_g.shape[1]
    n_experts = router_w.shape[1]
    assert batch == 1 and n_col == 3 * dc + 2 * dsg + 2 * d
    assert dc == COL_BLOCK and dsg == COL_BLOCK and d % COL_BLOCK == 0
    assert dsg == SGU_GROUPS * LANES and n_experts % N_EXPERT_GROUPS == 0
    n_p, n_s = batch * seq, dec_batch * dec_seq
    n = n_p + n_s
    tm_p = min(MIX_ROWS, n_p)
    assert n_p % tm_p == 0 and n_s == ROW_SUB and ROW_SUB % dec_seq == 0 and n_p % TOK_ROWS == 0
    assert seq % SGU_CHUNK == 0 and SGU_CHUNK % dec_seq == 0
    alpha = (2.0 * depth) ** 0.25
    cfg = MixCfg(tm_p, n_p // tm_p, seq, dec_seq, n_s, d // COL_BLOCK)

    n_pairs = TOP_K * n
    n_tiles = n_pairs // TOK_ROWS + n_experts
    assert n_pairs % TOK_ROWS == 0

    x, x_tail = x_prompt.reshape(n_p, d), x_sample.reshape(n_s, d)
    xb = _stack_bf16(x, x_tail)
    rw = jnp.pad(router_w, ((0, 0), (0, LANES - n_experts))).astype(BF16)
    rb = jnp.broadcast_to(router_bias[:, None], (n_experts, TOK_ROWS))
    zero_hist = jnp.zeros((batch, CONV_W - 1, dc), F32)

    w_in_b, w_a_b, w_b_b, w_o_b = (_cast_bf16(w) for w in (w_in, w_a_out, w_b_out, w_o))

    conv_p, conv_s, v_s = [], [], []
    for l in range(depth):
        bias = b_in[l][None, :]
        cw = jnp.concatenate([conv_w[l], conv_b[l][None, :],
                              jnp.zeros((SUBLANES - CONV_W - 1, dc), F32)], axis=0)
        lng, lnb = sgu_ln_g[l][None, :], sgu_ln_b[l][None, :]
        wt_p, bt_p = _spatial_tiles(w_spatial[l], b_spatial[l], min(seq, SGU_CHUNK))
        wt_s, bt_s = _spatial_tiles(w_spatial[l], b_spatial[l], min(dec_seq, SGU_CHUNK))

        ya, yb, gates, cs_p, cs_s, v_rows = _mixer_call(
            xb, w_in_b, l, bias, _history_rows(zero_hist), _history_rows(state_conv[l]), cw, lng, lnb,
            wt_p, bt_p, wt_s, bt_s, cfg)
        conv_p.append(_state_rows(cs_p))
        conv_s.append(_state_rows(cs_s))
        v_s.append(v_rows.reshape(dec_batch, dec_seq, dsg))

        x1, eidx, gw, rank, cnt = _merge_call(
            ya, yb, gates, x, x_tail, w_a_b, w_b_b, w_o_b, l,
            ln1_g[l][None, :], ln1_b[l][None, :], rw, rb, alpha, n_experts)

        counts = cnt[:, 0].astype(jnp.int32)
        padded = ((counts + TOK_ROWS - 1) // TOK_ROWS) * TOK_ROWS
        seg_end = jnp.cumsum(padded)
        seg_start = seg_end - padded
        n_used = (seg_end[-1:] // TOK_ROWS).astype(jnp.int32)
        tile_row0 = jnp.arange(n_tiles, dtype=jnp.int32) * TOK_ROWS
        tile_expert = jnp.minimum(
            jnp.sum(tile_row0[:, None] >= seg_end[None, :], axis=1), n_experts - 1).astype(jnp.int32)
        e_ids = jnp.arange(n_experts, dtype=jnp.int32)
        pos = rank[:TOP_K] + jnp.sum(
            jnp.where(eidx[:TOP_K, :, None] == e_ids, seg_start.astype(jnp.int32), 0), axis=-1)

        n_t = (padded // TOK_ROWS).astype(jnp.int32)
        later = jnp.where(n_t > 0, e_ids, n_experts)
        nxt_e = jnp.concatenate([jnp.flip(lax.cummin(jnp.flip(later)))[1:],
                                 jnp.full((1,), n_experts, jnp.int32)])
        group = (jnp.cumsum(n_t > 0) - 1).astype(jnp.int32)
        tile_i = jnp.arange(n_tiles, dtype=jnp.int32)
        of_tile = tile_expert[:, None] == e_ids

        def lookup(table):
            return jnp.sum(jnp.where(of_tile, table.astype(jnp.int32), 0), axis=1)

        q = tile_i - lookup(seg_start // TOK_ROWS)
        g_t = jnp.maximum(lookup(n_t), 1)
        nxt_t = lookup(nxt_e)
        loads = jnp.logical_and(tile_i < n_used[0], nxt_t < n_experts)
        t_lo = jnp.where(loads, (W_PARTS * q) // g_t, 0).astype(jnp.int32)
        t_hi = jnp.where(loads, (W_PARTS * (q + 1)) // g_t, 0).astype(jnp.int32)
        sched = (tile_expert, n_used, jnp.minimum(nxt_t, n_experts - 1).astype(jnp.int32),
                 t_lo, t_hi, (lookup(group) % 2).astype(jnp.int32))

        pos_flat = pos.reshape(-1)
        row_token = _invert_call(pos_flat, n, n_tiles * TOK_ROWS)
        ys = _ffn_call(sched, row_token, x1, w_gate, w_up, w_down, l)
        gwc = jnp.pad(gw[:TOP_K].T, ((0, 0), (0, LANES - TOP_K)))
        x, xb = _norm2_call(pos_flat, x1, gwc, ln2_g[l][None, :], ln2_b[l][None, :], ys, alpha,
                            n_p if l == depth - 1 else None)
        x_tail = x[:TOK_ROWS]

    return (x.reshape(batch, seq, d), xb.reshape(dec_batch, dec_seq, d),
            jnp.stack(conv_p), jnp.stack(conv_s), jnp.stack(v_s))
```

```python
import functools
from typing import NamedTuple

import jax
import jax.numpy as jnp
from jax import lax
from jax.experimental import pallas as pl
from jax.experimental.pallas import tpu as pltpu

CHUNK = 64
SGU_CHUNK = 128
SGU_GROUPS = 8
N_EXPERT_GROUPS = 4
TOP_K = 2
LN_EPS = 1e-5
CONV_W = 3

LANES = 128
SUBLANES = 8
COL_BLOCK = 1024
ROW_SUB = 256
MIX_ROWS = 1024
TOK_ROWS = 256
W_PARTS = 8
PREFETCH = 2
CAST_BLOCK_BYTES = 10 * 1024 * 1024
VMEM_LIMIT = 56 * 1024 * 1024

F32 = jnp.float32
BF16 = jnp.bfloat16


class MixCfg(NamedTuple):
    tm: int
    n_p_tiles: int
    seq: int
    dec_seq: int
    n_s: int
    ng: int


def _ln(z, g, b):
    mu = jnp.mean(z, axis=-1, keepdims=True)
    d = z - mu
    var = jnp.mean(d * d, axis=-1, keepdims=True)
    return d * lax.rsqrt(var + LN_EPS) * g + b


def _conv_rows(hin, p2, p1, cw_ref):
    rows = lax.broadcasted_iota(jnp.int32, hin.shape, 0)
    r1 = pltpu.roll(hin, 1, 0)
    r2 = pltpu.roll(hin, 2, 0)
    sh1 = jnp.where(rows == 0, p1, r1)
    sh2 = jnp.where(rows == 0, p2, jnp.where(rows == 1, p1, r2))
    out = cw_ref[3:4, :] + cw_ref[0:1, :] * sh2
    out = out + cw_ref[1:2, :] * sh1
    return out + cw_ref[2:3, :] * hin


def _spatial_keep(seq):
    t = lax.broadcasted_iota(jnp.int32, (ROW_SUB, ROW_SUB), 0)
    s = lax.broadcasted_iota(jnp.int32, (ROW_SUB, ROW_SUB), 1)
    same = (t // seq) == (s // seq)
    causal = ((s % seq) // CHUNK) <= ((t % seq) // CHUNK)
    return jnp.logical_and(same, causal)


BLK_B, BLK_C, BLK_H, BLK_U, BLK_V, BLK_GATES = 0, 1, 2, 3, 4, 5
STEP_H, STEP_C, STEP_B, STEP_V, STEP_U, STEP_GATES = 0, 1, 2, 3, 4, 5


def _mixer_block(j):
    return jnp.where(j < STEP_V, BLK_H - j, jnp.where(j < STEP_GATES, BLK_U + BLK_V - j, j))


def _mixer_kernel(x_ref, w_ref, bias_ref, prevp_ref, prevs_ref, cw_ref, lng_ref, lnb_ref,
                  wtp_ref, btp_ref, wts_ref, bts_ref,
                  ya_ref, yb_ref, gates_ref, csp_ref, css_ref, v_ref,
                  hsc, ssc, vsc, carry, wm_sc, bt_sc, *, cfg: MixCfg):
    i = pl.program_id(0)
    j = pl.program_id(1)
    is_prompt = i < cfg.n_p_tiles
    is_sample = i == cfg.n_p_tiles

    def sub_rows(r):
        return pl.ds(r * ROW_SUB, ROW_SUB)

    def proj(r):
        rows = sub_rows(r)
        acc = jnp.dot(x_ref[rows, :], w_ref[0], preferred_element_type=F32)
        return rows, acc + bias_ref[...]

    def for_sub(body):
        @pl.when(is_prompt)
        def _():
            for r in range(cfg.tm // ROW_SUB):
                body(r, False)

        @pl.when(is_sample)
        def _():
            for r in range(cfg.n_s // ROW_SUB):
                body(r, True)

    @pl.when(jnp.logical_and(i == 0, j == 0))
    def _():
        carry[...] = prevp_ref[0]

    @pl.when(j == STEP_H)
    def _():
        def body(r, sample):
            rows, a = proj(r)
            hsc[rows, :] = a
        for_sub(body)

    @pl.when(j == STEP_C)
    def _():
        def body(r, sample):
            rows, a = proj(r)
            hin = a * hsc[rows, :]
            if not sample:
                hsc[rows, :] = _conv_rows(hin, carry[6:7, :], carry[7:8, :], cw_ref)
                carry[...] = hin[ROW_SUB - SUBLANES:, :]
                csp_ref[0] = hin[ROW_SUB - SUBLANES:, :]
            else:
                sr = cfg.dec_seq
                for s in range(ROW_SUB // sr):
                    hs = hin[s * sr:(s + 1) * sr, :]
                    st = r * (ROW_SUB // sr) + s
                    srow = pl.ds(r * ROW_SUB + s * sr, sr)
                    hsc[srow, :] = _conv_rows(hs, prevs_ref[st, 6:7, :], prevs_ref[st, 7:8, :], cw_ref)
                    css_ref[st] = hs[sr - SUBLANES:, :]
        for_sub(body)

    @pl.when(j == STEP_B)
    def _():
        def body(r, sample):
            rows, a = proj(r)
            ya_ref[rows, :] = (a * hsc[rows, :]).astype(ya_ref.dtype)
        for_sub(body)

    @pl.when(j == STEP_V)
    def _():
        @pl.when(is_prompt)
        def _():
            keep = _spatial_keep(min(cfg.seq, SGU_CHUNK))
            for g in range(SGU_GROUPS):
                wm_sc[g] = jnp.where(keep, wtp_ref[g], 0.0).astype(wm_sc.dtype)
            bt_sc[...] = btp_ref[...]

        @pl.when(is_sample)
        def _():
            keep = _spatial_keep(min(cfg.dec_seq, SGU_CHUNK))
            for g in range(SGU_GROUPS):
                wm_sc[g] = jnp.where(keep, wts_ref[g], 0.0).astype(wm_sc.dtype)
            bt_sc[...] = bts_ref[...]

        def spatial(n_sub):
            for g in range(SGU_GROUPS):
                wide = jnp.dot(wm_sc[g], vsc[:, g * n_sub * LANES:(g + 1) * n_sub * LANES],
                               preferred_element_type=F32)
                for r in range(n_sub):
                    ssc[sub_rows(r), g * LANES:(g + 1) * LANES] = wide[:, r * LANES:(r + 1) * LANES] + bt_sc[g]

        def body(r, sample):
            n_sub = (cfg.n_s if sample else cfg.tm) // ROW_SUB
            rows, a = proj(r)
            vn = _ln(jax.nn.gelu(a), lng_ref[...], lnb_ref[...])
            if sample:
                v_ref[rows, :] = vn
            vnb = vn.astype(BF16)
            for g in range(SGU_GROUPS):
                col = (g * n_sub + r) * LANES
                vsc[:, col:col + LANES] = vnb[:, g * LANES:(g + 1) * LANES]
            if r == n_sub - 1:
                spatial(n_sub)
        for_sub(body)

    @pl.when(j == STEP_U)
    def _():
        def body(r, sample):
            rows, a = proj(r)
            yb_ref[rows, :] = (jax.nn.gelu(a) * ssc[rows, :]).astype(yb_ref.dtype)
        for_sub(body)

    @pl.when(j >= STEP_GATES)
    def _():
        def body(r, sample):
            rows, a = proj(r)
            gates_ref[rows, :] = jax.nn.sigmoid(a)
        for_sub(body)


def _mixer_call(xb, w_in, layer, b_in, prev_p, prev_s, cw, lng, lnb, wt_p, bt_p, wt_s, bt_s, cfg):
    n, d = xb.shape
    cb = COL_BLOCK
    n_j = w_in.shape[2] // cb
    n_i = cfg.n_p_tiles + 1
    n_gate_blocks = 2 * cfg.ng

    def const(shape):
        return pl.BlockSpec(shape, lambda i, j: (0,) * len(shape))

    in_specs = [
        pl.BlockSpec((cfg.tm, d), lambda i, j: (i, 0)),
        pl.BlockSpec((1, d, cb), lambda i, j: (layer, 0, _mixer_block(j))),
        pl.BlockSpec((1, cb), lambda i, j: (0, _mixer_block(j))),
        const(prev_p.shape), const(prev_s.shape), const(cw.shape), const(lng.shape), const(lnb.shape),
        const(wt_p.shape), const(bt_p.shape), const(wt_s.shape), const(bt_s.shape),
    ]
    out_shape = [
        jax.ShapeDtypeStruct((n, cb), BF16),
        jax.ShapeDtypeStruct((n, cb), BF16),
        jax.ShapeDtypeStruct((n, n_gate_blocks * cb), F32),
        jax.ShapeDtypeStruct(prev_p.shape, F32),
        jax.ShapeDtypeStruct(prev_s.shape, F32),
        jax.ShapeDtypeStruct((cfg.n_s, cb), F32),
    ]
    out_specs = [
        pl.BlockSpec((cfg.tm, cb), lambda i, j: (i, 0)),
        pl.BlockSpec((cfg.tm, cb), lambda i, j: (i, 0)),
        pl.BlockSpec((cfg.tm, cb), lambda i, j: (i, jnp.clip(j - STEP_GATES, 0, n_gate_blocks - 1))),
        const(prev_p.shape), const(prev_s.shape), const((cfg.n_s, cb)),
    ]
    scratch = [
        pltpu.VMEM((cfg.tm, cb), F32),
        pltpu.VMEM((cfg.tm, cb), F32),
        pltpu.VMEM((ROW_SUB, (cfg.tm // ROW_SUB) * cb), BF16),
        pltpu.VMEM((SUBLANES, cb), F32),
        pltpu.VMEM((SGU_GROUPS, ROW_SUB, ROW_SUB), BF16),
        pltpu.VMEM((SGU_GROUPS, ROW_SUB, LANES), F32),
    ]
    assert n_j == BLK_GATES + n_gate_blocks
    return pl.pallas_call(
        functools.partial(_mixer_kernel, cfg=cfg),
        grid=(n_i, n_j),
        in_specs=in_specs,
        out_specs=out_specs,
        out_shape=out_shape,
        scratch_shapes=scratch,
        compiler_params=pltpu.CompilerParams(
            dimension_semantics=("arbitrary", "arbitrary"), vmem_limit_bytes=VMEM_LIMIT),
        name="mixer",
    )(xb, w_in, b_in, prev_p, prev_s, cw, lng, lnb, wt_p, bt_p, wt_s, bt_s)


def _merge_kernel(ya_ref, yb_ref, ga_ref, gb_ref, xa_ref, xb_ref, wa_ref, wb_ref, wo_ref, g1_ref, b1_ref,
                  rw_ref, rb_ref, x1_ref, eidx_ref, gw_ref, rank_ref, cnt_ref, run, *,
                  alpha, n_experts, n_a_tiles, split):
    i = pl.program_id(0)
    tm = xa_ref.shape[0]
    epg = n_experts // N_EXPERT_GROUPS

    @pl.when(i == 0)
    def _():
        run[...] = jnp.zeros_like(run)

    a = jnp.dot(ya_ref[...], wa_ref[0], preferred_element_type=F32)
    b = jnp.dot(yb_ref[...], wb_ref[0], preferred_element_type=F32)
    merged = ga_ref[...] * a + gb_ref[...] * b
    x = xa_ref[...]
    if split:
        x = jnp.where(i < n_a_tiles, x, xb_ref[...])
    z = alpha * x + jnp.dot(merged.astype(BF16), wo_ref[0], preferred_element_type=F32)
    x1 = _ln(z, g1_ref[...], b1_ref[...])
    x1_ref[...] = x1

    logits = jnp.dot(x1.astype(BF16), rw_ref[...], preferred_element_type=F32)
    lt = logits.T[:n_experts, :]
    ex = jnp.exp(lt - jnp.max(lt, axis=0, keepdims=True))
    sc = ex / jnp.sum(ex, axis=0, keepdims=True)
    sel = sc + rb_ref[...]
    sel_r = [sel[e:e + 1, :] for e in range(n_experts)]
    sc_r = [sc[e:e + 1, :] for e in range(n_experts)]

    def pair_max(v):
        best = None
        for p in range(len(v)):
            for q in range(p + 1, len(v)):
                s = v[p] + v[q]
                best = s if best is None else jnp.maximum(best, s)
        return best

    grp = [pair_max(sel_r[g * epg:(g + 1) * epg]) for g in range(N_EXPERT_GROUPS)]
    gi = jnp.zeros((1, tm), jnp.int32)
    gbest = grp[0]
    for g in range(1, N_EXPERT_GROUPS):
        better = grp[g] > gbest
        gi = jnp.where(better, g, gi)
        gbest = jnp.where(better, grp[g], gbest)

    def pick(rows_, idx):
        out = rows_[-1]
        for c in range(len(rows_) - 2, -1, -1):
            out = jnp.where(idx == c, rows_[c], out)
        return out

    vk = [pick([sel_r[g * epg + q] for g in range(N_EXPERT_GROUPS)], gi) for q in range(epg)]
    pk = [pick([sc_r[g * epg + q] for g in range(N_EXPERT_GROUPS)], gi) for q in range(epg)]

    i1 = jnp.zeros((1, tm), jnp.int32)
    b1 = vk[0]
    for q in range(1, epg):
        better = vk[q] > b1
        i1 = jnp.where(better, q, i1)
        b1 = jnp.where(better, vk[q], b1)
    i2 = jnp.zeros((1, tm), jnp.int32)
    b2 = jnp.full((1, tm), -jnp.inf, F32)
    for q in range(epg):
        cand = jnp.logical_and(i1 != q, vk[q] > b2)
        i2 = jnp.where(cand, q, i2)
        b2 = jnp.where(cand, vk[q], b2)
    p1 = pick(pk, i1)
    p2 = pick(pk, i2)
    den = p1 + p2
    e1 = gi * epg + i1
    e2 = gi * epg + i2

    eio = lax.broadcasted_iota(jnp.int32, (n_experts, tm), 0)
    oh1 = (eio == e1).astype(F32)
    oh2 = (eio == e2).astype(F32)
    ts = lax.broadcasted_iota(jnp.int32, (tm, tm), 0)
    tt = lax.broadcasted_iota(jnp.int32, (tm, tm), 1)
    upper = (ts <= tt).astype(BF16)
    inc1 = jnp.dot(oh1.astype(BF16), upper, preferred_element_type=F32)
    inc2 = jnp.dot(oh2.astype(BF16), upper, preferred_element_type=F32)
    tot1 = jnp.sum(oh1, axis=1, keepdims=True)
    tot2 = jnp.sum(oh2, axis=1, keepdims=True)
    base = run[:, 0:1]
    r1 = jnp.sum(oh1 * (inc1 - 1.0 + base), axis=0, keepdims=True)
    r2 = jnp.sum(oh2 * (inc2 - 1.0 + base + tot1), axis=0, keepdims=True)
    new_run = run[...] + tot1 + tot2
    run[...] = new_run
    cnt_ref[...] = new_run

    eidx_ref[...] = jnp.zeros_like(eidx_ref)
    eidx_ref[0:1, :] = e1
    eidx_ref[1:2, :] = e2
    gw_ref[...] = jnp.zeros_like(gw_ref)
    gw_ref[0:1, :] = p1 / den
    gw_ref[1:2, :] = p2 / den
    rank_ref[...] = jnp.zeros_like(rank_ref)
    rank_ref[0:1, :] = r1.astype(jnp.int32)
    rank_ref[1:2, :] = r2.astype(jnp.int32)


def _merge_call(ya, yb, gates, xa, xb, wa, wb, wo, layer, g1, b1, rw, rb, alpha, n_experts):
    n, d = ya.shape[0], xa.shape[1]
    tm = TOK_ROWS
    dc, dsg = ya.shape[1], yb.shape[1]
    n_a_tiles = xa.shape[0] // tm
    assert xa.shape[0] % tm == 0 and xb.shape[0] == tm and n_a_tiles in (n // tm, n // tm - 1)

    def rows(c):
        return pl.BlockSpec((tm, c), lambda i: (i, 0))

    def const(shape):
        return pl.BlockSpec(shape, lambda i: (0,) * len(shape))

    def lanes(r):
        return pl.BlockSpec((r, tm), lambda i: (0, i))

    def layer_block(w):
        return pl.BlockSpec((1,) + w.shape[1:], lambda i: (layer, 0, 0))

    return pl.pallas_call(
        functools.partial(_merge_kernel, alpha=alpha, n_experts=n_experts, n_a_tiles=n_a_tiles,
                          split=n_a_tiles < n // tm),
        grid=(n // tm,),
        in_specs=[rows(dc), rows(dsg), rows(d), pl.BlockSpec((tm, d), lambda i: (i, 1)),
                  pl.BlockSpec((tm, d), lambda i: (jnp.minimum(i, n_a_tiles - 1), 0)), const((tm, d)),
                  layer_block(wa), layer_block(wb), layer_block(wo), const(g1.shape), const(b1.shape),
                  const(rw.shape), const(rb.shape)],
        out_specs=[rows(d), lanes(SUBLANES), lanes(SUBLANES), lanes(SUBLANES),
                   const((n_experts, LANES))],
        out_shape=[jax.ShapeDtypeStruct((n, d), F32),
                   jax.ShapeDtypeStruct((SUBLANES, n), jnp.int32),
                   jax.ShapeDtypeStruct((SUBLANES, n), F32),
                   jax.ShapeDtypeStruct((SUBLANES, n), jnp.int32),
                   jax.ShapeDtypeStruct((n_experts, LANES), F32)],
        scratch_shapes=[pltpu.VMEM((n_experts, LANES), F32)],
        compiler_params=pltpu.CompilerParams(
            dimension_semantics=("arbitrary",), vmem_limit_bytes=VMEM_LIMIT),
        name="merge_route",
    )(ya, yb, gates, gates, xa, xb, wa, wb, wo, g1, b1, rw, rb)


def _row_gather(src_hbm, dst, sem, index_of, n_rows):
    for r in range(n_rows):
        pltpu.make_async_copy(src_hbm.at[pl.ds(index_of(r), 1)], dst.at[pl.ds(r, 1)], sem).start()


def _row_gather_wait(src_hbm, dst, sem):
    pltpu.make_async_copy(src_hbm.at[pl.ds(0, dst.shape[0])], dst, sem).wait()


def _ffn_kernel(te_ref, nu_ref, ne_ref, t0_ref, t1_ref, ws_ref, tok_ref,
                x1_hbm, wg_hbm, wu_hbm, wd_hbm, ys_ref,
                xbuf, xsem, wgub, wdb, sg, su, sd, wsem, *, layer):
    i = pl.program_id(0)
    n_used = nu_ref[0]
    tm = xbuf.shape[1]
    slot = i % (PREFETCH + 1)

    def part_copies(e, t, s):
        return (pltpu.make_async_copy(wg_hbm.at[layer, e, t], sg.at[s], wsem.at[s]),
                pltpu.make_async_copy(wu_hbm.at[layer, e, t], su.at[s], wsem.at[s]),
                pltpu.make_async_copy(wd_hbm.at[layer, e, t], sd.at[s], wsem.at[s]))

    def start_part(e, t, s):
        for c in part_copies(e, t, s):
            c.start(priority=1)

    def finish_part(e, t, s, w):
        for c in part_copies(e, t, s):
            c.wait()
        rin, rout = sg.shape[1], sd.shape[1]
        f = sg.shape[2]
        wgub[w, pl.ds(pl.multiple_of(t * rin, rin), rin), :f] = sg[s].astype(BF16)
        wgub[w, pl.ds(pl.multiple_of(t * rin, rin), rin), f:] = su[s].astype(BF16)
        wdb[w, pl.ds(pl.multiple_of(t * rout, rout), rout), :] = sd[s].astype(BF16)

    def load_parts(e, t_lo, t_hi, w):
        def step(t, c):
            s = (t - t_lo) % 2
            finish_part(e, t, s, w)

            @pl.when(t + 2 < t_hi)
            def _():
                start_part(e, t + 2, s)
            return c
        lax.fori_loop(t_lo, t_hi, step, 0)

    def start_first_parts(e, t_lo, t_hi):
        @pl.when(t_lo < t_hi)
        def _():
            start_part(e, t_lo, 0)

        @pl.when(t_lo + 1 < t_hi)
        def _():
            start_part(e, t_lo + 1, 1)

    @pl.when(i == 0)
    def _():
        for a in range(PREFETCH):
            tile = jnp.minimum(a, n_used - 1)

            def first(r, c, a=a, tile=tile):
                pltpu.make_async_copy(x1_hbm.at[pl.ds(tok_ref[tile * tm + r], 1)], xbuf.at[a, pl.ds(r, 1)],
                                      xsem.at[a]).start()
                return c
            lax.fori_loop(0, tm, first, 0)
        start_first_parts(te_ref[0], 0, W_PARTS)
        load_parts(te_ref[0], 0, W_PARTS, 0)

    @pl.when(i < n_used)
    def _():
        w = ws_ref[i]
        e_next, t_lo, t_hi = ne_ref[i], t0_ref[i], t1_ref[i]
        start_first_parts(e_next, t_lo, t_hi)
        _row_gather_wait(x1_hbm, xbuf.at[slot], xsem.at[slot])
        x = xbuf[slot].astype(BF16)
        ahead = jnp.minimum(i + PREFETCH, n_used - 1)
        into = (i + PREFETCH) % (PREFETCH + 1)
        _row_gather(x1_hbm, xbuf.at[into], xsem.at[into], lambda r: tok_ref[ahead * tm + r], tm)
        gu = jnp.dot(x, wgub[w], preferred_element_type=F32)
        g, u = gu[:, :gu.shape[1] // 2], gu[:, gu.shape[1] // 2:]
        h = ((g * jax.nn.sigmoid(g)) * u).astype(BF16)
        ys_ref[...] = jnp.dot(h, wdb[w], preferred_element_type=F32)
        load_parts(e_next, t_lo, t_hi, 1 - w)

    @pl.when(i == n_used - 1)
    def _():
        for a in range(1, PREFETCH + 1):
            s = (i + a) % (PREFETCH + 1)
            _row_gather_wait(x1_hbm, xbuf.at[s], xsem.at[s])

    @pl.when(i >= n_used)
    def _():
        ys_ref[...] = jnp.zeros_like(ys_ref)


def _ffn_call(sched, row_token, x1, w_gate, w_up, w_down, layer):
    d = x1.shape[1]
    p = row_token.shape[0]
    depth, n_e, _, f = w_gate.shape
    tm = TOK_ROWS
    wg = w_gate.reshape(depth, n_e, W_PARTS, d // W_PARTS, f)
    wu = w_up.reshape(depth, n_e, W_PARTS, d // W_PARTS, f)
    wd = w_down.reshape(depth, n_e, W_PARTS, f // W_PARTS, d)
    any_spec = pl.BlockSpec(memory_space=pl.ANY)
    return pl.pallas_call(
        functools.partial(_ffn_kernel, layer=layer),
        grid_spec=pltpu.PrefetchScalarGridSpec(
            num_scalar_prefetch=7,
            grid=(p // tm,),
            in_specs=[any_spec, any_spec, any_spec, any_spec],
            out_specs=pl.BlockSpec((tm, d), lambda i, *_: (i, 0)),
            scratch_shapes=[pltpu.VMEM((PREFETCH + 1, tm, d), F32), pltpu.SemaphoreType.DMA((PREFETCH + 1,)),
                            pltpu.VMEM((2, d, 2 * f), BF16),
                            pltpu.VMEM((2, f, d), BF16),
                            pltpu.VMEM((2, d // W_PARTS, f), F32), pltpu.VMEM((2, d // W_PARTS, f), F32),
                            pltpu.VMEM((2, f // W_PARTS, d), F32), pltpu.SemaphoreType.DMA((2,))],
        ),
        out_shape=jax.ShapeDtypeStruct((p, d), F32),
        compiler_params=pltpu.CompilerParams(
            dimension_semantics=("arbitrary",), vmem_limit_bytes=VMEM_LIMIT),
        name="expert_ffn",
    )(*sched, row_token, x1, wg, wu, wd)


def _norm2_kernel(pos_ref, x1_ref, gw_ref, g_ref, b_ref, ys_hbm, out_a, out_b, ybuf, sem, *, alpha,
                  n_p_tiles):
    i = pl.program_id(0)
    n_steps = pl.num_programs(0)
    tm = x1_ref.shape[0]
    n = tm * n_steps
    slot = i % (PREFETCH + 1)

    @pl.when(i == 0)
    def _():
        for a in range(PREFETCH):
            tile = jnp.minimum(a, n_steps - 1)

            def first(r, c, a=a, tile=tile):
                for k in range(TOP_K):
                    pltpu.make_async_copy(ys_hbm.at[pl.ds(pos_ref[k * n + tile * tm + r], 1)],
                                          ybuf.at[a, k, pl.ds(r, 1)], sem.at[a]).start()
                return c
            lax.fori_loop(0, tm, first, 0)

    for k in range(TOP_K):
        _row_gather_wait(ys_hbm, ybuf.at[slot, k], sem.at[slot])
    ahead = jnp.minimum(i + PREFETCH, n_steps - 1)
    into = (i + PREFETCH) % (PREFETCH + 1)
    for k in range(TOP_K):
        _row_gather(ys_hbm, ybuf.at[into, k], sem.at[into],
                    lambda r, k=k: pos_ref[k * n + ahead * tm + r], tm)
    y = None
    for k in range(TOP_K):
        term = gw_ref[:, k:k + 1] * ybuf[slot, k]
        y = term if y is None else y + term
    x2 = _ln(alpha * x1_ref[...] + y, g_ref[...], b_ref[...])
    if n_p_tiles is None:
        out_a[...] = x2
        out_b[...] = x2.astype(out_b.dtype)
    else:
        @pl.when(i < n_p_tiles)
        def _():
            out_a[...] = x2

        @pl.when(i >= n_p_tiles)
        def _():
            out_b[...] = x2

    @pl.when(i == n_steps - 1)
    def _():
        for a in range(1, PREFETCH + 1):
            s = (i + a) % (PREFETCH + 1)
            for k in range(TOP_K):
                _row_gather_wait(ys_hbm, ybuf.at[s, k], sem.at[s])


def _norm2_call(pos_flat, x1, gwc, g, b, ys, alpha, n_p=None):
    n, d = x1.shape
    tm = TOK_ROWS
    rows = pl.BlockSpec((tm, d), lambda i, pos: (i, 0))
    const = pl.BlockSpec((1, d), lambda i, pos: (0, 0))
    if n_p is None:
        n_p_tiles = None
        out_specs = [rows, rows]
        out_shape = [jax.ShapeDtypeStruct((n, d), F32), jax.ShapeDtypeStruct((n, d), BF16)]
    else:
        assert n - n_p == tm
        n_p_tiles = n_p // tm
        out_specs = [pl.BlockSpec((tm, d), lambda i, pos: (jnp.minimum(i, n_p_tiles - 1), 0)),
                     pl.BlockSpec((tm, d), lambda i, pos: (0, 0))]
        out_shape = [jax.ShapeDtypeStruct((n_p, d), F32), jax.ShapeDtypeStruct((tm, d), F32)]
    return pl.pallas_call(
        functools.partial(_norm2_kernel, alpha=alpha, n_p_tiles=n_p_tiles),
        grid_spec=pltpu.PrefetchScalarGridSpec(
            num_scalar_prefetch=1,
            grid=(n // tm,),
            in_specs=[rows, pl.BlockSpec((tm, LANES), lambda i, pos: (i, 0)), const, const,
                      pl.BlockSpec(memory_space=pl.ANY)],
            out_specs=out_specs,
            scratch_shapes=[pltpu.VMEM((PREFETCH + 1, TOP_K, tm, d), F32),
                            pltpu.SemaphoreType.DMA((PREFETCH + 1,))],
        ),
        out_shape=out_shape,
        compiler_params=pltpu.CompilerParams(
            dimension_semantics=("arbitrary",), vmem_limit_bytes=VMEM_LIMIT),
        name="combine_norm2",
    )(pos_flat, x1, gwc, g, b, ys)


def _invert_kernel(pos_ref, zeros_hbm, tok_ref, sem, *, n):
    clear = pltpu.make_async_copy(zeros_hbm, tok_ref, sem)
    clear.start()
    clear.wait()

    def put(t, c):
        for k in range(TOP_K):
            tok_ref[pos_ref[k * n + t]] = t
        return c
    lax.fori_loop(0, n, put, 0, unroll=16)


def _invert_call(pos_flat, n, n_rows):
    smem = pl.BlockSpec(memory_space=pltpu.SMEM)
    return pl.pallas_call(
        functools.partial(_invert_kernel, n=n),
        in_specs=[smem, pl.BlockSpec(memory_space=pl.ANY)],
        out_specs=smem,
        out_shape=jax.ShapeDtypeStruct((n_rows,), jnp.int32),
        scratch_shapes=[pltpu.SemaphoreType.DMA(())],
        name="row_tokens",
    )(pos_flat, jnp.zeros((n_rows,), jnp.int32))


def _cast_kernel(x_ref, o_ref):
    o_ref[...] = x_ref[...].astype(o_ref.dtype)


def _cast_bf16(w):
    cols = w.shape[-1]
    rows = w.size // cols
    br = rows
    while br * cols * 4 > CAST_BLOCK_BYTES and br % 32 == 0:
        br //= 2
    block = pl.BlockSpec((br, cols), lambda i: (i, 0))
    out = pl.pallas_call(
        _cast_kernel,
        grid=(rows // br,),
        in_specs=[block],
        out_specs=block,
        out_shape=jax.ShapeDtypeStruct((rows, cols), BF16),
        compiler_params=pltpu.CompilerParams(
            dimension_semantics=("arbitrary",), vmem_limit_bytes=VMEM_LIMIT),
        name="cast_weights",
    )(w.reshape(rows, cols))
    return out.reshape(w.shape)


def _stack_kernel(a_ref, b_ref, o_ref, *, n_a_tiles):
    i = pl.program_id(0)

    @pl.when(i < n_a_tiles)
    def _():
        o_ref[...] = a_ref[...].astype(o_ref.dtype)

    @pl.when(i >= n_a_tiles)
    def _():
        o_ref[:b_ref.shape[0], :] = b_ref[...].astype(o_ref.dtype)


def _stack_bf16(a, b):
    tm = min(MIX_ROWS, a.shape[0])
    d = a.shape[1]
    n_a_tiles = a.shape[0] // tm
    assert a.shape[0] % tm == 0 and b.shape[0] <= tm
    return pl.pallas_call(
        functools.partial(_stack_kernel, n_a_tiles=n_a_tiles),
        grid=(n_a_tiles + 1,),
        in_specs=[pl.BlockSpec((tm, d), lambda i: (jnp.minimum(i, n_a_tiles - 1), 0)),
                  pl.BlockSpec(b.shape, lambda i: (0, 0))],
        out_specs=pl.BlockSpec((tm, d), lambda i: (i, 0)),
        out_shape=jax.ShapeDtypeStruct((a.shape[0] + b.shape[0], d), BF16),
        compiler_params=pltpu.CompilerParams(
            dimension_semantics=("arbitrary",), vmem_limit_bytes=VMEM_LIMIT),
        name="stack_tokens",
    )(a, b)


def _spatial_tiles(w_s, b_s, seq):
    rep = (jnp.arange(ROW_SUB)[:, None] % seq == jnp.arange(seq)[None, :]).astype(F32)
    exact = lax.Precision.HIGHEST
    wt = jnp.einsum("ta,gab,sb->gts", rep, w_s[:, :seq, :seq], rep, precision=exact)
    bt = jnp.einsum("ta,ga->gt", rep, b_s[:, :seq], precision=exact)
    return wt, jnp.broadcast_to(bt[:, :, None], bt.shape + (LANES,))


def _history_rows(hist):
    return jnp.pad(hist, ((0, 0), (SUBLANES - (CONV_W - 1), 0), (0, 0)))


def _state_rows(cs):
    return cs[:, SUBLANES - (CONV_W - 1):, :]


def kernel(x_prompt, x_sample, state_conv, w_in, b_in, conv_w, conv_b, sgu_ln_g, sgu_ln_b, w_spatial,
           b_spatial, w_a_out, w_b_out, w_o, ln1_g, ln1_b, router_w, router_bias, w_gate, w_up, w_down,
           ln2_g, ln2_b):
    depth, d, n_col = w_in.shape
    batch, seq, _ = x_prompt.shape
    dec_batch, dec_seq, _ = x_sample.shape
    dc = conv_w.shape[2]
    dsg = sgu_ln_g.shape[1]
    n_experts = router_w.shape[1]
    assert batch == 1 and n_col == 3 * dc + 2 * dsg + 2 * d
    assert dc == COL_BLOCK and dsg == COL_BLOCK and d % COL_BLOCK == 0
    assert dsg == SGU_GROUPS * LANES and n_experts % N_EXPERT_GROUPS == 0
    n_p, n_s = batch * seq, dec_batch * dec_seq
    n = n_p + n_s
    tm_p = min(MIX_ROWS, n_p)
    assert n_p % tm_p == 0 and n_s == ROW_SUB and ROW_SUB % dec_seq == 0 and n_p % TOK_ROWS == 0
    assert seq % SGU_CHUNK == 0 and SGU_CHUNK % dec_seq == 0
    alpha = (2.0 * depth) ** 0.25
    cfg = MixCfg(tm_p, n_p // tm_p, seq, dec_seq, n_s, d // COL_BLOCK)

    n_pairs = TOP_K * n
    n_tiles = n_pairs // TOK_ROWS + n_experts
    assert n_pairs % TOK_ROWS == 0

    x, x_tail = x_prompt.reshape(n_p, d), x_sample.reshape(n_s, d)
    xb = _stack_bf16(x, x_tail)
    rw = jnp.pad(router_w, ((0, 0), (0, LANES - n_experts))).astype(BF16)
    rb = jnp.broadcast_to(router_bias[:, None], (n_experts, TOK_ROWS))
    zero_hist = jnp.zeros((batch, CONV_W - 1, dc), F32)

    w_in_b, w_a_b, w_b_b, w_o_b = (_cast_bf16(w) for w in (w_in, w_a_out, w_b_out, w_o))

    conv_p, conv_s, v_s = [], [], []
    for l in range(depth):
        bias = b_in[l][None, :]
        cw = jnp.concatenate([conv_w[l], conv_b[l][None, :],
                              jnp.zeros((SUBLANES - CONV_W - 1, dc), F32)], axis=0)
        lng, lnb = sgu_ln_g[l][None, :], sgu_ln_b[l][None, :]
        wt_p, bt_p = _spatial_tiles(w_spatial[l], b_spatial[l], min(seq, SGU_CHUNK))
        wt_s, bt_s = _spatial_tiles(w_spatial[l], b_spatial[l], min(dec_seq, SGU_CHUNK))

        ya, yb, gates, cs_p, cs_s, v_rows = _mixer_call(
            xb, w_in_b, l, bias, _history_rows(zero_hist), _history_rows(state_conv[l]), cw, lng, lnb,
            wt_p, bt_p, wt_s, bt_s, cfg)
        conv_p.append(_state_rows(cs_p))
        conv_s.append(_state_rows(cs_s))
        v_s.append(v_rows.reshape(dec_batch, dec_seq, dsg))

        x1, eidx, gw, rank, cnt = _merge_call(
            ya, yb, gates, x, x_tail, w_a_b, w_b_b, w_o_b, l,
            ln1_g[l][None, :], ln1_b[l][None, :], rw, rb, alpha, n_experts)

        counts = cnt[:, 0].astype(jnp.int32)
        padded = ((counts + TOK_ROWS - 1) // TOK_ROWS) * TOK_ROWS
        seg_end = jnp.cumsum(padded)
        seg_start = seg_end - padded
        n_used = (seg_end[-1:] // TOK_ROWS).astype(jnp.int32)
        tile_row0 = jnp.arange(n_tiles, dtype=jnp.int32) * TOK_ROWS
        tile_expert = jnp.minimum(
            jnp.sum(tile_row0[:, None] >= seg_end[None, :], axis=1), n_experts - 1).astype(jnp.int32)
        e_ids = jnp.arange(n_experts, dtype=jnp.int32)
        pos = rank[:TOP_K] + jnp.sum(
            jnp.where(eidx[:TOP_K, :, None] == e_ids, seg_start.astype(jnp.int32), 0), axis=-1)

        n_t = (padded // TOK_ROWS).astype(jnp.int32)
        later = jnp.where(n_t > 0, e_ids, n_experts)
        nxt_e = jnp.concatenate([jnp.flip(lax.cummin(jnp.flip(later)))[1:],
                                 jnp.full((1,), n_experts, jnp.int32)])
        group = (jnp.cumsum(n_t > 0) - 1).astype(jnp.int32)
        tile_i = jnp.arange(n_tiles, dtype=jnp.int32)
        of_tile = tile_expert[:, None] == e_ids

        def lookup(table):
            return jnp.sum(jnp.where(of_tile, table.astype(jnp.int32), 0), axis=1)

        q = tile_i - lookup(seg_start // TOK_ROWS)
        g_t = jnp.maximum(lookup(n_t), 1)
        nxt_t = lookup(nxt_e)
        loads = jnp.logical_and(tile_i < n_used[0], nxt_t < n_experts)
        t_lo = jnp.where(loads, (W_PARTS * q) // g_t, 0).astype(jnp.int32)
        t_hi = jnp.where(loads, (W_PARTS * (q + 1)) // g_t, 0).astype(jnp.int32)
        sched = (tile_expert, n_used, jnp.minimum(nxt_t, n_experts - 1).astype(jnp.int32),
                 t_lo, t_hi, (lookup(group) % 2).astype(jnp.int32))

        pos_flat = pos.reshape(-1)
        row_token = _invert_call(pos_flat, n, n_tiles * TOK_ROWS)
        ys = _ffn_call(sched, row_token, x1, w_gate, w_up, w_down, l)
        gwc = jnp.pad(gw[:TOP_K].T, ((0, 0), (0, LANES - TOP_K)))
        x, xb = _norm2_call(pos_flat, x1, gwc, ln2_g[l][None, :], ln2_b[l][None, :], ys, alpha,
                            n_p if l == depth - 1 else None)
        x_tail = x[:TOK_ROWS]

    return (x.reshape(batch, seq, d), xb.reshape(dec_batch, dec_seq, d),
            jnp.stack(conv_p), jnp.stack(conv_s), jnp.stack(v_s))
```
